```python
import jax
import jax.numpy as jnp
from jax import lax
import numpy as np

D_MODEL = 1024
BATCH = 8
SEQ = 2048
DEPTH = 2
DEC_BATCH = 128
DEC_SEQ = 4
PAST_LEN = 16384
PAGE_SIZE = 128

N_EVEN = (DEPTH + 1) // 2
N_ODD = DEPTH // 2
PLE_DIM = 256
NORM_EPS = 1e-6
HG_HEADS = 4
HG_DK = 128
HG_DV = 128
HG_WIDTH = HG_HEADS * HG_DK
HG_CHUNK = 64
MLA_HEADS = 8
Q_LORA = 384
KV_LORA = 256
NOPE_DIM = 64
ROPE_DIM = 32
V_DIM = 64
MLA_WIDTH = MLA_HEADS * V_DIM
MLA_SCALE = (NOPE_DIM + ROPE_DIM) ** -0.5
ROPE_THETA = 10000.0
Q_BLOCK = 128
IN_SPLITS = (HG_WIDTH, HG_WIDTH, HG_HEADS * HG_DV, HG_HEADS * HG_DV, Q_LORA, KV_LORA, ROPE_DIM, MLA_WIDTH)
IN_WIDTH = sum(IN_SPLITS)
IN_OFFSETS = tuple(int(v) for v in np.cumsum(IN_SPLITS)[:-1])
EVEN_MIX_WIDTH = HG_HEADS * HG_DV + MLA_WIDTH
RW_N = 64
RW_HEADS = D_MODEL // RW_N
DECAY_LORA = 64
AAA_LORA = 64
RW_EPS = 64e-5

kernel_name = 'hgrn2_mla_rwkv7_hybrid_step'

F32 = jnp.float32


def rmsnorm(x, g, eps=NORM_EPS):
    xf = x.astype(F32)
    y = xf * lax.rsqrt(jnp.mean(xf * xf, axis=-1, keepdims=True) + eps)
    return (y * g.astype(F32)).astype(x.dtype)


def rope(x, pos):
    half = ROPE_DIM // 2
    inv = ROPE_THETA ** (-jnp.arange(half, dtype=F32) / half)
    ang = pos.astype(F32)[:, None] * inv[None, :]
    shape = (1, x.shape[1]) + (1,) * (x.ndim - 3) + (half,)
    cos = jnp.cos(ang).reshape(shape).astype(x.dtype)
    sin = jnp.sin(ang).reshape(shape).astype(x.dtype)
    x1, x2 = x[..., :half], x[..., half:]
    return jnp.concatenate([x1 * cos - x2 * sin, x2 * cos + x1 * sin], axis=-1)


def hgrn2_chunked(q, k, v, logf, s0):
    B, T, H, _ = q.shape
    C = HG_CHUNK if T % HG_CHUNK == 0 else T
    nc = T // C
    chunks = lambda z: z.reshape(B, nc, C, H, z.shape[-1]).transpose(1, 0, 3, 2, 4)
    causal = jnp.tril(jnp.ones((C, C), bool))

    def step(S, inp):
        qc, kc, vc, gc = inp
        b = jnp.cumsum(gc, axis=2)
        diff = b[:, :, :, None, :] - b[:, :, None, :, :]
        dec = jnp.exp(jnp.where(causal[:, :, None], diff, -jnp.inf))
        att = jnp.einsum('bhtk,bhsk,bhtsk->bhts', qc, kc, dec)
        o = jnp.einsum('bhts,bhsv->bhtv', att, vc) + jnp.einsum('bhtk,bhkv->bhtv', qc * jnp.exp(b), S)
        b_end = b[:, :, -1:, :]
        S = jnp.exp(b_end[:, :, 0, :, None]) * S + jnp.einsum('bhsk,bhsv->bhkv', kc * jnp.exp(b_end - b), vc)
        return S, o

    s_last, o = lax.scan(step, s0.astype(F32), (chunks(q), chunks(k), chunks(v), chunks(logf)))
    o = o.transpose(1, 0, 3, 2, 4).reshape(B, T, H, v.shape[-1])
    return o, s_last


def attend_prompt(q_lat, q_rope, ckv, kr):
    B, S, H, L = q_lat.shape
    nb = S // Q_BLOCK
    blocks = lambda z: jnp.moveaxis(z.reshape(B, nb, Q_BLOCK, H, z.shape[-1]), 1, 0)
    kpos = jnp.arange(S)

    def one_block(args):
        ql, qr, start = args
        s = (jnp.einsum('bthl,bsl->bhts', ql, ckv) + jnp.einsum('bthr,bsr->bhts', qr, kr)).astype(F32) * MLA_SCALE
        qpos = start + jnp.arange(Q_BLOCK)
        s = jnp.where(kpos[None, :] <= qpos[:, None], s, -jnp.inf)
        pr = jax.nn.softmax(s, axis=-1).astype(ckv.dtype)
        return jnp.einsum('bhts,bsl->bthl', pr, ckv)

    ctx = lax.map(one_block, (blocks(q_lat), blocks(q_rope), jnp.arange(nb) * Q_BLOCK))
    return jnp.moveaxis(ctx, 0, 1).reshape(B, S, H, L)


def attend_sample(q_lat, q_rope, ckv, kr, past_ckv, past_kr):
    T = q_lat.shape[1]
    P = past_ckv.shape[1]
    s_past = (jnp.einsum('bthl,bpl->bhtp', q_lat, past_ckv) + jnp.einsum('bthr,bpr->bhtp', q_rope, past_kr)).astype(F32) * MLA_SCALE
    s_new = (jnp.einsum('bthl,bsl->bhts', q_lat, ckv) + jnp.einsum('bthr,bsr->bhts', q_rope, kr)).astype(F32) * MLA_SCALE
    s_new = jnp.where(jnp.tril(jnp.ones((T, T), bool)), s_new, -jnp.inf)
    pr = jax.nn.softmax(jnp.concatenate([s_past, s_new], axis=-1), axis=-1).astype(ckv.dtype)
    return jnp.einsum('bhtp,bpl->bthl', pr[..., :P], past_ckv) + jnp.einsum('bhts,bsl->bthl', pr[..., P:], ckv)


def even_mixer(hn, pos, lb, W, j, hg_s0, past):
    B, T, _ = hn.shape
    hq, hf, hi, hgate, cq, ckv, kr, mgate = jnp.split(hn @ W['ev_w_in'][j], IN_OFFSETS, axis=-1)
    lb = lb.reshape(HG_HEADS, HG_DK)
    f = lb + (1.0 - lb) * jax.nn.sigmoid(hf.astype(F32).reshape(B, T, HG_HEADS, HG_DK))
    q = jax.nn.silu(hq.astype(F32)).reshape(B, T, HG_HEADS, HG_DK)
    i_in = hi.astype(F32).reshape(B, T, HG_HEADS, HG_DV)
    o_hg, s_hg = hgrn2_chunked(q, 1.0 - f, i_in, jnp.log(f), hg_s0)
    o_hg = rmsnorm(o_hg, W['hg_norm'][j]).reshape(B, T, HG_HEADS * HG_DV).astype(hn.dtype) * jax.nn.silu(hgate)
    qfull = (rmsnorm(cq, W['mla_q_norm'][j]) @ W['mla_w_uq'][j]).reshape(B, T, MLA_HEADS, NOPE_DIM + ROPE_DIM)
    q_nope = qfull[..., :NOPE_DIM]
    q_rope = rope(qfull[..., NOPE_DIM:], pos)
    ckv = rmsnorm(ckv, W['mla_kv_norm'][j])
    kr = rope(kr, pos)
    q_lat = jnp.einsum('bthn,hnl->bthl', q_nope, W['mla_w_uk'][j])
    if past is None:
        ctx = attend_prompt(q_lat, q_rope, ckv, kr)
    else:
        ctx = attend_sample(q_lat, q_rope, ckv, kr, past[0], past[1])
    o_mla = jnp.einsum('bthl,hvl->bthv', ctx, W['mla_w_uv'][j]).reshape(B, T, MLA_WIDTH) * jax.nn.silu(mgate)
    out = jnp.concatenate([o_hg, o_mla], axis=-1) @ W['ev_w_out'][j]
    return out, ckv, kr, s_hg


def wkv_scan(r, w, k, v, a, b, s0):
    def step(S, inp):
        r_t, w_t, k_t, v_t, a_t, b_t = inp
        sa = jnp.einsum('bhvk,bhk->bhv', S, a_t)
        S = S * w_t[:, :, None, :] + sa[..., None] * b_t[:, :, None, :] + v_t[..., None] * k_t[:, :, None, :]
        return S, jnp.einsum('bhvk,bhk->bhv', S, r_t)

    xs = tuple(jnp.moveaxis(z, 1, 0) for z in (r, w, k, v, a, b))
    s_last, y = lax.scan(step, s0.astype(F32), xs)
    return jnp.moveaxis(y, 0, 1), s_last


def rwkv7_mixer(hn, shift0, wkv0, W, j):
    B, T, D = hn.shape
    prev = jnp.concatenate([shift0[:, None, :].astype(hn.dtype), hn[:, :-1]], axis=1)
    xs = hn[None] + (prev - hn)[None] * W['rw_mu'][j][:, None, None, :]
    r, k, v, g = jnp.einsum('jbtd,jde->jbte', xs[:4], W['rw_w_rkvg'][j])
    w_log = -jax.nn.softplus(-(W['rw_w0'][j] + jnp.tanh(xs[4] @ W['rw_w1'][j]) @ W['rw_w2'][j]).astype(F32)) - 0.5
    decay = jnp.exp(-jnp.exp(w_log))
    a = jax.nn.sigmoid((W['rw_a0'][j] + (xs[5] @ W['rw_a1'][j]) @ W['rw_a2'][j]).astype(F32))
    heads = lambda z: z.reshape(B, T, RW_HEADS, RW_N)
    kf = k.astype(F32)
    kk = heads(kf * W['rw_k_k'][j].astype(F32))
    kk = kk / jnp.maximum(jnp.sqrt(jnp.sum(kk * kk, axis=-1, keepdims=True)), 1e-12)
    k_mod = heads(kf * (1.0 + (a - 1.0) * W['rw_k_a'][j].astype(F32)))
    rf, vf = heads(r.astype(F32)), heads(v.astype(F32))
    y, s_wkv = wkv_scan(rf, heads(decay), k_mod, vf, -kk, kk * heads(a), wkv0)
    mean = jnp.mean(y, axis=-1, keepdims=True)
    var = jnp.mean(jnp.square(y - mean), axis=-1, keepdims=True)
    y = ((y - mean) * lax.rsqrt(var + RW_EPS)).reshape(B, T, D) * W['rw_ln_w'][j].astype(F32) + W['rw_ln_b'][j].astype(F32)
    bonus = jnp.sum(rf * k_mod * W['rw_r_k'][j].astype(F32), axis=-1, keepdims=True) * vf
    y = (y + bonus.reshape(B, T, D)).astype(hn.dtype) * jax.nn.silu(g)
    return y @ W['rw_w_o'][j], s_wkv, hn[:, -1]


def trunk(x, p, pos, W, hg_s0, wkv_s0, shift_s0, cache_ckv=None, cache_krope=None, page_table=None):
    B = x.shape[0]
    lb_all = jnp.cumsum(jax.nn.softmax(W['hg_lb_logits'].astype(F32), axis=0), axis=0)
    h = x
    ckvs, krs, hgs, wkvs, shifts = [], [], [], [], []
    for i in range(DEPTH):
        j = i // 2
        hn = rmsnorm(h, W['mix_norm'][i])
        if i % 2 == 0:
            past = None
            if page_table is not None:
                past = (cache_ckv[j, page_table].reshape(B, -1, KV_LORA),
                        cache_krope[j, page_table].reshape(B, -1, ROPE_DIM))
            out, ckv, kr, s_hg = even_mixer(hn, pos, lb_all[i], W, j, hg_s0[j], past)
            ckvs.append(ckv)
            krs.append(kr)
            hgs.append(s_hg)
        else:
            out, s_wkv, last = rwkv7_mixer(hn, shift_s0[j], wkv_s0[j], W, j)
            wkvs.append(s_wkv)
            shifts.append(last)
        h = h + out
        gate = jax.nn.sigmoid(rmsnorm(h, W['ple_norm'][i]) @ W['ple_gate'][i])
        h = h + gate * (p[i] @ W['ple_proj'][i])
    y = rmsnorm(h, W['final_norm'])
    return y, jnp.stack(ckvs), jnp.stack(krs), jnp.stack(hgs), jnp.stack(wkvs), jnp.stack(shifts)


def setup_inputs(seed: int = 0) -> dict:
    key = jax.random.key(seed)
    ks = iter(jax.random.split(key, 64))
    nrm = lambda shape, scale: scale * jax.random.normal(next(ks), shape, F32)
    n_pages = PAST_LEN // PAGE_SIZE
    n_used = DEC_BATCH * n_pages
    n_pool = n_used + n_used // 4
    D = D_MODEL
    return {
        'x_prompt': nrm((BATCH, SEQ, D), 1.0),
        'x_sample': nrm((DEC_BATCH, DEC_SEQ, D), 1.0),
        'cache_ckv': nrm((N_EVEN, n_pool, PAGE_SIZE, KV_LORA), 1.0),
        'cache_krope': nrm((N_EVEN, n_pool, PAGE_SIZE, ROPE_DIM), 1.0),
        'state_hgrn': nrm((N_EVEN, DEC_BATCH, HG_HEADS, HG_DK, HG_DV), 0.3),
        'state_wkv': nrm((N_ODD, DEC_BATCH, RW_HEADS, RW_N, RW_N), 0.3),
        'state_shift': nrm((N_ODD, DEC_BATCH, D), 1.0),
        'page_table': jax.random.permutation(next(ks), n_pool)[:n_used].reshape(DEC_BATCH, n_pages).astype(jnp.int32),
        'p_prompt': nrm((DEPTH, BATCH, SEQ, PLE_DIM), 1.0),
        'p_sample': nrm((DEPTH, DEC_BATCH, DEC_SEQ, PLE_DIM), 1.0),
        'mix_norm': 1.0 + nrm((DEPTH, D), 0.02),
        'ev_w_in': nrm((N_EVEN, D, IN_WIDTH), D ** -0.5),
        'hg_lb_logits': nrm((DEPTH + 1, HG_WIDTH), 0.1),
        'hg_norm': 1.0 + nrm((N_EVEN, HG_DV), 0.02),
        'mla_q_norm': 1.0 + nrm((N_EVEN, Q_LORA), 0.02),
        'mla_w_uq': nrm((N_EVEN, Q_LORA, MLA_HEADS * (NOPE_DIM + ROPE_DIM)), Q_LORA ** -0.5),
        'mla_kv_norm': 1.0 + nrm((N_EVEN, KV_LORA), 0.02),
        'mla_w_uk': nrm((N_EVEN, MLA_HEADS, NOPE_DIM, KV_LORA), KV_LORA ** -0.5),
        'mla_w_uv': nrm((N_EVEN, MLA_HEADS, V_DIM, KV_LORA), KV_LORA ** -0.5),
        'ev_w_out': nrm((N_EVEN, EVEN_MIX_WIDTH, D), EVEN_MIX_WIDTH ** -0.5),
        'rw_mu': jax.random.uniform(next(ks), (N_ODD, 6, D), F32),
        'rw_w_rkvg': nrm((N_ODD, 4, D, D), D ** -0.5),
        'rw_w0': nrm((N_ODD, D), 0.5),
        'rw_w1': nrm((N_ODD, D, DECAY_LORA), D ** -0.5),
        'rw_w2': nrm((N_ODD, DECAY_LORA, D), 0.1 * DECAY_LORA ** -0.5),
        'rw_a0': nrm((N_ODD, D), 0.1),
        'rw_a1': nrm((N_ODD, D, AAA_LORA), D ** -0.5),
        'rw_a2': nrm((N_ODD, AAA_LORA, D), 0.1 * AAA_LORA ** -0.5),
        'rw_k_k': 0.85 + nrm((N_ODD, D), 0.05),
        'rw_k_a': 1.0 + nrm((N_ODD, D), 0.05),
        'rw_r_k': nrm((N_ODD, RW_HEADS, RW_N), 0.1),
        'rw_ln_w': 1.0 + nrm((N_ODD, D), 0.02),
        'rw_ln_b': nrm((N_ODD, D), 0.02),
        'rw_w_o': nrm((N_ODD, D, D), D ** -0.5),
        'ple_norm': 1.0 + nrm((DEPTH, D), 0.02),
        'ple_gate': nrm((DEPTH, D, D), D ** -0.5),
        'ple_proj': nrm((DEPTH, PLE_DIM, D), PLE_DIM ** -0.5),
        'final_norm': 1.0 + nrm((D,), 0.02),
    }


def reference(x_prompt, x_sample, cache_ckv, cache_krope, state_hgrn, state_wkv, state_shift, page_table,
              p_prompt, p_sample, mix_norm, ev_w_in, hg_lb_logits, hg_norm, mla_q_norm, mla_w_uq, mla_kv_norm,
              mla_w_uk, mla_w_uv, ev_w_out, rw_mu, rw_w_rkvg, rw_w0, rw_w1, rw_w2, rw_a0, rw_a1, rw_a2,
              rw_k_k, rw_k_a, rw_r_k, rw_ln_w, rw_ln_b, rw_w_o, ple_norm, ple_gate, ple_proj, final_norm):
    W = dict(mix_norm=mix_norm, ev_w_in=ev_w_in, hg_lb_logits=hg_lb_logits, hg_norm=hg_norm,
             mla_q_norm=mla_q_norm, mla_w_uq=mla_w_uq, mla_kv_norm=mla_kv_norm, mla_w_uk=mla_w_uk,
             mla_w_uv=mla_w_uv, ev_w_out=ev_w_out, rw_mu=rw_mu, rw_w_rkvg=rw_w_rkvg, rw_w0=rw_w0,
             rw_w1=rw_w1, rw_w2=rw_w2, rw_a0=rw_a0, rw_a1=rw_a1, rw_a2=rw_a2, rw_k_k=rw_k_k, rw_k_a=rw_k_a,
             rw_r_k=rw_r_k, rw_ln_w=rw_ln_w, rw_ln_b=rw_ln_b, rw_w_o=rw_w_o, ple_norm=ple_norm,
             ple_gate=ple_gate, ple_proj=ple_proj, final_norm=final_norm)
    Bp, Tp, _ = x_prompt.shape
    Ts = x_sample.shape[1]
    hg0 = jnp.zeros((N_EVEN, Bp, HG_HEADS, HG_DK, HG_DV), F32)
    wkv0 = jnp.zeros((N_ODD, Bp, RW_HEADS, RW_N, RW_N), F32)
    sh0 = jnp.zeros((N_ODD, Bp, D_MODEL), x_prompt.dtype)
    y_prompt, ckv_prompt, krope_prompt, hgrn_prompt, wkv_prompt, shift_prompt = trunk(
        x_prompt, p_prompt, jnp.arange(Tp), W, hg0, wkv0, sh0)
    y_sample, ckv_sample, krope_sample, hgrn_sample, wkv_sample, shift_sample = trunk(
        x_sample, p_sample, PAST_LEN + jnp.arange(Ts), W, state_hgrn, state_wkv, state_shift,
        cache_ckv, cache_krope, page_table)
    return (y_prompt, y_sample, ckv_prompt, krope_prompt, ckv_sample, krope_sample,
            hgrn_prompt, hgrn_sample, wkv_prompt, wkv_sample, shift_prompt, shift_sample)
```

```python
import functools

import jax
import jax.numpy as jnp
import numpy as np
from jax import lax
from jax.experimental import pallas as pl
from jax.experimental.pallas import tpu as pltpu

F32 = jnp.float32
BF16 = jnp.bfloat16

NORM_EPS = 1e-6
HG_HEADS = 4
HG_DK = 128
HG_DV = 128
HG_WIDTH = HG_HEADS * HG_DK
MLA_HEADS = 8
Q_LORA = 384
KV_LORA = 256
NOPE_DIM = 64
ROPE_DIM = 32
V_DIM = 64
MLA_WIDTH = MLA_HEADS * V_DIM
MLA_SCALE = (NOPE_DIM + ROPE_DIM) ** -0.5
ROPE_THETA = 10000.0
RW_N = 64
RW_EPS = 64e-5

LANES = 128
SUBLANES = 8
MXU_DIM = 256
VMEM_LIMIT_BYTES = 56 * 1024 * 1024

ROW_TILE = 256
HG_CHUNK = 64
HG_BLOCK = SUBLANES
ATT_BLOCK = 256
PAGES_PER_STEP = 8
SCAN_TIME_BLOCK = 32


def _cparams(sem):
    return pltpu.CompilerParams(dimension_semantics=sem, vmem_limit_bytes=VMEM_LIMIT_BYTES)


def _rms(x, g):
    return x * lax.rsqrt(jnp.mean(x * x, axis=-1, keepdims=True) + NORM_EPS) * g


def _sigmoid(x):
    return 1.0 / (1.0 + jnp.exp(-x))


def _silu(x):
    return x * _sigmoid(x)


def _dot(a, b):
    return jnp.dot(a, b, preferred_element_type=F32)


def _dot_nt(a, b):
    return lax.dot_general(a, b, (((1,), (1,)), ((), ())), preferred_element_type=F32)


def _head_sum(x, bd):
    hi = x.astype(BF16)
    lo = (x - hi.astype(F32)).astype(BF16)
    outs = []
    for c in range(x.shape[-1] // MXU_DIM):
        sl = slice(c * MXU_DIM, (c + 1) * MXU_DIM)
        outs.append(_dot(hi[:, sl], bd) + _dot(lo[:, sl], bd))
    return jnp.concatenate(outs, axis=-1)


def _full(shape):
    nd = len(shape)
    return pl.BlockSpec(shape, lambda *_: (0,) * nd)


def _even_pre_kernel(x_ref, cos_ref, sin_ref, g_ref, lbl_ref, w1_ref, w2_ref, qg_ref, kvg_ref, wuq_ref,
                     wuk_ref, zhg_ref, qlat_ref, qrope_ref, ckv_ref, kr_ref, ckvb_ref, krb_ref, mg_ref,
                     *, lb_rows):
    hn = _rms(x_ref[...], g_ref[...]).astype(BF16)
    z1 = _dot(hn, w1_ref[...])
    z2 = _dot(hn, w2_ref[...])
    lg = lbl_ref[...]
    e = jnp.exp(lg - jnp.max(lg, axis=0, keepdims=True))
    p = e / jnp.sum(e, axis=0, keepdims=True)
    lb = jnp.sum(p[:lb_rows], axis=0, keepdims=True)
    W = HG_WIDTH
    f = lb + (1.0 - lb) * _sigmoid(z1[:, W:2 * W])
    zhg_ref[:, 0:W] = _silu(z1[:, 0:W])
    zhg_ref[:, W:2 * W] = jnp.log(f)
    zhg_ref[:, 2 * W:3 * W] = z1[:, 2 * W:3 * W]
    zhg_ref[:, 3 * W:4 * W] = _silu(z1[:, 3 * W:4 * W])
    o_kv = Q_LORA
    o_mg = o_kv + KV_LORA
    o_kr = o_mg + MLA_WIDTH
    o_krs = o_kr + ROPE_DIM
    cqn = _rms(z2[:, 0:o_kv], qg_ref[...]).astype(BF16)
    qf = _dot(cqn, wuq_ref[...])
    cos = cos_ref[...]
    sin = sin_ref[...]
    n_nope = MLA_HEADS * NOPE_DIM
    n_rope = MLA_HEADS * ROPE_DIM
    qr = (qf[:, n_nope:n_nope + n_rope] * cos + qf[:, n_nope + n_rope:n_nope + 2 * n_rope] * sin) * MLA_SCALE
    for h in range(MLA_HEADS):
        qrope_ref[h] = qr[:, h * ROPE_DIM:(h + 1) * ROPE_DIM].astype(BF16)
        qn = qf[:, h * NOPE_DIM:(h + 1) * NOPE_DIM].astype(BF16)
        qlat_ref[h] = (_dot(qn, wuk_ref[h]) * MLA_SCALE).astype(BF16)
    ckv = _rms(z2[:, o_kv:o_mg], kvg_ref[...])
    ckv_ref[...] = ckv
    ckvb_ref[...] = ckv.astype(BF16)
    kr = z2[:, o_kr:o_krs] * cos[:, :ROPE_DIM] + z2[:, o_krs:o_krs + ROPE_DIM] * sin[:, :ROPE_DIM]
    kr_ref[...] = kr
    krb_ref[...] = kr.astype(BF16)
    mg_ref[...] = _silu(z2[:, o_mg:o_kr])


def _even_pre(x2d, cos_tab, sin_tab, g, lb_logits, w1, w2, qg, kvg, wuq, wuk, *, lb_rows):
    n, d = x2d.shape
    tm = min(ROW_TILE, n)
    n_tab = cos_tab.shape[0] // tm
    row = lambda i: (i, 0)
    tab = lambda i: (i % n_tab, 0)
    out_shape = (
        jax.ShapeDtypeStruct((n, 4 * HG_WIDTH), F32),
        jax.ShapeDtypeStruct((MLA_HEADS, n, KV_LORA), BF16),
        jax.ShapeDtypeStruct((MLA_HEADS, n, ROPE_DIM), BF16),
        jax.ShapeDtypeStruct((n, KV_LORA), F32),
        jax.ShapeDtypeStruct((n, ROPE_DIM), F32),
        jax.ShapeDtypeStruct((n, KV_LORA), BF16),
        jax.ShapeDtypeStruct((n, ROPE_DIM), BF16),
        jax.ShapeDtypeStruct((n, MLA_WIDTH), F32),
    )
    out_specs = (
        pl.BlockSpec((tm, 4 * HG_WIDTH), row),
        pl.BlockSpec((MLA_HEADS, tm, KV_LORA), lambda i: (0, i, 0)),
        pl.BlockSpec((MLA_HEADS, tm, ROPE_DIM), lambda i: (0, i, 0)),
        pl.BlockSpec((tm, KV_LORA), row),
        pl.BlockSpec((tm, ROPE_DIM), row),
        pl.BlockSpec((tm, KV_LORA), row),
        pl.BlockSpec((tm, ROPE_DIM), row),
        pl.BlockSpec((tm, MLA_WIDTH), row),
    )
    in_specs = [
        pl.BlockSpec((tm, d), row),
        pl.BlockSpec((tm, cos_tab.shape[1]), tab),
        pl.BlockSpec((tm, sin_tab.shape[1]), tab),
        _full(g.shape), _full(lb_logits.shape), _full(w1.shape), _full(w2.shape), _full(qg.shape),
        _full(kvg.shape), _full(wuq.shape), _full(wuk.shape),
    ]
    return pl.pallas_call(
        functools.partial(_even_pre_kernel, lb_rows=lb_rows),
        grid=(n // tm,), in_specs=in_specs, out_specs=out_specs, out_shape=out_shape,
        compiler_params=_cparams(("parallel",)), name="even_pre",
    )(x2d, cos_tab, sin_tab, g, lb_logits, w1, w2, qg, kvg, wuq, wuk)


def _tril_ones(n, block):
    r = lax.broadcasted_iota(jnp.int32, (n, n), 0)
    c = lax.broadcasted_iota(jnp.int32, (n, n), 1)
    return ((r >= c) & ((r // block) == (c // block))).astype(F32)


def _hgrn_exact_blocks(q, k, v, b, block):
    rows = q.shape[0]
    rid = lax.broadcasted_iota(jnp.int32, (rows, 1), 0) % block
    o = jnp.sum(q * k, axis=-1, keepdims=True) * v
    for d in range(1, block):
        kd = pltpu.roll(k, d, 0)
        bd = pltpu.roll(b, d, 0)
        vd = pltpu.roll(v, d, 0)
        w = jnp.sum(q * kd * jnp.exp(jnp.minimum(b - bd, 0.0)), axis=-1, keepdims=True)
        o = o + jnp.where(rid >= d, w, 0.0) * vd
    return o


def _hgrn_finish(o, g, gate):
    return (o * lax.rsqrt(jnp.mean(o * o, axis=-1, keepdims=True) + NORM_EPS) * g * gate).astype(BF16)


def _hgrn_prompt_kernel(q_ref, lf_ref, v_ref, gt_ref, g_ref, o_ref, s_ref):
    C = HG_CHUNK
    T = q_ref.shape[0]
    tril = _tril_ones(C, C)
    rid = lax.broadcasted_iota(jnp.int32, (C, 1), 0)
    rr = lax.broadcasted_iota(jnp.int32, (C, C), 0)
    cc = lax.broadcasted_iota(jnp.int32, (C, C), 1)
    g = g_ref[...]

    def chunk(c, st):
        sl = pl.ds(pl.multiple_of(c * C, C), C)
        q = q_ref[sl, :]
        lf = lf_ref[sl, :]
        v = v_ref[sl, :]
        k = 1.0 - jnp.exp(lf)
        b = jnp.dot(tril, lf, precision=lax.Precision.HIGHEST, preferred_element_type=F32)
        o = _dot_nt((q * jnp.exp(b)).astype(BF16), st.astype(BF16))
        o = o + _hgrn_exact_blocks(q, k, v, b, HG_BLOCK)
        att = jnp.zeros((C, C), F32)
        m = HG_BLOCK
        while 2 * m <= C:
            nb = C // (2 * m)
            b3 = b.reshape(nb, 2 * m, HG_DK)
            ref = jnp.broadcast_to(b3[:, m - 1:m, :], (nb, 2 * m, HG_DK)).reshape(C, HG_DK)
            upper = (rid % (2 * m)) >= m
            qt = q * jnp.where(upper, jnp.exp(jnp.minimum(b - ref, 0.0)), 0.0)
            kt = k * jnp.where(upper, 0.0, jnp.exp(jnp.minimum(ref - b, 0.0)))
            a = _dot_nt(qt.astype(BF16), kt.astype(BF16))
            if 2 * m < C:
                a = jnp.where((rr // (2 * m)) == (cc // (2 * m)), a, 0.0)
            att = att + a
            m *= 2
        o = o + _dot(att.astype(BF16), v.astype(BF16))
        o_ref[sl, :] = _hgrn_finish(o, g, gt_ref[sl, :])
        bend = b[C - 1:C, :]
        kt = k * jnp.exp(bend - b)
        return st * jnp.exp(bend) + _dot(v.T.astype(BF16), kt.astype(BF16))

    st = lax.fori_loop(0, T // C, chunk, jnp.zeros((HG_DV, HG_DK), F32))
    s_ref[...] = st.T


def _hgrn_prompt(zhg, g, batch, seq):
    blk = lambda off: pl.BlockSpec((seq, HG_DK), lambda b, h: (b, off + h))
    return pl.pallas_call(
        _hgrn_prompt_kernel,
        grid=(batch, HG_HEADS),
        in_specs=[blk(0), blk(HG_HEADS), blk(2 * HG_HEADS), blk(3 * HG_HEADS), _full(g.shape)],
        out_specs=(pl.BlockSpec((seq, HG_DV), lambda b, h: (b, h)),
                   pl.BlockSpec((None, None, HG_DK, HG_DV), lambda b, h: (b, h, 0, 0))),
        out_shape=(jax.ShapeDtypeStruct((batch * seq, HG_HEADS * HG_DV), BF16),
                   jax.ShapeDtypeStruct((batch, HG_HEADS, HG_DK, HG_DV), F32)),
        compiler_params=_cparams(("parallel", "parallel")), name="hgrn_prompt",
    )(zhg, zhg, zhg, zhg, g)


def _hgrn_sample_kernel(z_ref, s0_ref, g_ref, o_ref, s_ref, *, seq):
    rows = z_ref.shape[0]
    nb = rows // seq
    tril = _tril_ones(rows, seq)
    rb = lax.broadcasted_iota(jnp.int32, (rows, 1), 0) // seq
    cb = lax.broadcasted_iota(jnp.int32, (1, rows), 1) // seq
    g = g_ref[...]
    W = HG_WIDTH
    for h in range(HG_HEADS):
        hs = slice(h * HG_DK, (h + 1) * HG_DK)
        q = z_ref[:, hs]
        lf = z_ref[:, W + h * HG_DK:W + (h + 1) * HG_DK]
        v = z_ref[:, 2 * W + h * HG_DV:2 * W + (h + 1) * HG_DV]
        gate = z_ref[:, 3 * W + h * HG_DV:3 * W + (h + 1) * HG_DV]
        k = 1.0 - jnp.exp(lf)
        b = jnp.dot(tril, lf, precision=lax.Precision.HIGHEST, preferred_element_type=F32)
        o = _hgrn_exact_blocks(q, k, v, b, seq)
        qe = (q * jnp.exp(b)).astype(BF16)
        bt = b.T
        vb = v.astype(BF16)
        for i in range(nb):
            s0 = s0_ref[i, h]
            o = o + jnp.where(rb == i, _dot(qe, s0.astype(BF16)), 0.0)
            last = i * seq + seq - 1
            bend_row = b[last:last + 1, :]
            kt = jnp.where(rb == i, k * jnp.exp(bend_row - b), 0.0)
            ktt = kt.T.astype(BF16)
            s_ref[i, h] = s0 * jnp.exp(bt[:, last:last + 1]) + _dot(ktt, vb)
        o_ref[:, hs] = _hgrn_finish(o, g, gate)


def _hgrn_sample(zhg, s0, g, seq):
    n = zhg.shape[0]
    nb = 8
    rows = nb * seq
    return pl.pallas_call(
        functools.partial(_hgrn_sample_kernel, seq=seq),
        grid=(n // rows,),
        in_specs=[pl.BlockSpec((rows, 4 * HG_WIDTH), lambda i: (i, 0)),
                  pl.BlockSpec((nb, HG_HEADS, HG_DK, HG_DV), lambda i: (i, 0, 0, 0)),
                  _full(g.shape)],
        out_specs=(pl.BlockSpec((rows, HG_HEADS * HG_DV), lambda i: (i, 0)),
                   pl.BlockSpec((nb, HG_HEADS, HG_DK, HG_DV), lambda i: (i, 0, 0, 0))),
        out_shape=(jax.ShapeDtypeStruct((n, HG_HEADS * HG_DV), BF16),
                   jax.ShapeDtypeStruct(s0.shape, F32)),
        compiler_params=_cparams(("parallel",)), name="hgrn_sample",
    )(zhg, s0, g)


def _online_update(s, kv, m_ref, l_ref, acc_ref):
    m_old = m_ref[...]
    m_new = jnp.maximum(m_old, jnp.max(s, axis=-1, keepdims=True))
    alpha = jnp.exp(m_old - m_new)
    p = jnp.exp(s - m_new)
    l_ref[...] = alpha * l_ref[...] + jnp.sum(p, axis=-1, keepdims=True)
    acc_ref[...] = alpha * acc_ref[...] + _dot(p.astype(BF16), kv)
    m_ref[...] = m_new


def _attn_prompt_kernel(ql_ref, qr_ref, kv_ref, kr_ref, o_ref, m_ref, l_ref, acc_ref):
    tq = ql_ref.shape[1]
    i = pl.program_id(1)
    ql = ql_ref[...].reshape(MLA_HEADS * tq, KV_LORA)
    qr = qr_ref[...].reshape(MLA_HEADS * tq, ROPE_DIM)
    m_ref[...] = jnp.full(m_ref.shape, -jnp.inf, F32)
    l_ref[...] = jnp.zeros(l_ref.shape, F32)
    acc_ref[...] = jnp.zeros(acc_ref.shape, F32)

    def scores(j):
        sl = pl.ds(pl.multiple_of(j * tq, tq), tq)
        kv = kv_ref[sl, :]
        return _dot_nt(ql, kv) + _dot_nt(qr, kr_ref[sl, :]), kv

    def body(j, carry):
        s, kv = scores(j)
        _online_update(s, kv, m_ref, l_ref, acc_ref)
        return carry

    lax.fori_loop(0, i, body, 0)
    s, kv = scores(i)
    qpos = lax.broadcasted_iota(jnp.int32, (MLA_HEADS * tq, tq), 0) % tq
    kpos = lax.broadcasted_iota(jnp.int32, (MLA_HEADS * tq, tq), 1)
    _online_update(jnp.where(kpos <= qpos, s, -jnp.inf), kv, m_ref, l_ref, acc_ref)
    out = acc_ref[...] / l_ref[...]
    for h in range(MLA_HEADS):
        o_ref[:, h * KV_LORA:(h + 1) * KV_LORA] = out[h * tq:(h + 1) * tq, :].astype(BF16)


def _attn_prompt(qlat, qrope, ckvb, krb, batch, seq):
    tq = ATT_BLOCK
    nq = seq // tq
    rows = MLA_HEADS * tq
    return pl.pallas_call(
        _attn_prompt_kernel,
        grid=(batch, nq),
        in_specs=[pl.BlockSpec((MLA_HEADS, tq, KV_LORA), lambda b, i: (0, b * nq + i, 0)),
                  pl.BlockSpec((MLA_HEADS, tq, ROPE_DIM), lambda b, i: (0, b * nq + i, 0)),
                  pl.BlockSpec((seq, KV_LORA), lambda b, i: (b, 0)),
                  pl.BlockSpec((seq, ROPE_DIM), lambda b, i: (b, 0))],
        out_specs=pl.BlockSpec((tq, MLA_HEADS * KV_LORA), lambda b, i: (b * nq + i, 0)),
        out_shape=jax.ShapeDtypeStruct((batch * seq, MLA_HEADS * KV_LORA), BF16),
        scratch_shapes=[pltpu.VMEM((rows, 1), F32), pltpu.VMEM((rows, 1), F32),
                        pltpu.VMEM((rows, KV_LORA), F32)],
        compiler_params=_cparams(("parallel", "arbitrary")), name="attn_prompt",
    )(qlat, qrope, ckvb, krb)


def _attn_sample_kernel(pt_ref, ql_ref, qr_ref, kvn_ref, krn_ref, *rest, seq):
    del pt_ref
    G = PAGES_PER_STEP
    kv_pages = rest[:G]
    kr_pages = rest[G:2 * G]
    o_ref, m_ref, l_ref, acc_ref, kvb_ref, krb_ref = rest[2 * G:]
    page = kv_pages[0].shape[0]
    g = pl.program_id(1)

    @pl.when(g == 0)
    def _():
        m_ref[...] = jnp.full(m_ref.shape, -jnp.inf, F32)
        l_ref[...] = jnp.zeros(l_ref.shape, F32)
        acc_ref[...] = jnp.zeros(acc_ref.shape, F32)

    for j in range(G):
        kvb_ref[j * page:(j + 1) * page, :] = kv_pages[j][...].astype(BF16)
        krb_ref[j * page:(j + 1) * page, :] = kr_pages[j][...].astype(BF16)
    ql = ql_ref[...]
    qr = qr_ref[...]
    kv = kvb_ref[...]
    s = _dot_nt(ql, kv) + _dot_nt(qr, krb_ref[...])
    _online_update(s, kv, m_ref, l_ref, acc_ref)

    @pl.when(g == pl.num_programs(1) - 1)
    def _():
        kvn = kvn_ref[...]
        sn = _dot_nt(ql, kvn) + _dot_nt(qr, krn_ref[...])
        qpos = lax.broadcasted_iota(jnp.int32, sn.shape, 0) % seq
        kpos = lax.broadcasted_iota(jnp.int32, sn.shape, 1)
        _online_update(jnp.where(kpos <= qpos, sn, -jnp.inf), kvn, m_ref, l_ref, acc_ref)
        o_ref[...] = (acc_ref[...] / l_ref[...]).astype(BF16)


def _attn_sample(page_table, qlat, qrope, kv_new, kr_new, cache_ckv, cache_krope, seq):
    batch, n_pages = page_table.shape
    G = PAGES_PER_STEP
    page = cache_ckv.shape[1]
    rows = qlat.shape[1]
    npad = kv_new.shape[1]

    def page_spec(width, j):
        return pl.BlockSpec((None, page, width), lambda b, g, pt: (pt[b, g * G + j], 0, 0))

    per_b = lambda width, r: pl.BlockSpec((None, r, width), lambda b, g, pt: (b, 0, 0))
    grid_spec = pltpu.PrefetchScalarGridSpec(
        num_scalar_prefetch=1,
        grid=(batch, n_pages // G),
        in_specs=[per_b(KV_LORA, rows), per_b(ROPE_DIM, rows), per_b(KV_LORA, npad), per_b(ROPE_DIM, npad)]
        + [page_spec(KV_LORA, j) for j in range(G)] + [page_spec(ROPE_DIM, j) for j in range(G)],
        out_specs=per_b(KV_LORA, rows),
        scratch_shapes=[pltpu.VMEM((rows, 1), F32), pltpu.VMEM((rows, 1), F32),
                        pltpu.VMEM((rows, KV_LORA), F32),
                        pltpu.VMEM((G * page, KV_LORA), BF16), pltpu.VMEM((G * page, ROPE_DIM), BF16)],
    )
    return pl.pallas_call(
        functools.partial(_attn_sample_kernel, seq=seq),
        grid_spec=grid_spec,
        out_shape=jax.ShapeDtypeStruct((batch, rows, KV_LORA), BF16),
        compiler_params=_cparams(("parallel", "arbitrary")), name="attn_sample",
    )(page_table, qlat, qrope, kv_new, kr_new, *([cache_ckv] * G), *([cache_krope] * G))


def _ple(h, p, png, pgw, ppw):
    gate = _sigmoid(_dot(_rms(h, png).astype(BF16), pgw))
    return h + gate * _dot(p.astype(BF16), ppw)


def _even_post_kernel(x_ref, ohg_ref, ctx_ref, mg_ref, p_ref, wuv_ref, wout_ref, png_ref, pgw_ref, ppw_ref,
                      ng_ref, h_ref, hn_ref):
    o_mla = (_dot(ctx_ref[...], wuv_ref[...]) * mg_ref[...]).astype(BF16)
    n_hg = HG_HEADS * HG_DV
    h = x_ref[...] + _dot(ohg_ref[...], wout_ref[0:n_hg, :]) + _dot(o_mla, wout_ref[n_hg:, :])
    h = _ple(h, p_ref[...], png_ref[...], pgw_ref[...], ppw_ref[...])
    h_ref[...] = h
    hn_ref[...] = _rms(h, ng_ref[...])


def _even_post(x2d, ohg, ctx, mg, p2d, wuv, wout, png, pgw, ppw, ng):
    n, d = x2d.shape
    tm = min(ROW_TILE, n)
    row = lambda a: pl.BlockSpec((tm, a.shape[1]), lambda i: (i, 0))
    return pl.pallas_call(
        _even_post_kernel,
        grid=(n // tm,),
        in_specs=[row(x2d), row(ohg), row(ctx), row(mg), row(p2d), _full(wuv.shape), _full(wout.shape),
                  _full(png.shape), _full(pgw.shape), _full(ppw.shape), _full(ng.shape)],
        out_specs=(row(x2d), row(x2d)),
        out_shape=(jax.ShapeDtypeStruct((n, d), F32), jax.ShapeDtypeStruct((n, d), F32)),
        compiler_params=_cparams(("parallel",)), name="even_post",
    )(x2d, ohg, ctx, mg, p2d, wuv, wout, png, pgw, ppw, ng)


def _rwkv_pre_kernel(hn_ref, pv_ref, mu_ref, wr_ref, wk_ref, wv_ref, wg_ref, w0_ref, w1_ref, w2_ref, a0_ref,
                     a1_ref, a2_ref, kk_ref, ka_ref, bd_ref,
                     r_ref, w_ref, k_ref, v_ref, na_ref, b_ref, g_ref):
    hn = hn_ref[...]
    dlt = pv_ref[...] - hn
    mix = lambda j: (hn + dlt * mu_ref[j:j + 1, :]).astype(BF16)
    r = _dot(mix(0), wr_ref[...])
    k = _dot(mix(1), wk_ref[...])
    v = _dot(mix(2), wv_ref[...])
    g = _dot(mix(3), wg_ref[...])
    wl = w0_ref[...] + _dot(jnp.tanh(_dot(mix(4), w1_ref[...])).astype(BF16), w2_ref[...])
    w_log = -(jnp.maximum(-wl, 0.0) + jnp.log(1.0 + jnp.exp(-jnp.abs(wl)))) - 0.5
    a = _sigmoid(a0_ref[...] + _dot(_dot(mix(5), a1_ref[...]).astype(BF16), a2_ref[...]))
    kk = k * kk_ref[...]
    kk = kk / jnp.maximum(jnp.sqrt(_head_sum(kk * kk, bd_ref[...])), 1e-12)
    r_ref[...] = r
    w_ref[...] = jnp.exp(-jnp.exp(w_log))
    k_ref[...] = k * (1.0 + (a - 1.0) * ka_ref[...])
    v_ref[...] = v
    na_ref[...] = -kk
    b_ref[...] = kk * a
    g_ref[...] = _silu(g)


def _rwkv_pre(hn, prev, mu, wr, wk, wv, wg, w0, w1, w2, a0, a1, a2, kk, ka, bd):
    n, d = hn.shape
    tm = min(ROW_TILE, n)
    row = pl.BlockSpec((tm, d), lambda i: (i, 0))
    ws = [mu, wr, wk, wv, wg, w0, w1, w2, a0, a1, a2, kk, ka, bd]
    return pl.pallas_call(
        _rwkv_pre_kernel,
        grid=(n // tm,),
        in_specs=[row, row] + [_full(w.shape) for w in ws],
        out_specs=(row,) * 7,
        out_shape=(jax.ShapeDtypeStruct((n, d), F32),) * 7,
        compiler_params=_cparams(("parallel",)), name="rwkv_pre",
    )(hn, prev, *ws)


def _wkv_scan_kernel(r_ref, w_ref, k_ref, v_ref, a_ref, b_ref, s0_ref, y_ref, s_ref):
    tt = r_ref.shape[0]
    n = s_ref.shape[0]

    @pl.when(pl.program_id(1) == 0)
    def _():
        s_ref[...] = s0_ref[...]

    def step(t, carry):
        row = lambda ref, k: ref[t, k:k + 1, :]
        sa = s_ref[0] * row(a_ref, 0)
        for k in range(1, n):
            sa = sa + s_ref[k] * row(a_ref, k)
        v = v_ref[t]
        y = jnp.zeros_like(v)
        for k in range(n):
            s = s_ref[k] * row(w_ref, k) + sa * row(b_ref, k) + v * row(k_ref, k)
            s_ref[k] = s
            y = y + s * row(r_ref, k)
        y_ref[t] = y
        return carry

    lax.fori_loop(0, tt, step, 0)


def _wkv_scan(r, w, k, v, a, b, s0):
    t, n, lanes = r.shape
    tt = min(SCAN_TIME_BLOCK, t)
    seq = pl.BlockSpec((tt, n, LANES), lambda g, i: (i, 0, g))
    st = pl.BlockSpec((n, n, LANES), lambda g, i: (0, 0, g))
    return pl.pallas_call(
        _wkv_scan_kernel,
        grid=(lanes // LANES, t // tt),
        in_specs=[seq] * 6 + [st],
        out_specs=(seq, st),
        out_shape=(jax.ShapeDtypeStruct((t, n, lanes), F32), jax.ShapeDtypeStruct((n, n, lanes), F32)),
        compiler_params=_cparams(("parallel", "arbitrary")), name="wkv_scan",
    )(r, w, k, v, a, b, s0)


def _rwkv_post_kernel(y_ref, r_ref, k_ref, v_ref, g_ref, h_ref, p_ref, lnw_ref, lnb_ref, rk_ref, bd_ref,
                      wo_ref, png_ref, pgw_ref, ppw_ref, fg_ref, o_ref):
    bd = bd_ref[...]
    y = y_ref[...]
    inv_n = 1.0 / RW_N
    yc = y - _head_sum(y, bd) * inv_n
    var = _head_sum(yc * yc, bd) * inv_n
    yn = yc * lax.rsqrt(var + RW_EPS) * lnw_ref[...] + lnb_ref[...]
    bonus = _head_sum(r_ref[...] * k_ref[...] * rk_ref[...], bd) * v_ref[...]
    mix = ((yn + bonus) * g_ref[...]).astype(BF16)
    h = h_ref[...] + _dot(mix, wo_ref[...])
    h = _ple(h, p_ref[...], png_ref[...], pgw_ref[...], ppw_ref[...])
    o_ref[...] = _rms(h, fg_ref[...])


def _rwkv_post(y, r, k, v, g, h, p2d, lnw, lnb, rk, bd, wo, png, pgw, ppw, fg):
    n, d = h.shape
    tm = min(ROW_TILE, n)
    row = lambda a: pl.BlockSpec((tm, a.shape[1]), lambda i: (i, 0))
    ws = [lnw, lnb, rk, bd, wo, png, pgw, ppw, fg]
    return pl.pallas_call(
        _rwkv_post_kernel,
        grid=(n // tm,),
        in_specs=[row(y), row(r), row(k), row(v), row(g), row(h), row(p2d)] + [_full(w.shape) for w in ws],
        out_specs=row(h),
        out_shape=jax.ShapeDtypeStruct((n, d), F32),
        compiler_params=_cparams(("parallel",)), name="rwkv_post",
    )(y, r, k, v, g, h, p2d, *ws)


def _rope_tables(pos, rows):
    half = ROPE_DIM // 2
    inv = ROPE_THETA ** (-jnp.arange(half, dtype=F32) / half)
    ang = pos.astype(F32)[:, None] * inv[None, :]
    cos = jnp.cos(ang)
    sin = jnp.sin(ang)
    cos = jnp.tile(jnp.concatenate([cos, cos], axis=-1), (rows // pos.shape[0], MLA_HEADS))
    sin = jnp.tile(jnp.concatenate([-sin, sin], axis=-1), (rows // pos.shape[0], MLA_HEADS))
    return cos, sin


def _swap_halves(w):
    half = w.shape[-1] // 2
    return jnp.concatenate([w[..., half:], w[..., :half]], axis=-1)


def _to_lanes(x, batch, seq):
    heads = x.shape[1] // RW_N
    return x.reshape(batch, seq, heads, RW_N).transpose(1, 3, 0, 2).reshape(seq, RW_N, batch * heads)


def _from_lanes(y, batch, seq):
    heads = y.shape[2] // batch
    return y.reshape(seq, RW_N, batch, heads).transpose(2, 0, 3, 1).reshape(batch * seq, heads * RW_N)


def _trunk(x, p, pos, W, hg_s0, wkv_s0, shift_s0, paged):
    batch, seq, d = x.shape
    n = batch * seq
    x2d = x.reshape(n, d)
    row = lambda v: v.reshape(1, -1)

    cos_tab, sin_tab = _rope_tables(pos, max(seq, min(ROW_TILE, n)))
    zhg, qlat, qrope, ckv, kr, ckvb, krb, mg = _even_pre(
        x2d, cos_tab, sin_tab, row(W['mix_norm'][0]), W['hg_lb_logits'], W['w_in_hg'], W['w_in_mla'],
        row(W['mla_q_norm'][0]), row(W['mla_kv_norm'][0]), W['w_uq'], W['w_uk'], lb_rows=1)
    g_hg = row(W['hg_norm'][0])
    if paged is None:
        ohg, s_hg = _hgrn_prompt(zhg, g_hg, batch, seq)
        ctx = _attn_prompt(qlat, qrope, ckvb, krb, batch, seq)
    else:
        ohg, s_hg = _hgrn_sample(zhg, hg_s0, g_hg, seq)
        cache_ckv, cache_krope, page_table = paged
        stack = lambda q: q.reshape(MLA_HEADS, batch, seq, q.shape[-1]).transpose(1, 0, 2, 3).reshape(
            batch, MLA_HEADS * seq, q.shape[-1])
        pad = lambda a: jnp.pad(a.reshape(batch, seq, a.shape[-1]), ((0, 0), (0, 2 * SUBLANES - seq), (0, 0)))
        ctx = _attn_sample(page_table, stack(qlat), stack(qrope), pad(ckvb), pad(krb), cache_ckv, cache_krope, seq)
        ctx = ctx.reshape(batch, MLA_HEADS, seq, KV_LORA).transpose(0, 2, 1, 3).reshape(n, MLA_HEADS * KV_LORA)
    h1, hn1 = _even_post(
        x2d, ohg, ctx, mg, p[0].reshape(n, -1), W['w_uv_bd'], W['w_out'], row(W['ple_norm'][0]),
        W['ple_gate_b'][0], W['ple_proj_b'][0], row(W['mix_norm'][1]))

    hn3 = hn1.reshape(batch, seq, d)
    prev = jnp.concatenate([shift_s0[:, None, :], hn3[:, :-1]], axis=1).reshape(n, d)
    r, w, k, v, na, b, g = _rwkv_pre(
        hn1, prev, W['rw_mu'][0], W['w_r'], W['w_k'], W['w_v'], W['w_g'], row(W['rw_w0'][0]), W['w_w1'],
        W['w_w2'], row(W['rw_a0'][0]), W['w_a1'], W['w_a2'], row(W['rw_k_k'][0]), row(W['rw_k_a'][0]),
        W['head_bd'])
    lanes = lambda a: _to_lanes(a, batch, seq)
    s0 = wkv_s0.transpose(3, 2, 0, 1).reshape(RW_N, RW_N, -1)
    y, s_wkv = _wkv_scan(lanes(r), lanes(w), lanes(k), lanes(v), lanes(na), lanes(b), s0)
    heads = d // RW_N
    s_wkv = s_wkv.reshape(RW_N, RW_N, batch, heads).transpose(2, 3, 1, 0)
    out = _rwkv_post(
        _from_lanes(y, batch, seq), r, k, v, g, h1, p[1].reshape(n, -1), row(W['rw_ln_w'][0]),
        row(W['rw_ln_b'][0]), row(W['rw_r_k'][0]), W['head_bd'], W['w_o'], row(W['ple_norm'][1]),
        W['ple_gate_b'][1], W['ple_proj_b'][1], row(W['final_norm']))
    return (out.reshape(batch, seq, d), ckv.reshape(1, batch, seq, KV_LORA), kr.reshape(1, batch, seq, ROPE_DIM),
            s_hg[None], s_wkv[None], hn3[:, -1][None])


def kernel(x_prompt, x_sample, cache_ckv, cache_krope, state_hgrn, state_wkv, state_shift, page_table, p_prompt, p_sample, mix_norm, ev_w_in, hg_lb_logits, hg_norm, mla_q_norm, mla_w_uq, mla_kv_norm, mla_w_uk, mla_w_uv, ev_w_out, rw_mu, rw_w_rkvg, rw_w0, rw_w1, rw_w2, rw_a0, rw_a1, rw_a2, rw_k_k, rw_k_a, rw_r_k, rw_ln_w, rw_ln_b, rw_w_o, ple_norm, ple_gate, ple_proj, final_norm):
    bf = lambda a: a.astype(BF16)
    d = x_prompt.shape[-1]
    w_in = ev_w_in[0]
    o = np.cumsum([0, HG_WIDTH, HG_WIDTH, HG_HEADS * HG_DV, HG_HEADS * HG_DV, Q_LORA, KV_LORA, ROPE_DIM, MLA_WIDTH])
    cq, ckv_w, kr_w, mg_w = (w_in[:, o[4]:o[5]], w_in[:, o[5]:o[6]], w_in[:, o[6]:o[7]], w_in[:, o[7]:o[8]])
    uq = mla_w_uq[0].reshape(Q_LORA, MLA_HEADS, NOPE_DIM + ROPE_DIM)
    uq_rope = uq[:, :, NOPE_DIM:]
    uv_bd = jnp.zeros((MLA_HEADS, KV_LORA, MLA_HEADS, V_DIM), F32)
    uv_bd = uv_bd.at[jnp.arange(MLA_HEADS), :, jnp.arange(MLA_HEADS), :].set(mla_w_uv[0].transpose(0, 2, 1))
    hid = np.arange(MXU_DIM) // RW_N
    W = dict(
        mix_norm=mix_norm, hg_lb_logits=hg_lb_logits, hg_norm=hg_norm, mla_q_norm=mla_q_norm,
        mla_kv_norm=mla_kv_norm, ple_norm=ple_norm, final_norm=final_norm, rw_mu=rw_mu, rw_w0=rw_w0, rw_a0=rw_a0,
        rw_k_k=rw_k_k, rw_k_a=rw_k_a, rw_ln_w=rw_ln_w, rw_ln_b=rw_ln_b, rw_r_k=rw_r_k.reshape(rw_r_k.shape[0], -1),
        w_in_hg=bf(w_in[:, :o[4]]),
        w_in_mla=bf(jnp.concatenate([cq, ckv_w, mg_w, kr_w, _swap_halves(kr_w)], axis=-1)),
        w_uq=bf(jnp.concatenate([uq[:, :, :NOPE_DIM].reshape(Q_LORA, -1), uq_rope.reshape(Q_LORA, -1),
                                 _swap_halves(uq_rope).reshape(Q_LORA, -1)], axis=-1)),
        w_uk=bf(mla_w_uk[0]),
        w_uv_bd=bf(uv_bd.reshape(MLA_HEADS * KV_LORA, MLA_WIDTH)),
        w_out=bf(ev_w_out[0]),
        ple_gate_b=bf(ple_gate), ple_proj_b=bf(ple_proj),
        w_r=bf(rw_w_rkvg[0, 0]), w_k=bf(rw_w_rkvg[0, 1]), w_v=bf(rw_w_rkvg[0, 2]), w_g=bf(rw_w_rkvg[0, 3]),
        w_w1=bf(rw_w1[0]), w_w2=bf(rw_w2[0]), w_a1=bf(rw_a1[0]), w_a2=bf(rw_a2[0]), w_o=bf(rw_w_o[0]),
        head_bd=jnp.asarray(hid[:, None] == hid[None, :], BF16),
    )
    bp, tp, _ = x_prompt.shape
    bs, ts, _ = x_sample.shape
    past_len = page_table.shape[1] * cache_ckv.shape[2]
    heads = d // RW_N
    yp, ckv_p, kr_p, hg_p, wkv_p, sh_p = _trunk(
        x_prompt, p_prompt, jnp.arange(tp), W, None, jnp.zeros((bp, heads, RW_N, RW_N), F32),
        jnp.zeros((bp, d), F32), None)
    ys, ckv_s, kr_s, hg_s, wkv_s, sh_s = _trunk(
        x_sample, p_sample, past_len + jnp.arange(ts), W, state_hgrn[0], state_wkv[0], state_shift[0],
        (cache_ckv.reshape(cache_ckv.shape[1:]), cache_krope.reshape(cache_krope.shape[1:]), page_table))
    return (yp, ys, ckv_p, kr_p, ckv_s, kr_s, hg_p, hg_s, wkv_p, wkv_s, sh_p, sh_s)
```

```python
import functools

import jax
import jax.numpy as jnp
import numpy as np
from jax import lax
from jax.experimental import pallas as pl
from jax.experimental.pallas import tpu as pltpu

F32 = jnp.float32
BF16 = jnp.bfloat16

NORM_EPS = 1e-6
HG_HEADS = 4
HG_DK = 128
HG_DV = 128
HG_WIDTH = HG_HEADS * HG_DK
MLA_HEADS = 8
Q_LORA = 384
KV_LORA = 256
NOPE_DIM = 64
ROPE_DIM = 32
V_DIM = 64
MLA_WIDTH = MLA_HEADS * V_DIM
MLA_SCALE = (NOPE_DIM + ROPE_DIM) ** -0.5
ROPE_THETA = 10000.0
RW_N = 64
RW_EPS = 64e-5

LANES = 128
SUBLANES = 8
MXU_DIM = 256
VMEM_LIMIT_BYTES = 56 * 1024 * 1024

ROW_TILE = 256
HG_CHUNK = 64
HG_BLOCK = SUBLANES
HG_TIME_BLOCK = 1024
ATT_BLOCK = 256
PAGES_PER_STEP = 32
SCAN_TIME_BLOCK = 32


def _cparams(sem):
    return pltpu.CompilerParams(dimension_semantics=sem, vmem_limit_bytes=VMEM_LIMIT_BYTES)


def _rms(x, g):
    return x * lax.rsqrt(jnp.mean(x * x, axis=-1, keepdims=True) + NORM_EPS) * g


def _sigmoid(x):
    return 1.0 / (1.0 + jnp.exp(-x))


def _silu(x):
    return x * _sigmoid(x)


def _dot(a, b):
    return jnp.dot(a, b, preferred_element_type=F32)


def _dot_nt(a, b):
    return lax.dot_general(a, b, (((1,), (1,)), ((), ())), preferred_element_type=F32)


def _head_sum(x, bd):
    hi = x.astype(BF16)
    lo = (x - hi.astype(F32)).astype(BF16)
    outs = []
    for c in range(x.shape[-1] // MXU_DIM):
        sl = slice(c * MXU_DIM, (c + 1) * MXU_DIM)
        outs.append(_dot(hi[:, sl], bd) + _dot(lo[:, sl], bd))
    return jnp.concatenate(outs, axis=-1)


def _full(shape):
    nd = len(shape)
    return pl.BlockSpec(shape, lambda *_: (0,) * nd)


def _even_pre_kernel(x_ref, cos_ref, sin_ref, g_ref, lbl_ref, w1_ref, w2_ref, qg_ref, kvg_ref, wuq_ref,
                     wuk_ref, zhg_ref, qlat_ref, qrope_ref, ckv_ref, kr_ref, ckvb_ref, krb_ref, mg_ref,
                     *, lb_rows):
    hn = _rms(x_ref[...], g_ref[...]).astype(BF16)
    z1 = _dot(hn, w1_ref[...])
    z2 = _dot(hn, w2_ref[...])
    lg = lbl_ref[...]
    e = jnp.exp(lg - jnp.max(lg, axis=0, keepdims=True))
    p = e / jnp.sum(e, axis=0, keepdims=True)
    lb = jnp.sum(p[:lb_rows], axis=0, keepdims=True)
    W = HG_WIDTH
    f = lb + (1.0 - lb) * _sigmoid(z1[:, W:2 * W])
    zhg_ref[:, 0:W] = _silu(z1[:, 0:W])
    zhg_ref[:, W:2 * W] = jnp.log(f)
    zhg_ref[:, 2 * W:3 * W] = z1[:, 2 * W:3 * W]
    zhg_ref[:, 3 * W:4 * W] = _silu(z1[:, 3 * W:4 * W])
    o_kv = Q_LORA
    o_mg = o_kv + KV_LORA
    o_kr = o_mg + MLA_WIDTH
    o_krs = o_kr + ROPE_DIM
    cqn = _rms(z2[:, 0:o_kv], qg_ref[...]).astype(BF16)
    qf = _dot(cqn, wuq_ref[...])
    cos = cos_ref[...]
    sin = sin_ref[...]
    n_nope = MLA_HEADS * NOPE_DIM
    n_rope = MLA_HEADS * ROPE_DIM
    qr = (qf[:, n_nope:n_nope + n_rope] * cos + qf[:, n_nope + n_rope:n_nope + 2 * n_rope] * sin) * MLA_SCALE
    for h in range(MLA_HEADS):
        qrope_ref[h] = qr[:, h * ROPE_DIM:(h + 1) * ROPE_DIM].astype(BF16)
        qn = qf[:, h * NOPE_DIM:(h + 1) * NOPE_DIM].astype(BF16)
        qlat_ref[h] = (_dot(qn, wuk_ref[h]) * MLA_SCALE).astype(BF16)
    ckv = _rms(z2[:, o_kv:o_mg], kvg_ref[...])
    ckv_ref[...] = ckv
    ckvb_ref[...] = ckv.astype(BF16)
    kr = z2[:, o_kr:o_krs] * cos[:, :ROPE_DIM] + z2[:, o_krs:o_krs + ROPE_DIM] * sin[:, :ROPE_DIM]
    kr_ref[...] = kr
    krb_ref[...] = kr.astype(BF16)
    mg_ref[...] = _silu(z2[:, o_mg:o_kr])


def _even_pre(x2d, cos_tab, sin_tab, g, lb_logits, w1, w2, qg, kvg, wuq, wuk, *, lb_rows):
    n, d = x2d.shape
    tm = min(ROW_TILE, n)
    n_tab = cos_tab.shape[0] // tm
    row = lambda i: (i, 0)
    tab = lambda i: (i % n_tab, 0)
    out_shape = (
        jax.ShapeDtypeStruct((n, 4 * HG_WIDTH), F32),
        jax.ShapeDtypeStruct((MLA_HEADS, n, KV_LORA), BF16),
        jax.ShapeDtypeStruct((MLA_HEADS, n, ROPE_DIM), BF16),
        jax.ShapeDtypeStruct((n, KV_LORA), F32),
        jax.ShapeDtypeStruct((n, ROPE_DIM), F32),
        jax.ShapeDtypeStruct((n, KV_LORA), BF16),
        jax.ShapeDtypeStruct((n, ROPE_DIM), BF16),
        jax.ShapeDtypeStruct((n, MLA_WIDTH), F32),
    )
    out_specs = (
        pl.BlockSpec((tm, 4 * HG_WIDTH), row),
        pl.BlockSpec((MLA_HEADS, tm, KV_LORA), lambda i: (0, i, 0)),
        pl.BlockSpec((MLA_HEADS, tm, ROPE_DIM), lambda i: (0, i, 0)),
        pl.BlockSpec((tm, KV_LORA), row),
        pl.BlockSpec((tm, ROPE_DIM), row),
        pl.BlockSpec((tm, KV_LORA), row),
        pl.BlockSpec((tm, ROPE_DIM), row),
        pl.BlockSpec((tm, MLA_WIDTH), row),
    )
    in_specs = [
        pl.BlockSpec((tm, d), row),
        pl.BlockSpec((tm, cos_tab.shape[1]), tab),
        pl.BlockSpec((tm, sin_tab.shape[1]), tab),
        _full(g.shape), _full(lb_logits.shape), _full(w1.shape), _full(w2.shape), _full(qg.shape),
        _full(kvg.shape), _full(wuq.shape), _full(wuk.shape),
    ]
    return pl.pallas_call(
        functools.partial(_even_pre_kernel, lb_rows=lb_rows),
        grid=(n // tm,), in_specs=in_specs, out_specs=out_specs, out_shape=out_shape,
        compiler_params=_cparams(("parallel",)), name="even_pre",
    )(x2d, cos_tab, sin_tab, g, lb_logits, w1, w2, qg, kvg, wuq, wuk)


def _tril_ones(n, block):
    r = lax.broadcasted_iota(jnp.int32, (n, n), 0)
    c = lax.broadcasted_iota(jnp.int32, (n, n), 1)
    return ((r >= c) & ((r // block) == (c // block))).astype(F32)


def _hgrn_exact_blocks(q, k, v, b, block):
    rows = q.shape[0]
    rid = lax.broadcasted_iota(jnp.int32, (rows, 1), 0) % block
    o = jnp.sum(q * k, axis=-1, keepdims=True) * v
    for d in range(1, block):
        kd = pltpu.roll(k, d, 0)
        bd = pltpu.roll(b, d, 0)
        vd = pltpu.roll(v, d, 0)
        w = jnp.sum(q * kd * jnp.exp(jnp.minimum(b - bd, 0.0)), axis=-1, keepdims=True)
        o = o + jnp.where(rid >= d, w, 0.0) * vd
    return o


def _hgrn_finish(o, g, gate):
    return (o * lax.rsqrt(jnp.mean(o * o, axis=-1, keepdims=True) + NORM_EPS) * g * gate).astype(BF16)


def _hgrn_prompt_kernel(q_ref, lf_ref, v_ref, gt_ref, g_ref, o_ref, s_ref, st_ref):
    C = HG_CHUNK
    tb = q_ref.shape[0]
    tril = _tril_ones(C, C)
    rid = lax.broadcasted_iota(jnp.int32, (C, 1), 0)
    rr = lax.broadcasted_iota(jnp.int32, (C, C), 0)
    cc = lax.broadcasted_iota(jnp.int32, (C, C), 1)
    g = g_ref[...]

    @pl.when(pl.program_id(1) == 0)
    def _():
        st_ref[...] = jnp.zeros(st_ref.shape, F32)

    def head_chunk(sl, h):
        hs = slice(h * HG_DK, (h + 1) * HG_DK)
        st = st_ref[h]
        q = q_ref[sl, hs]
        lf = lf_ref[sl, hs]
        v = v_ref[sl, hs]
        k = 1.0 - jnp.exp(lf)
        b = jnp.dot(tril, lf, precision=lax.Precision.HIGHEST, preferred_element_type=F32)
        o = _dot_nt((q * jnp.exp(b)).astype(BF16), st.astype(BF16))
        o = o + _hgrn_exact_blocks(q, k, v, b, HG_BLOCK)
        att = jnp.zeros((C, C), F32)
        m = HG_BLOCK
        while 2 * m <= C:
            nb = C // (2 * m)
            b3 = b.reshape(nb, 2 * m, HG_DK)
            ref = jnp.broadcast_to(b3[:, m - 1:m, :], (nb, 2 * m, HG_DK)).reshape(C, HG_DK)
            upper = (rid % (2 * m)) >= m
            qt = q * jnp.where(upper, jnp.exp(jnp.minimum(b - ref, 0.0)), 0.0)
            kt = k * jnp.where(upper, 0.0, jnp.exp(jnp.minimum(ref - b, 0.0)))
            a = _dot_nt(qt.astype(BF16), kt.astype(BF16))
            if 2 * m < C:
                a = jnp.where((rr // (2 * m)) == (cc // (2 * m)), a, 0.0)
            att = att + a
            m *= 2
        o = o + _dot(att.astype(BF16), v.astype(BF16))
        o_ref[sl, hs] = _hgrn_finish(o, g, gt_ref[sl, hs])
        bend = b[C - 1:C, :]
        kt = k * jnp.exp(bend - b)
        st_ref[h] = st * jnp.exp(bend) + _dot(v.T.astype(BF16), kt.astype(BF16))

    def chunk(c, carry):
        sl = pl.ds(pl.multiple_of(c * C, C), C)
        for h in range(HG_HEADS):
            head_chunk(sl, h)
        return carry

    lax.fori_loop(0, tb // C, chunk, 0)

    @pl.when(pl.program_id(1) == pl.num_programs(1) - 1)
    def _():
        for h in range(HG_HEADS):
            s_ref[h] = st_ref[h].T


def _hgrn_prompt(zhg, g, batch, seq):
    tb = min(HG_TIME_BLOCK, seq)
    nt = seq // tb
    blk = lambda off: pl.BlockSpec((tb, HG_WIDTH), lambda b, i: (b * nt + i, off))
    return pl.pallas_call(
        _hgrn_prompt_kernel,
        grid=(batch, nt),
        in_specs=[blk(0), blk(1), blk(2), blk(3), _full(g.shape)],
        out_specs=(pl.BlockSpec((tb, HG_HEADS * HG_DV), lambda b, i: (b * nt + i, 0)),
                   pl.BlockSpec((None, HG_HEADS, HG_DK, HG_DV), lambda b, i: (b, 0, 0, 0))),
        out_shape=(jax.ShapeDtypeStruct((batch * seq, HG_HEADS * HG_DV), BF16),
                   jax.ShapeDtypeStruct((batch, HG_HEADS, HG_DK, HG_DV), F32)),
        scratch_shapes=[pltpu.VMEM((HG_HEADS, HG_DV, HG_DK), F32)],
        compiler_params=_cparams(("parallel", "arbitrary")), name="hgrn_prompt",
    )(zhg, zhg, zhg, zhg, g)


def _hgrn_sample_kernel(z_ref, s0_ref, g_ref, o_ref, s_ref, *, seq):
    rows = z_ref.shape[0]
    nb = rows // seq
    tril = _tril_ones(rows, seq)
    rb = lax.broadcasted_iota(jnp.int32, (rows, 1), 0) // seq
    cb = lax.broadcasted_iota(jnp.int32, (1, rows), 1) // seq
    g = g_ref[...]
    W = HG_WIDTH
    for h in range(HG_HEADS):
        hs = slice(h * HG_DK, (h + 1) * HG_DK)
        q = z_ref[:, hs]
        lf = z_ref[:, W + h * HG_DK:W + (h + 1) * HG_DK]
        v = z_ref[:, 2 * W + h * HG_DV:2 * W + (h + 1) * HG_DV]
        gate = z_ref[:, 3 * W + h * HG_DV:3 * W + (h + 1) * HG_DV]
        k = 1.0 - jnp.exp(lf)
        b = jnp.dot(tril, lf, precision=lax.Precision.HIGHEST, preferred_element_type=F32)
        o = _hgrn_exact_blocks(q, k, v, b, seq)
        qe = (q * jnp.exp(b)).astype(BF16)
        bt = b.T
        vb = v.astype(BF16)
        for i in range(nb):
            s0 = s0_ref[i, h]
            o = o + jnp.where(rb == i, _dot(qe, s0.astype(BF16)), 0.0)
            last = i * seq + seq - 1
            bend_row = b[last:last + 1, :]
            kt = jnp.where(rb == i, k * jnp.exp(bend_row - b), 0.0)
            ktt = kt.T.astype(BF16)
            s_ref[i, h] = s0 * jnp.exp(bt[:, last:last + 1]) + _dot(ktt, vb)
        o_ref[:, hs] = _hgrn_finish(o, g, gate)


def _hgrn_sample(zhg, s0, g, seq):
    n = zhg.shape[0]
    nb = 8
    rows = nb * seq
    return pl.pallas_call(
        functools.partial(_hgrn_sample_kernel, seq=seq),
        grid=(n // rows,),
        in_specs=[pl.BlockSpec((rows, 4 * HG_WIDTH), lambda i: (i, 0)),
                  pl.BlockSpec((nb, HG_HEADS, HG_DK, HG_DV), lambda i: (i, 0, 0, 0)),
                  _full(g.shape)],
        out_specs=(pl.BlockSpec((rows, HG_HEADS * HG_DV), lambda i: (i, 0)),
                   pl.BlockSpec((nb, HG_HEADS, HG_DK, HG_DV), lambda i: (i, 0, 0, 0))),
        out_shape=(jax.ShapeDtypeStruct((n, HG_HEADS * HG_DV), BF16),
                   jax.ShapeDtypeStruct(s0.shape, F32)),
        compiler_params=_cparams(("parallel",)), name="hgrn_sample",
    )(zhg, s0, g)


def _lane_tile(x, width):
    if width <= LANES:
        return x[:, :width]
    return jnp.concatenate([x] * (width // LANES), axis=1)


def _softmax_init(m_ref, l_ref, acc_ref):
    m_ref[...] = jnp.full(m_ref.shape, -jnp.inf, F32)
    l_ref[...] = jnp.zeros(l_ref.shape, F32)
    acc_ref[...] = jnp.zeros(acc_ref.shape, F32)


def _online_update(s, kv, rows, m_ref, l_ref, acc_ref):
    m_old = m_ref[rows, :]
    m_new = jnp.maximum(m_old, jnp.max(s, axis=-1, keepdims=True))
    alpha = jnp.exp(m_old - m_new)
    p = jnp.exp(s - _lane_tile(m_new, s.shape[1]))
    l_ref[rows, :] = alpha * l_ref[rows, :] + jnp.sum(p, axis=-1, keepdims=True)
    acc_ref[rows, :] = _lane_tile(alpha, acc_ref.shape[1]) * acc_ref[rows, :] + _dot(p.astype(BF16), kv)
    m_ref[rows, :] = m_new


def _attn_prompt_kernel(ql_ref, qr_ref, kv_ref, kr_ref, o_ref, m_ref, l_ref, acc_ref):
    tq = ql_ref.shape[1]
    i = pl.program_id(1)
    _softmax_init(m_ref, l_ref, acc_ref)
    n_split = 2
    hs = MLA_HEADS // n_split
    mh = hs * tq
    qpos = lax.broadcasted_iota(jnp.int32, (mh, tq), 0) % tq
    kpos = lax.broadcasted_iota(jnp.int32, (mh, tq), 1)

    def block(j, masked):
        sl = pl.ds(pl.multiple_of(j * tq, tq), tq)
        kv = kv_ref[sl, :]
        kr = kr_ref[sl, :]
        for r in range(n_split):
            ql = ql_ref[r * hs:(r + 1) * hs].reshape(mh, KV_LORA)
            qr = qr_ref[r * hs:(r + 1) * hs].reshape(mh, ROPE_DIM)
            s = _dot_nt(ql, kv) + _dot_nt(qr, kr)
            if masked:
                s = jnp.where(kpos <= qpos, s, -jnp.inf)
            _online_update(s, kv, slice(r * mh, (r + 1) * mh), m_ref, l_ref, acc_ref)

    def body(j, carry):
        block(j, False)
        return carry

    lax.fori_loop(0, i, body, 0)
    block(i, True)
    for h in range(MLA_HEADS):
        rows = slice(h * tq, (h + 1) * tq)
        out = acc_ref[rows, :] / _lane_tile(l_ref[rows, :], KV_LORA)
        o_ref[:, h * KV_LORA:(h + 1) * KV_LORA] = out.astype(BF16)


def _attn_prompt(qlat, qrope, ckvb, krb, batch, seq):
    tq = ATT_BLOCK
    nq = seq // tq
    rows = MLA_HEADS * tq
    return pl.pallas_call(
        _attn_prompt_kernel,
        grid=(batch, nq),
        in_specs=[pl.BlockSpec((MLA_HEADS, tq, KV_LORA), lambda b, i: (0, b * nq + i, 0)),
                  pl.BlockSpec((MLA_HEADS, tq, ROPE_DIM), lambda b, i: (0, b * nq + i, 0)),
                  pl.BlockSpec((seq, KV_LORA), lambda b, i: (b, 0)),
                  pl.BlockSpec((seq, ROPE_DIM), lambda b, i: (b, 0))],
        out_specs=pl.BlockSpec((tq, MLA_HEADS * KV_LORA), lambda b, i: (b * nq + i, 0)),
        out_shape=jax.ShapeDtypeStruct((batch * seq, MLA_HEADS * KV_LORA), BF16),
        scratch_shapes=[pltpu.VMEM((rows, LANES), F32), pltpu.VMEM((rows, LANES), F32),
                        pltpu.VMEM((rows, KV_LORA), F32)],
        compiler_params=_cparams(("parallel", "arbitrary")), name="attn_prompt",
    )(qlat, qrope, ckvb, krb)


def _attn_sample_kernel(pt_ref, ql_ref, qr_ref, kvn_ref, krn_ref, *rest, seq, G):
    del pt_ref
    kv_pages = rest[:G]
    kr_pages = rest[G:2 * G]
    o_ref, m_ref, l_ref, acc_ref, kvb_ref, krb_ref = rest[2 * G:]
    page = kv_pages[0].shape[0]
    g = pl.program_id(1)
    every = slice(None)

    @pl.when(g == 0)
    def _():
        _softmax_init(m_ref, l_ref, acc_ref)

    for j in range(G):
        kvb_ref[j * page:(j + 1) * page, :] = kv_pages[j][...].astype(BF16)
        krb_ref[:, j * page:(j + 1) * page] = kr_pages[j][...].astype(BF16)
    ql = ql_ref[...]
    qr = qr_ref[...]
    kv = kvb_ref[...]
    s = _dot_nt(ql, kv) + _dot(qr, krb_ref[...])
    _online_update(s, kv, every, m_ref, l_ref, acc_ref)

    @pl.when(g == pl.num_programs(1) - 1)
    def _():
        kvn = kvn_ref[...]
        sn = _dot_nt(ql, kvn) + _dot(qr, krn_ref[...])
        qpos = lax.broadcasted_iota(jnp.int32, sn.shape, 0) % seq
        kpos = lax.broadcasted_iota(jnp.int32, sn.shape, 1)
        _online_update(jnp.where(kpos <= qpos, sn, -jnp.inf), kvn, every, m_ref, l_ref, acc_ref)
        o_ref[...] = (acc_ref[...] / _lane_tile(l_ref[...], KV_LORA)).astype(BF16)


def _attn_sample(page_table, qlat, qrope, kv_new, kr_new_t, cache_ckv, cache_krope_t, seq):
    batch, n_pages = page_table.shape
    G = min(PAGES_PER_STEP, n_pages)
    page = cache_ckv.shape[1]
    rows = qlat.shape[1]
    npad = kv_new.shape[1]

    def page_spec(shape, j):
        return pl.BlockSpec((None,) + shape, lambda b, g, pt: (pt[b, g * G + j], 0, 0))

    per_b = lambda r, width: pl.BlockSpec((None, r, width), lambda b, g, pt: (b, 0, 0))
    grid_spec = pltpu.PrefetchScalarGridSpec(
        num_scalar_prefetch=1,
        grid=(batch, n_pages // G),
        in_specs=[per_b(rows, KV_LORA), per_b(rows, ROPE_DIM), per_b(npad, KV_LORA), per_b(ROPE_DIM, npad)]
        + [page_spec((page, KV_LORA), j) for j in range(G)]
        + [page_spec((ROPE_DIM, page), j) for j in range(G)],
        out_specs=per_b(rows, KV_LORA),
        scratch_shapes=[pltpu.VMEM((rows, LANES), F32), pltpu.VMEM((rows, LANES), F32),
                        pltpu.VMEM((rows, KV_LORA), F32),
                        pltpu.VMEM((G * page, KV_LORA), BF16), pltpu.VMEM((ROPE_DIM, G * page), BF16)],
    )
    return pl.pallas_call(
        functools.partial(_attn_sample_kernel, seq=seq, G=G),
        grid_spec=grid_spec,
        out_shape=jax.ShapeDtypeStruct((batch, rows, KV_LORA), BF16),
        compiler_params=_cparams(("parallel", "arbitrary")), name="attn_sample",
    )(page_table, qlat, qrope, kv_new, kr_new_t, *([cache_ckv] * G), *([cache_krope_t] * G))


def _ple(h, p, png, pgw, ppw):
    gate = _sigmoid(_dot(_rms(h, png).astype(BF16), pgw))
    return h + gate * _dot(p.astype(BF16), ppw)


def _even_post_kernel(x_ref, ohg_ref, ctx_ref, mg_ref, p_ref, wuv_ref, wout_ref, png_ref, pgw_ref, ppw_ref,
                      ng_ref, h_ref, hn_ref):
    o_mla = (_dot(ctx_ref[...], wuv_ref[...]) * mg_ref[...]).astype(BF16)
    n_hg = HG_HEADS * HG_DV
    h = x_ref[...] + _dot(ohg_ref[...], wout_ref[0:n_hg, :]) + _dot(o_mla, wout_ref[n_hg:, :])
    h = _ple(h, p_ref[...], png_ref[...], pgw_ref[...], ppw_ref[...])
    h_ref[...] = h
    hn_ref[...] = _rms(h, ng_ref[...])


def _even_post(x2d, ohg, ctx, mg, p2d, wuv, wout, png, pgw, ppw, ng):
    n, d = x2d.shape
    tm = min(ROW_TILE, n)
    row = lambda a: pl.BlockSpec((tm, a.shape[1]), lambda i: (i, 0))
    return pl.pallas_call(
        _even_post_kernel,
        grid=(n // tm,),
        in_specs=[row(x2d), row(ohg), row(ctx), row(mg), row(p2d), _full(wuv.shape), _full(wout.shape),
                  _full(png.shape), _full(pgw.shape), _full(ppw.shape), _full(ng.shape)],
        out_specs=(row(x2d), row(x2d)),
        out_shape=(jax.ShapeDtypeStruct((n, d), F32), jax.ShapeDtypeStruct((n, d), F32)),
        compiler_params=_cparams(("parallel",)), name="even_post",
    )(x2d, ohg, ctx, mg, p2d, wuv, wout, png, pgw, ppw, ng)


def _rwkv_pre_kernel(hn_ref, pv_ref, mu_ref, wr_ref, wk_ref, wv_ref, wg_ref, w0_ref, w1_ref, w2_ref, a0_ref,
                     a1_ref, a2_ref, kk_ref, ka_ref, rk_ref, bd_ref,
                     r_ref, w_ref, k_ref, v_ref, na_ref, b_ref, bonus_ref, g_ref, *, time_minor):
    def put(ref, x):
        ref[...] = x.T if time_minor else x

    hn = hn_ref[...]
    dlt = pv_ref[...] - hn
    mix = lambda j: (hn + dlt * mu_ref[j:j + 1, :]).astype(BF16)
    r = _dot(mix(0), wr_ref[...])
    k = _dot(mix(1), wk_ref[...])
    v = _dot(mix(2), wv_ref[...])
    g = _dot(mix(3), wg_ref[...])
    wl = w0_ref[...] + _dot(jnp.tanh(_dot(mix(4), w1_ref[...])).astype(BF16), w2_ref[...])
    w_log = -(jnp.maximum(-wl, 0.0) + jnp.log(1.0 + jnp.exp(-jnp.abs(wl)))) - 0.5
    a = _sigmoid(a0_ref[...] + _dot(_dot(mix(5), a1_ref[...]).astype(BF16), a2_ref[...]))
    kk = k * kk_ref[...]
    kk = kk / jnp.maximum(jnp.sqrt(_head_sum(kk * kk, bd_ref[...])), 1e-12)
    k_mod = k * (1.0 + (a - 1.0) * ka_ref[...])
    put(r_ref, r)
    put(w_ref, jnp.exp(-jnp.exp(w_log)))
    put(k_ref, k_mod)
    put(v_ref, v)
    put(na_ref, -kk)
    put(b_ref, kk * a)
    bonus_ref[...] = _head_sum(r * k_mod * rk_ref[...], bd_ref[...]) * v
    g_ref[...] = _silu(g)


def _rwkv_pre(hn, prev, mu, wr, wk, wv, wg, w0, w1, w2, a0, a1, a2, kk, ka, rk, bd, *, batch, time_minor):
    n, d = hn.shape
    tm = min(ROW_TILE, n)
    seq = n // batch
    nt = seq // tm if time_minor else 1
    row = pl.BlockSpec((tm, d), lambda i: (i, 0))
    ws = [mu, wr, wk, wv, wg, w0, w1, w2, a0, a1, a2, kk, ka, rk, bd]
    if time_minor:
        scan_spec = pl.BlockSpec((None, d, tm), lambda i: (i // nt, 0, i % nt))
        scan_shape = jax.ShapeDtypeStruct((batch, d, seq), F32)
    else:
        scan_spec, scan_shape = row, jax.ShapeDtypeStruct((n, d), F32)
    return pl.pallas_call(
        functools.partial(_rwkv_pre_kernel, time_minor=time_minor),
        grid=(n // tm,),
        in_specs=[row, row] + [_full(w.shape) for w in ws],
        out_specs=(scan_spec,) * 6 + (row, row),
        out_shape=(scan_shape,) * 6 + (jax.ShapeDtypeStruct((n, d), F32),) * 2,
        compiler_params=_cparams(("parallel",)), name="rwkv_pre",
    )(hn, prev, *ws)


def _swap_outer_kernel(*refs):
    n = len(refs) // 2
    for src, dst in zip(refs[:n], refs[n:]):
        for j in range(src.shape[1]):
            dst[:, j, :] = src[:, j, :].T


def _swap_outer(arrays):
    p, k, q = arrays[0].shape
    kb = 16
    n = len(arrays)
    return pl.pallas_call(
        _swap_outer_kernel,
        grid=(p // LANES, k // kb, q // LANES),
        in_specs=[pl.BlockSpec((LANES, kb, LANES), lambda a, b, c: (a, b, c))] * n,
        out_specs=(pl.BlockSpec((LANES, kb, LANES), lambda a, b, c: (c, b, a)),) * n,
        out_shape=(jax.ShapeDtypeStruct((q, k, p), F32),) * n,
        compiler_params=_cparams(("parallel", "parallel", "parallel")), name="swap_outer",
    )(*arrays)


def _wkv_scan_kernel(r_ref, w_ref, k_ref, v_ref, a_ref, b_ref, s0_ref, y_ref, s_ref):
    tt = r_ref.shape[0]
    n = s_ref.shape[0]

    @pl.when(pl.program_id(1) == 0)
    def _():
        s_ref[...] = s0_ref[...]

    sa0 = s_ref[0] * a_ref[0, 0:1, :]
    for k in range(1, n):
        sa0 = sa0 + s_ref[k] * a_ref[0, k:k + 1, :]

    def step(t, sa):
        tn = jnp.minimum(t + 1, tt - 1)
        row = lambda ref, k: ref[t, k:k + 1, :]
        v = v_ref[t]
        y = jnp.zeros_like(v)
        sa_next = jnp.zeros_like(v)
        for k in range(n):
            s = s_ref[k] * row(w_ref, k) + sa * row(b_ref, k) + v * row(k_ref, k)
            s_ref[k] = s
            y = y + s * row(r_ref, k)
            sa_next = sa_next + s * a_ref[tn, k:k + 1, :]
        y_ref[t] = y
        return sa_next

    lax.fori_loop(0, tt, step, sa0)


def _wkv_scan(r, w, k, v, a, b, s0):
    t, n, lanes = r.shape
    tt = min(SCAN_TIME_BLOCK, t)
    seq = pl.BlockSpec((tt, n, LANES), lambda g, i: (i, 0, g))
    st = pl.BlockSpec((n, n, LANES), lambda g, i: (0, 0, g))
    return pl.pallas_call(
        _wkv_scan_kernel,
        grid=(lanes // LANES, t // tt),
        in_specs=[seq] * 6 + [st],
        out_specs=(seq, st),
        out_shape=(jax.ShapeDtypeStruct((t, n, lanes), F32), jax.ShapeDtypeStruct((n, n, lanes), F32)),
        compiler_params=_cparams(("parallel", "arbitrary")), name="wkv_scan",
    )(r, w, k, v, a, b, s0)


def _rwkv_post_kernel(y_ref, bonus_ref, g_ref, h_ref, p_ref, lnw_ref, lnb_ref, bd_ref, wo_ref, png_ref,
                      pgw_ref, ppw_ref, fg_ref, o_ref, *, time_minor):
    bd = bd_ref[...]
    y = y_ref[...].T if time_minor else y_ref[...]
    inv_n = 1.0 / RW_N
    yc = y - _head_sum(y, bd) * inv_n
    var = _head_sum(yc * yc, bd) * inv_n
    yn = yc * lax.rsqrt(var + RW_EPS) * lnw_ref[...] + lnb_ref[...]
    mix = ((yn + bonus_ref[...]) * g_ref[...]).astype(BF16)
    h = h_ref[...] + _dot(mix, wo_ref[...])
    h = _ple(h, p_ref[...], png_ref[...], pgw_ref[...], ppw_ref[...])
    o_ref[...] = _rms(h, fg_ref[...])


def _rwkv_post(y, bonus, g, h, p2d, lnw, lnb, bd, wo, png, pgw, ppw, fg, *, time_minor):
    n, d = h.shape
    tm = min(ROW_TILE, n)
    row = lambda a: pl.BlockSpec((tm, a.shape[1]), lambda i: (i, 0))
    if time_minor:
        nt = y.shape[2] // tm
        y_spec = pl.BlockSpec((None, d, tm), lambda i: (i // nt, 0, i % nt))
    else:
        y_spec = row(y)
    ws = [lnw, lnb, bd, wo, png, pgw, ppw, fg]
    return pl.pallas_call(
        functools.partial(_rwkv_post_kernel, time_minor=time_minor),
        grid=(n // tm,),
        in_specs=[y_spec, row(bonus), row(g), row(h), row(p2d)] + [_full(w.shape) for w in ws],
        out_specs=row(h),
        out_shape=jax.ShapeDtypeStruct((n, d), F32),
        compiler_params=_cparams(("parallel",)), name="rwkv_post",
    )(y, bonus, g, h, p2d, *ws)


def _rope_tables(pos, rows):
    half = ROPE_DIM // 2
    inv = ROPE_THETA ** (-jnp.arange(half, dtype=F32) / half)
    ang = pos.astype(F32)[:, None] * inv[None, :]
    cos = jnp.cos(ang)
    sin = jnp.sin(ang)
    cos = jnp.tile(jnp.concatenate([cos, cos], axis=-1), (rows // pos.shape[0], MLA_HEADS))
    sin = jnp.tile(jnp.concatenate([-sin, sin], axis=-1), (rows // pos.shape[0], MLA_HEADS))
    return cos, sin


def _swap_halves(w):
    half = w.shape[-1] // 2
    return jnp.concatenate([w[..., half:], w[..., :half]], axis=-1)


def _to_lanes(x, batch, seq):
    heads = x.shape[1] // RW_N
    return x.reshape(batch, seq, heads, RW_N).transpose(1, 3, 0, 2).reshape(seq, RW_N, batch * heads)


def _from_lanes(y, batch, seq):
    heads = y.shape[2] // batch
    return y.reshape(seq, RW_N, batch, heads).transpose(2, 0, 3, 1).reshape(batch * seq, heads * RW_N)


def _trunk(x, p, pos, W, hg_s0, wkv_s0, shift_s0, paged):
    batch, seq, d = x.shape
    n = batch * seq
    x2d = x.reshape(n, d)
    row = lambda v: v.reshape(1, -1)

    cos_tab, sin_tab = _rope_tables(pos, max(seq, min(ROW_TILE, n)))
    zhg, qlat, qrope, ckv, kr, ckvb, krb, mg = _even_pre(
        x2d, cos_tab, sin_tab, row(W['mix_norm'][0]), W['hg_lb_logits'], W['w_in_hg'], W['w_in_mla'],
        row(W['mla_q_norm'][0]), row(W['mla_kv_norm'][0]), W['w_uq'], W['w_uk'], lb_rows=1)
    g_hg = row(W['hg_norm'][0])
    if paged is None:
        ohg, s_hg = _hgrn_prompt(zhg, g_hg, batch, seq)
        ctx = _attn_prompt(qlat, qrope, ckvb, krb, batch, seq)
    else:
        ohg, s_hg = _hgrn_sample(zhg, hg_s0, g_hg, seq)
        cache_ckv, cache_krope, page_table = paged
        stack = lambda q: q.reshape(MLA_HEADS, batch, seq, q.shape[-1]).transpose(1, 0, 2, 3).reshape(
            batch, MLA_HEADS * seq, q.shape[-1])
        pad = lambda a: jnp.pad(a.reshape(batch, seq, a.shape[-1]), ((0, 0), (0, 2 * SUBLANES - seq), (0, 0)))
        ctx = _attn_sample(page_table, stack(qlat), stack(qrope), pad(ckvb), pad(krb).transpose(0, 2, 1),
                           cache_ckv, cache_krope, seq)
        ctx = ctx.reshape(batch, MLA_HEADS, seq, KV_LORA).transpose(0, 2, 1, 3).reshape(n, MLA_HEADS * KV_LORA)
    h1, hn1 = _even_post(
        x2d, ohg, ctx, mg, p[0].reshape(n, -1), W['w_uv_bd'], W['w_out'], row(W['ple_norm'][0]),
        W['ple_gate_b'][0], W['ple_proj_b'][0], row(W['mix_norm'][1]))

    hn3 = hn1.reshape(batch, seq, d)
    prev = jnp.concatenate([shift_s0[:, None, :], hn3[:, :-1]], axis=1).reshape(n, d)
    heads = d // RW_N
    time_minor = seq % LANES == 0 and (batch * heads) % LANES == 0
    *scan_in, bonus, g = _rwkv_pre(
        hn1, prev, W['rw_mu'][0], W['w_r'], W['w_k'], W['w_v'], W['w_g'], row(W['rw_w0'][0]), W['w_w1'],
        W['w_w2'], row(W['rw_a0'][0]), W['w_a1'], W['w_a2'], row(W['rw_k_k'][0]), row(W['rw_k_a'][0]),
        row(W['rw_r_k'][0]), W['head_bd'], batch=batch, time_minor=time_minor)
    if time_minor:
        scan_in = _swap_outer([a.reshape(batch * heads, RW_N, seq) for a in scan_in])
    else:
        scan_in = [_to_lanes(a, batch, seq) for a in scan_in]
    s0 = wkv_s0.transpose(3, 2, 0, 1).reshape(RW_N, RW_N, -1)
    y, s_wkv = _wkv_scan(*scan_in, s0)
    s_wkv = s_wkv.reshape(RW_N, RW_N, batch, heads).transpose(2, 3, 1, 0)
    y = _swap_outer([y])[0].reshape(batch, d, seq) if time_minor else _from_lanes(y, batch, seq)
    out = _rwkv_post(
        y, bonus, g, h1, p[1].reshape(n, -1), row(W['rw_ln_w'][0]), row(W['rw_ln_b'][0]), W['head_bd'],
        W['w_o'], row(W['ple_norm'][1]), W['ple_gate_b'][1], W['ple_proj_b'][1], row(W['final_norm']),
        time_minor=time_minor)
    return (out.reshape(batch, seq, d), ckv.reshape(1, batch, seq, KV_LORA), kr.reshape(1, batch, seq, ROPE_DIM),
            s_hg[None], s_wkv[None], hn3[:, -1][None])


def kernel(x_prompt, x_sample, cache_ckv, cache_krope, state_hgrn, state_wkv, state_shift, page_table, p_prompt, p_sample, mix_norm, ev_w_in, hg_lb_logits, hg_norm, mla_q_norm, mla_w_uq, mla_kv_norm, mla_w_uk, mla_w_uv, ev_w_out, rw_mu, rw_w_rkvg, rw_w0, rw_w1, rw_w2, rw_a0, rw_a1, rw_a2, rw_k_k, rw_k_a, rw_r_k, rw_ln_w, rw_ln_b, rw_w_o, ple_norm, ple_gate, ple_proj, final_norm):
    bf = lambda a: a.astype(BF16)
    d = x_prompt.shape[-1]
    w_in = ev_w_in[0]
    o = np.cumsum([0, HG_WIDTH, HG_WIDTH, HG_HEADS * HG_DV, HG_HEADS * HG_DV, Q_LORA, KV_LORA, ROPE_DIM, MLA_WIDTH])
    cq, ckv_w, kr_w, mg_w = (w_in[:, o[4]:o[5]], w_in[:, o[5]:o[6]], w_in[:, o[6]:o[7]], w_in[:, o[7]:o[8]])
    uq = mla_w_uq[0].reshape(Q_LORA, MLA_HEADS, NOPE_DIM + ROPE_DIM)
    uq_rope = uq[:, :, NOPE_DIM:]
    uv_bd = jnp.zeros((MLA_HEADS, KV_LORA, MLA_HEADS, V_DIM), F32)
    uv_bd = uv_bd.at[jnp.arange(MLA_HEADS), :, jnp.arange(MLA_HEADS), :].set(mla_w_uv[0].transpose(0, 2, 1))
    hid = np.arange(MXU_DIM) // RW_N
    W = dict(
        mix_norm=mix_norm, hg_lb_logits=hg_lb_logits, hg_norm=hg_norm, mla_q_norm=mla_q_norm,
        mla_kv_norm=mla_kv_norm, ple_norm=ple_norm, final_norm=final_norm, rw_mu=rw_mu, rw_w0=rw_w0, rw_a0=rw_a0,
        rw_k_k=rw_k_k, rw_k_a=rw_k_a, rw_ln_w=rw_ln_w, rw_ln_b=rw_ln_b, rw_r_k=rw_r_k.reshape(rw_r_k.shape[0], -1),
        w_in_hg=bf(w_in[:, :o[4]]),
        w_in_mla=bf(jnp.concatenate([cq, ckv_w, mg_w, kr_w, _swap_halves(kr_w)], axis=-1)),
        w_uq=bf(jnp.concatenate([uq[:, :, :NOPE_DIM].reshape(Q_LORA, -1), uq_rope.reshape(Q_LORA, -1),
                                 _swap_halves(uq_rope).reshape(Q_LORA, -1)], axis=-1)),
        w_uk=bf(mla_w_uk[0]),
        w_uv_bd=bf(uv_bd.reshape(MLA_HEADS * KV_LORA, MLA_WIDTH)),
        w_out=bf(ev_w_out[0]),
        ple_gate_b=bf(ple_gate), ple_proj_b=bf(ple_proj),
        w_r=bf(rw_w_rkvg[0, 0]), w_k=bf(rw_w_rkvg[0, 1]), w_v=bf(rw_w_rkvg[0, 2]), w_g=bf(rw_w_rkvg[0, 3]),
        w_w1=bf(rw_w1[0]), w_w2=bf(rw_w2[0]), w_a1=bf(rw_a1[0]), w_a2=bf(rw_a2[0]), w_o=bf(rw_w_o[0]),
        head_bd=jnp.asarray(hid[:, None] == hid[None, :], BF16),
    )
    bp, tp, _ = x_prompt.shape
    bs, ts, _ = x_sample.shape
    past_len = page_table.shape[1] * cache_ckv.shape[2]
    heads = d // RW_N
    yp, ckv_p, kr_p, hg_p, wkv_p, sh_p = _trunk(
        x_prompt, p_prompt, jnp.arange(tp), W, None, jnp.zeros((bp, heads, RW_N, RW_N), F32),
        jnp.zeros((bp, d), F32), None)
    ys, ckv_s, kr_s, hg_s, wkv_s, sh_s = _trunk(
        x_sample, p_sample, past_len + jnp.arange(ts), W, state_hgrn[0], state_wkv[0], state_shift[0],
        (cache_ckv.reshape(cache_ckv.shape[1:]), jnp.swapaxes(cache_krope.reshape(cache_krope.shape[1:]), 1, 2),
         page_table))
    return (yp, ys, ckv_p, kr_p, ckv_s, kr_s, hg_p, hg_s, wkv_p, wkv_s, sh_p, sh_s)
```

```python
import functools

import jax
import jax.numpy as jnp
import numpy as np
from jax import lax
from jax.experimental import pallas as pl
from jax.experimental.pallas import tpu as pltpu

F32 = jnp.float32
BF16 = jnp.bfloat16

NORM_EPS = 1e-6
HG_HEADS = 4
HG_DK = 128
HG_DV = 128
HG_WIDTH = HG_HEADS * HG_DK
MLA_HEADS = 8
Q_LORA = 384
KV_LORA = 256
NOPE_DIM = 64
ROPE_DIM = 32
V_DIM = 64
MLA_WIDTH = MLA_HEADS * V_DIM
MLA_SCALE = (NOPE_DIM + ROPE_DIM) ** -0.5
ROPE_THETA = 10000.0
RW_N = 64
RW_EPS = 64e-5

LANES = 128
SUBLANES = 8
MXU_DIM = 256
VMEM_LIMIT_BYTES = 56 * 1024 * 1024

ROW_TILE = 256
HG_CHUNK = 64
HG_BLOCK = SUBLANES
HG_TIME_BLOCK = 1024
ATT_BLOCK = 256
PAGES_PER_STEP = 32
PAGES_PER_BLOCK = 16
PAGE_RING = 3
SCAN_TIME_BLOCK = 32


def _cparams(sem):
    return pltpu.CompilerParams(dimension_semantics=sem, vmem_limit_bytes=VMEM_LIMIT_BYTES)


def _rms(x, g):
    return x * lax.rsqrt(jnp.mean(x * x, axis=-1, keepdims=True) + NORM_EPS) * g


def _sigmoid(x):
    return 1.0 / (1.0 + jnp.exp(-x))


def _silu(x):
    return x * _sigmoid(x)


def _dot(a, b):
    return jnp.dot(a, b, preferred_element_type=F32)


def _dot_nt(a, b):
    return lax.dot_general(a, b, (((1,), (1,)), ((), ())), preferred_element_type=F32)


def _head_sum(x, bd):
    hi = x.astype(BF16)
    lo = (x - hi.astype(F32)).astype(BF16)
    outs = []
    for c in range(x.shape[-1] // MXU_DIM):
        sl = slice(c * MXU_DIM, (c + 1) * MXU_DIM)
        outs.append(_dot(hi[:, sl], bd) + _dot(lo[:, sl], bd))
    return jnp.concatenate(outs, axis=-1)


def _full(shape):
    nd = len(shape)
    return pl.BlockSpec(shape, lambda *_: (0,) * nd)


def _even_pre_kernel(x_ref, cos_ref, sin_ref, g_ref, lbl_ref, w1_ref, w2_ref, qg_ref, kvg_ref, wuq_ref,
                     wuk_ref, zhg_ref, qlat_ref, qrope_ref, ckv_ref, kr_ref, ckvb_ref, krb_ref, mg_ref,
                     *, lb_rows):
    hn = _rms(x_ref[...], g_ref[...]).astype(BF16)
    z1 = _dot(hn, w1_ref[...])
    z2 = _dot(hn, w2_ref[...])
    lg = lbl_ref[...]
    e = jnp.exp(lg - jnp.max(lg, axis=0, keepdims=True))
    p = e / jnp.sum(e, axis=0, keepdims=True)
    lb = jnp.sum(p[:lb_rows], axis=0, keepdims=True)
    W = HG_WIDTH
    f = lb + (1.0 - lb) * _sigmoid(z1[:, W:2 * W])
    zhg_ref[:, 0:W] = _silu(z1[:, 0:W])
    zhg_ref[:, W:2 * W] = jnp.log(f)
    zhg_ref[:, 2 * W:3 * W] = z1[:, 2 * W:3 * W]
    zhg_ref[:, 3 * W:4 * W] = _silu(z1[:, 3 * W:4 * W])
    o_kv = Q_LORA
    o_mg = o_kv + KV_LORA
    o_kr = o_mg + MLA_WIDTH
    o_krs = o_kr + ROPE_DIM
    cqn = _rms(z2[:, 0:o_kv], qg_ref[...]).astype(BF16)
    qf = _dot(cqn, wuq_ref[...])
    cos = cos_ref[...]
    sin = sin_ref[...]
    n_nope = MLA_HEADS * NOPE_DIM
    n_rope = MLA_HEADS * ROPE_DIM
    qr = (qf[:, n_nope:n_nope + n_rope] * cos + qf[:, n_nope + n_rope:n_nope + 2 * n_rope] * sin) * MLA_SCALE
    for h in range(MLA_HEADS):
        qrope_ref[h] = qr[:, h * ROPE_DIM:(h + 1) * ROPE_DIM].astype(BF16)
        qn = qf[:, h * NOPE_DIM:(h + 1) * NOPE_DIM].astype(BF16)
        qlat_ref[h] = (_dot(qn, wuk_ref[h]) * MLA_SCALE).astype(BF16)
    ckv = _rms(z2[:, o_kv:o_mg], kvg_ref[...])
    ckv_ref[...] = ckv
    ckvb_ref[...] = ckv.astype(BF16)
    kr = z2[:, o_kr:o_krs] * cos[:, :ROPE_DIM] + z2[:, o_krs:o_krs + ROPE_DIM] * sin[:, :ROPE_DIM]
    kr_ref[...] = kr
    krb_ref[...] = kr.astype(BF16)
    mg_ref[...] = _silu(z2[:, o_mg:o_kr])


def _even_pre(x2d, cos_tab, sin_tab, g, lb_logits, w1, w2, qg, kvg, wuq, wuk, *, lb_rows):
    n, d = x2d.shape
    tm = min(ROW_TILE, n)
    n_tab = cos_tab.shape[0] // tm
    row = lambda i: (i, 0)
    tab = lambda i: (i % n_tab, 0)
    out_shape = (
        jax.ShapeDtypeStruct((n, 4 * HG_WIDTH), F32),
        jax.ShapeDtypeStruct((MLA_HEADS, n, KV_LORA), BF16),
        jax.ShapeDtypeStruct((MLA_HEADS, n, ROPE_DIM), BF16),
        jax.ShapeDtypeStruct((n, KV_LORA), F32),
        jax.ShapeDtypeStruct((n, ROPE_DIM), F32),
        jax.ShapeDtypeStruct((n, KV_LORA), BF16),
        jax.ShapeDtypeStruct((n, ROPE_DIM), BF16),
        jax.ShapeDtypeStruct((n, MLA_WIDTH), F32),
    )
    out_specs = (
        pl.BlockSpec((tm, 4 * HG_WIDTH), row),
        pl.BlockSpec((MLA_HEADS, tm, KV_LORA), lambda i: (0, i, 0)),
        pl.BlockSpec((MLA_HEADS, tm, ROPE_DIM), lambda i: (0, i, 0)),
        pl.BlockSpec((tm, KV_LORA), row),
        pl.BlockSpec((tm, ROPE_DIM), row),
        pl.BlockSpec((tm, KV_LORA), row),
        pl.BlockSpec((tm, ROPE_DIM), row),
        pl.BlockSpec((tm, MLA_WIDTH), row),
    )
    in_specs = [
        pl.BlockSpec((tm, d), row),
        pl.BlockSpec((tm, cos_tab.shape[1]), tab),
        pl.BlockSpec((tm, sin_tab.shape[1]), tab),
        _full(g.shape), _full(lb_logits.shape), _full(w1.shape), _full(w2.shape), _full(qg.shape),
        _full(kvg.shape), _full(wuq.shape), _full(wuk.shape),
    ]
    return pl.pallas_call(
        functools.partial(_even_pre_kernel, lb_rows=lb_rows),
        grid=(n // tm,), in_specs=in_specs, out_specs=out_specs, out_shape=out_shape,
        compiler_params=_cparams(("parallel",)), name="even_pre",
    )(x2d, cos_tab, sin_tab, g, lb_logits, w1, w2, qg, kvg, wuq, wuk)


def _tril_ones(n, block):
    r = lax.broadcasted_iota(jnp.int32, (n, n), 0)
    c = lax.broadcasted_iota(jnp.int32, (n, n), 1)
    return ((r >= c) & ((r // block) == (c // block))).astype(F32)


def _hgrn_exact_blocks(q, k, v, b, block):
    rows = q.shape[0]
    rid = lax.broadcasted_iota(jnp.int32, (rows, 1), 0) % block
    o = jnp.sum(q * k, axis=-1, keepdims=True) * v
    for d in range(1, block):
        kd = pltpu.roll(k, d, 0)
        bd = pltpu.roll(b, d, 0)
        vd = pltpu.roll(v, d, 0)
        w = jnp.sum(q * kd * jnp.exp(jnp.minimum(b - bd, 0.0)), axis=-1, keepdims=True)
        o = o + jnp.where(rid >= d, w, 0.0) * vd
    return o


def _hgrn_finish(o, g, gate):
    return (o * lax.rsqrt(jnp.mean(o * o, axis=-1, keepdims=True) + NORM_EPS) * g * gate).astype(BF16)


def _hgrn_prompt_kernel(q_ref, lf_ref, v_ref, gt_ref, g_ref, o_ref, s_ref, st_ref):
    C = HG_CHUNK
    tb = q_ref.shape[0]
    tril = _tril_ones(C, C)
    rid = lax.broadcasted_iota(jnp.int32, (C, 1), 0)
    rr = lax.broadcasted_iota(jnp.int32, (C, C), 0)
    cc = lax.broadcasted_iota(jnp.int32, (C, C), 1)
    g = g_ref[...]

    @pl.when(pl.program_id(1) == 0)
    def _():
        st_ref[...] = jnp.zeros(st_ref.shape, F32)

    def head_chunk(sl, h):
        hs = slice(h * HG_DK, (h + 1) * HG_DK)
        st = st_ref[h]
        q = q_ref[sl, hs]
        lf = lf_ref[sl, hs]
        v = v_ref[sl, hs]
        k = 1.0 - jnp.exp(lf)
        b = jnp.dot(tril, lf, precision=lax.Precision.HIGHEST, preferred_element_type=F32)
        o = _dot_nt((q * jnp.exp(b)).astype(BF16), st.astype(BF16))
        o = o + _hgrn_exact_blocks(q, k, v, b, HG_BLOCK)
        att = jnp.zeros((C, C), F32)
        m = HG_BLOCK
        while 2 * m <= C:
            nb = C // (2 * m)
            b3 = b.reshape(nb, 2 * m, HG_DK)
            ref = jnp.broadcast_to(b3[:, m - 1:m, :], (nb, 2 * m, HG_DK)).reshape(C, HG_DK)
            upper = (rid % (2 * m)) >= m
            qt = q * jnp.where(upper, jnp.exp(jnp.minimum(b - ref, 0.0)), 0.0)
            kt = k * jnp.where(upper, 0.0, jnp.exp(jnp.minimum(ref - b, 0.0)))
            a = _dot_nt(qt.astype(BF16), kt.astype(BF16))
            if 2 * m < C:
                a = jnp.where((rr // (2 * m)) == (cc // (2 * m)), a, 0.0)
            att = att + a
            m *= 2
        o = o + _dot(att.astype(BF16), v.astype(BF16))
        o_ref[sl, hs] = _hgrn_finish(o, g, gt_ref[sl, hs])
        bend = b[C - 1:C, :]
        kt = k * jnp.exp(bend - b)
        st_ref[h] = st * jnp.exp(bend) + _dot(v.T.astype(BF16), kt.astype(BF16))

    def chunk(c, carry):
        sl = pl.ds(pl.multiple_of(c * C, C), C)
        for h in range(HG_HEADS):
            head_chunk(sl, h)
        return carry

    lax.fori_loop(0, tb // C, chunk, 0)

    @pl.when(pl.program_id(1) == pl.num_programs(1) - 1)
    def _():
        for h in range(HG_HEADS):
            s_ref[h] = st_ref[h].T


def _hgrn_prompt(zhg, g, batch, seq):
    tb = min(HG_TIME_BLOCK, seq)
    nt = seq // tb
    blk = lambda off: pl.BlockSpec((tb, HG_WIDTH), lambda b, i: (b * nt + i, off))
    return pl.pallas_call(
        _hgrn_prompt_kernel,
        grid=(batch, nt),
        in_specs=[blk(0), blk(1), blk(2), blk(3), _full(g.shape)],
        out_specs=(pl.BlockSpec((tb, HG_HEADS * HG_DV), lambda b, i: (b * nt + i, 0)),
                   pl.BlockSpec((None, HG_HEADS, HG_DK, HG_DV), lambda b, i: (b, 0, 0, 0))),
        out_shape=(jax.ShapeDtypeStruct((batch * seq, HG_HEADS * HG_DV), BF16),
                   jax.ShapeDtypeStruct((batch, HG_HEADS, HG_DK, HG_DV), F32)),
        scratch_shapes=[pltpu.VMEM((HG_HEADS, HG_DV, HG_DK), F32)],
        compiler_params=_cparams(("parallel", "arbitrary")), name="hgrn_prompt",
    )(zhg, zhg, zhg, zhg, g)


def _hgrn_sample_kernel(z_ref, s0_ref, g_ref, o_ref, s_ref, *, seq):
    rows = z_ref.shape[0]
    nb = rows // seq
    tril = _tril_ones(rows, seq)
    rb = lax.broadcasted_iota(jnp.int32, (rows, 1), 0) // seq
    cb = lax.broadcasted_iota(jnp.int32, (1, rows), 1) // seq
    g = g_ref[...]
    W = HG_WIDTH
    for h in range(HG_HEADS):
        hs = slice(h * HG_DK, (h + 1) * HG_DK)
        q = z_ref[:, hs]
        lf = z_ref[:, W + h * HG_DK:W + (h + 1) * HG_DK]
        v = z_ref[:, 2 * W + h * HG_DV:2 * W + (h + 1) * HG_DV]
        gate = z_ref[:, 3 * W + h * HG_DV:3 * W + (h + 1) * HG_DV]
        k = 1.0 - jnp.exp(lf)
        b = jnp.dot(tril, lf, precision=lax.Precision.HIGHEST, preferred_element_type=F32)
        o = _hgrn_exact_blocks(q, k, v, b, seq)
        qe = (q * jnp.exp(b)).astype(BF16)
        bt = b.T
        vb = v.astype(BF16)
        for i in range(nb):
            s0 = s0_ref[i, h]
            o = o + jnp.where(rb == i, _dot(qe, s0.astype(BF16)), 0.0)
            last = i * seq + seq - 1
            bend_row = b[last:last + 1, :]
            kt = jnp.where(rb == i, k * jnp.exp(bend_row - b), 0.0)
            ktt = kt.T.astype(BF16)
            s_ref[i, h] = s0 * jnp.exp(bt[:, last:last + 1]) + _dot(ktt, vb)
        o_ref[:, hs] = _hgrn_finish(o, g, gate)


def _hgrn_sample(zhg, s0, g, seq):
    n = zhg.shape[0]
    nb = 8
    rows = nb * seq
    return pl.pallas_call(
        functools.partial(_hgrn_sample_kernel, seq=seq),
        grid=(n // rows,),
        in_specs=[pl.BlockSpec((rows, 4 * HG_WIDTH), lambda i: (i, 0)),
                  pl.BlockSpec((nb, HG_HEADS, HG_DK, HG_DV), lambda i: (i, 0, 0, 0)),
                  _full(g.shape)],
        out_specs=(pl.BlockSpec((rows, HG_HEADS * HG_DV), lambda i: (i, 0)),
                   pl.BlockSpec((nb, HG_HEADS, HG_DK, HG_DV), lambda i: (i, 0, 0, 0))),
        out_shape=(jax.ShapeDtypeStruct((n, HG_HEADS * HG_DV), BF16),
                   jax.ShapeDtypeStruct(s0.shape, F32)),
        compiler_params=_cparams(("parallel",)), name="hgrn_sample",
    )(zhg, s0, g)


def _lane_tile(x, width):
    if width <= LANES:
        return x[:, :width]
    return jnp.concatenate([x] * (width // LANES), axis=1)


def _softmax_init(m_ref, l_ref, acc_ref):
    m_ref[...] = jnp.full(m_ref.shape, -jnp.inf, F32)
    l_ref[...] = jnp.zeros(l_ref.shape, F32)
    acc_ref[...] = jnp.zeros(acc_ref.shape, F32)


def _online_step(s, kv, m_old, l_old, acc_old):
    m_new = jnp.maximum(m_old, jnp.max(s, axis=-1, keepdims=True))
    alpha = jnp.exp(m_old - m_new)
    p = jnp.exp(s - _lane_tile(m_new, s.shape[1]))
    l_new = alpha * l_old + jnp.sum(p, axis=-1, keepdims=True)
    acc_new = _lane_tile(alpha, acc_old.shape[1]) * acc_old + _dot(p.astype(BF16), kv)
    return m_new, l_new, acc_new


def _online_update(s, kv, rows, m_ref, l_ref, acc_ref):
    m_ref[rows, :], l_ref[rows, :], acc_ref[rows, :] = _online_step(
        s, kv, m_ref[rows, :], l_ref[rows, :], acc_ref[rows, :])


def _attn_prompt_kernel(ql_ref, qr_ref, kv_ref, kr_ref, o_ref, m_ref, l_ref, acc_ref):
    tq = ql_ref.shape[1]
    i = pl.program_id(1)
    _softmax_init(m_ref, l_ref, acc_ref)
    n_split = 2
    hs = MLA_HEADS // n_split
    mh = hs * tq
    qpos = lax.broadcasted_iota(jnp.int32, (mh, tq), 0) % tq
    kpos = lax.broadcasted_iota(jnp.int32, (mh, tq), 1)

    def block(j, masked):
        sl = pl.ds(pl.multiple_of(j * tq, tq), tq)
        kv = kv_ref[sl, :]
        kr = kr_ref[sl, :]
        for r in range(n_split):
            ql = ql_ref[r * hs:(r + 1) * hs].reshape(mh, KV_LORA)
            qr = qr_ref[r * hs:(r + 1) * hs].reshape(mh, ROPE_DIM)
            s = _dot_nt(ql, kv) + _dot_nt(qr, kr)
            if masked:
                s = jnp.where(kpos <= qpos, s, -jnp.inf)
            _online_update(s, kv, slice(r * mh, (r + 1) * mh), m_ref, l_ref, acc_ref)

    def body(j, carry):
        block(j, False)
        return carry

    lax.fori_loop(0, i, body, 0)
    block(i, True)
    for h in range(MLA_HEADS):
        rows = slice(h * tq, (h + 1) * tq)
        out = acc_ref[rows, :] / _lane_tile(l_ref[rows, :], KV_LORA)
        o_ref[:, h * KV_LORA:(h + 1) * KV_LORA] = out.astype(BF16)


def _attn_prompt(qlat, qrope, ckvb, krb, batch, seq):
    tq = ATT_BLOCK
    nq = seq // tq
    rows = MLA_HEADS * tq
    return pl.pallas_call(
        _attn_prompt_kernel,
        grid=(batch, nq),
        in_specs=[pl.BlockSpec((MLA_HEADS, tq, KV_LORA), lambda b, i: (0, b * nq + i, 0)),
                  pl.BlockSpec((MLA_HEADS, tq, ROPE_DIM), lambda b, i: (0, b * nq + i, 0)),
                  pl.BlockSpec((seq, KV_LORA), lambda b, i: (b, 0)),
                  pl.BlockSpec((seq, ROPE_DIM), lambda b, i: (b, 0))],
        out_specs=pl.BlockSpec((tq, MLA_HEADS * KV_LORA), lambda b, i: (b * nq + i, 0)),
        out_shape=jax.ShapeDtypeStruct((batch * seq, MLA_HEADS * KV_LORA), BF16),
        scratch_shapes=[pltpu.VMEM((rows, LANES), F32), pltpu.VMEM((rows, LANES), F32),
                        pltpu.VMEM((rows, KV_LORA), F32)],
        compiler_params=_cparams(("parallel", "arbitrary")), name="attn_prompt",
    )(qlat, qrope, ckvb, krb)


def _attn_sample_kernel(pt_ref, ql_ref, qr_ref, kvn_ref, krn_ref, ckv_hbm, kr_hbm, o_ref,
                        kvbuf, krbuf, sems, m_ref, l_ref, acc_ref, *, seq, G):
    page = kvbuf.shape[1] // G
    n_groups = pl.num_programs(1)
    g = pl.program_id(1)
    step = pl.program_id(0) * n_groups + g
    total = pl.num_programs(0) * n_groups

    def page_copies(s):
        slot = s % PAGE_RING
        bb = s // n_groups
        first = (s % n_groups) * G
        out = []
        for j in range(G):
            pg = pt_ref[bb, first + j]
            out.append(pltpu.make_async_copy(ckv_hbm.at[pg], kvbuf.at[slot, pl.ds(j * page, page), :],
                                             sems.at[0, slot]))
            out.append(pltpu.make_async_copy(kr_hbm.at[pg], krbuf.at[slot, j], sems.at[1, slot]))
        return out

    @pl.when(step == 0)
    def _():
        for s in range(PAGE_RING - 1):
            @pl.when(s < total)
            def _():
                for c in page_copies(s):
                    c.start()

    @pl.when(g == 0)
    def _():
        _softmax_init(m_ref, l_ref, acc_ref)

    for c in page_copies(step):
        c.wait()
    slot = step % PAGE_RING
    ql = ql_ref[...]
    qr = qr_ref[...]
    state = (m_ref[...], l_ref[...], acc_ref[...])
    sub = min(PAGES_PER_BLOCK, G)
    for c in range(0, G, sub):
        kv = kvbuf[slot, c * page:(c + sub) * page, :].astype(BF16)
        kr = jnp.concatenate([krbuf[slot, c + j].astype(BF16) for j in range(sub)], axis=1)
        state = _online_step(_dot_nt(ql, kv) + _dot(qr, kr), kv, *state)
    m_ref[...], l_ref[...], acc_ref[...] = state

    @pl.when(step + PAGE_RING - 1 < total)
    def _():
        for c in page_copies(step + PAGE_RING - 1):
            c.start()

    @pl.when(g == n_groups - 1)
    def _():
        kvn = kvn_ref[...]
        sn = _dot_nt(ql, kvn) + _dot(qr, krn_ref[...])
        qpos = lax.broadcasted_iota(jnp.int32, sn.shape, 0) % seq
        kpos = lax.broadcasted_iota(jnp.int32, sn.shape, 1)
        _, l, acc = _online_step(jnp.where(kpos <= qpos, sn, -jnp.inf), kvn, *state)
        o_ref[...] = (acc / _lane_tile(l, KV_LORA)).astype(BF16)


def _attn_sample(page_table, qlat, qrope, kv_new, kr_new_t, cache_ckv, cache_krope_t, seq):
    batch, n_pages = page_table.shape
    G = min(PAGES_PER_STEP, n_pages)
    page = cache_ckv.shape[1]
    rows = qlat.shape[1]
    npad = kv_new.shape[1]

    per_b = lambda r, width: pl.BlockSpec((None, r, width), lambda b, g, pt: (b, 0, 0))
    in_hbm = pl.BlockSpec(memory_space=pl.ANY)
    grid_spec = pltpu.PrefetchScalarGridSpec(
        num_scalar_prefetch=1,
        grid=(batch, n_pages // G),
        in_specs=[per_b(rows, KV_LORA), per_b(rows, ROPE_DIM), per_b(npad, KV_LORA), per_b(ROPE_DIM, npad),
                  in_hbm, in_hbm],
        out_specs=per_b(rows, KV_LORA),
        scratch_shapes=[pltpu.VMEM((PAGE_RING, G * page, KV_LORA), F32),
                        pltpu.VMEM((PAGE_RING, G, ROPE_DIM, page), F32),
                        pltpu.SemaphoreType.DMA((2, PAGE_RING)),
                        pltpu.VMEM((rows, LANES), F32), pltpu.VMEM((rows, LANES), F32),
                        pltpu.VMEM((rows, KV_LORA), F32)],
    )
    return pl.pallas_call(
        functools.partial(_attn_sample_kernel, seq=seq, G=G),
        grid_spec=grid_spec,
        out_shape=jax.ShapeDtypeStruct((batch, rows, KV_LORA), BF16),
        compiler_params=_cparams(("arbitrary", "arbitrary")), name="attn_sample",
    )(page_table, qlat, qrope, kv_new, kr_new_t, cache_ckv, cache_krope_t)


def _ple(h, p, png, pgw, ppw):
    gate = _sigmoid(_dot(_rms(h, png).astype(BF16), pgw))
    return h + gate * _dot(p.astype(BF16), ppw)


def _even_post_kernel(x_ref, ohg_ref, ctx_ref, mg_ref, p_ref, wuv_ref, wout_ref, png_ref, pgw_ref, ppw_ref,
                      ng_ref, h_ref, hn_ref):
    o_mla = (_dot(ctx_ref[...], wuv_ref[...]) * mg_ref[...]).astype(BF16)
    n_hg = HG_HEADS * HG_DV
    h = x_ref[...] + _dot(ohg_ref[...], wout_ref[0:n_hg, :]) + _dot(o_mla, wout_ref[n_hg:, :])
    h = _ple(h, p_ref[...], png_ref[...], pgw_ref[...], ppw_ref[...])
    h_ref[...] = h
    hn_ref[...] = _rms(h, ng_ref[...])


def _layer_rows(p3d, layer, tm):
    return pl.BlockSpec((None, tm, p3d.shape[2]), lambda i: (layer, i, 0))


def _even_post(x2d, ohg, ctx, mg, p3d, layer, wuv, wout, png, pgw, ppw, ng):
    n, d = x2d.shape
    tm = min(ROW_TILE, n)
    row = lambda a: pl.BlockSpec((tm, a.shape[1]), lambda i: (i, 0))
    return pl.pallas_call(
        _even_post_kernel,
        grid=(n // tm,),
        in_specs=[row(x2d), row(ohg), row(ctx), row(mg), _layer_rows(p3d, layer, tm), _full(wuv.shape),
                  _full(wout.shape), _full(png.shape), _full(pgw.shape), _full(ppw.shape), _full(ng.shape)],
        out_specs=(row(x2d), row(x2d)),
        out_shape=(jax.ShapeDtypeStruct((n, d), F32), jax.ShapeDtypeStruct((n, d), F32)),
        compiler_params=_cparams(("parallel",)), name="even_post",
    )(x2d, ohg, ctx, mg, p3d, wuv, wout, png, pgw, ppw, ng)


def _rwkv_pre_kernel(hn_ref, pv_ref, mu_ref, wr_ref, wk_ref, wv_ref, wg_ref, w0_ref, w1_ref, w2_ref, a0_ref,
                     a1_ref, a2_ref, kk_ref, ka_ref, rk_ref, bd_ref,
                     r_ref, w_ref, k_ref, v_ref, na_ref, b_ref, bonus_ref, g_ref, *, time_minor):
    def put(ref, x):
        ref[...] = x.T if time_minor else x

    hn = hn_ref[...]
    if time_minor:
        first = lax.broadcasted_iota(jnp.int32, (hn.shape[0], 1), 0) == 0
        prev = jnp.where(first, pv_ref[...], pltpu.roll(hn, 1, 0))
    else:
        prev = pv_ref[...]
    dlt = prev - hn
    mix = lambda j: (hn + dlt * mu_ref[j:j + 1, :]).astype(BF16)
    r = _dot(mix(0), wr_ref[...])
    k = _dot(mix(1), wk_ref[...])
    v = _dot(mix(2), wv_ref[...])
    g = _dot(mix(3), wg_ref[...])
    wl = w0_ref[...] + _dot(jnp.tanh(_dot(mix(4), w1_ref[...])).astype(BF16), w2_ref[...])
    w_log = -(jnp.maximum(-wl, 0.0) + jnp.log(1.0 + jnp.exp(-jnp.abs(wl)))) - 0.5
    a = _sigmoid(a0_ref[...] + _dot(_dot(mix(5), a1_ref[...]).astype(BF16), a2_ref[...]))
    kk = k * kk_ref[...]
    kk = kk / jnp.maximum(jnp.sqrt(_head_sum(kk * kk, bd_ref[...])), 1e-12)
    k_mod = k * (1.0 + (a - 1.0) * ka_ref[...])
    put(r_ref, r)
    put(w_ref, jnp.exp(-jnp.exp(w_log)))
    put(k_ref, k_mod)
    put(v_ref, v)
    put(na_ref, -kk)
    put(b_ref, kk * a)
    bonus_ref[...] = _head_sum(r * k_mod * rk_ref[...], bd_ref[...]) * v
    g_ref[...] = _silu(g)


def _rwkv_pre(hn, prev, mu, wr, wk, wv, wg, w0, w1, w2, a0, a1, a2, kk, ka, rk, bd, *, batch, time_minor):
    n, d = hn.shape
    tm = min(ROW_TILE, n)
    seq = n // batch
    nt = seq // tm if time_minor else 1
    row = pl.BlockSpec((tm, d), lambda i: (i, 0))
    ws = [mu, wr, wk, wv, wg, w0, w1, w2, a0, a1, a2, kk, ka, rk, bd]
    if time_minor:
        scan_spec = pl.BlockSpec((None, d, tm), lambda i: (i // nt, 0, i % nt))
        scan_shape = jax.ShapeDtypeStruct((batch, d, seq), F32)
        prev_spec = pl.BlockSpec((None, 1, d), lambda i: (i, 0, 0))
    else:
        scan_spec, scan_shape, prev_spec = row, jax.ShapeDtypeStruct((n, d), F32), row
    return pl.pallas_call(
        functools.partial(_rwkv_pre_kernel, time_minor=time_minor),
        grid=(n // tm,),
        in_specs=[row, prev_spec] + [_full(w.shape) for w in ws],
        out_specs=(scan_spec,) * 6 + (row, row),
        out_shape=(scan_shape,) * 6 + (jax.ShapeDtypeStruct((n, d), F32),) * 2,
        compiler_params=_cparams(("parallel",)), name="rwkv_pre",
    )(hn, prev, *ws)


def _swap_outer_kernel(*refs):
    n = len(refs) // 2
    for src, dst in zip(refs[:n], refs[n:]):
        p, kb, q = src.shape
        src2 = src.reshape(p * kb, q)
        dst2 = dst.reshape(q * kb, p)
        for j in range(kb):
            dst2[pl.ds(j, q, stride=kb), :] = src2[pl.ds(j, p, stride=kb), :].T


def _swap_outer(arrays):
    p, k, q = arrays[0].shape
    kb = SUBLANES
    n = len(arrays)
    return pl.pallas_call(
        _swap_outer_kernel,
        grid=(p // LANES, k // kb, q // LANES),
        in_specs=[pl.BlockSpec((LANES, kb, LANES), lambda a, b, c: (a, b, c))] * n,
        out_specs=(pl.BlockSpec((LANES, kb, LANES), lambda a, b, c: (c, b, a)),) * n,
        out_shape=(jax.ShapeDtypeStruct((q, k, p), F32),) * n,
        compiler_params=_cparams(("parallel", "parallel", "parallel")), name="swap_outer",
    )(*arrays)


def _wkv_scan_kernel(r_ref, w_ref, k_ref, v_ref, a_ref, b_ref, s0_ref, y_ref, so_ref, s_ref, *, value_major):
    tt = r_ref.shape[0]
    n = s_ref.shape[0]
    swap = (lambda x: pltpu.einshape("abl->bal", x)) if value_major else (lambda x: x)

    @pl.when(pl.program_id(1) == 0)
    def _():
        s_ref[...] = swap(s0_ref[...])

    sa0 = s_ref[0] * a_ref[0, 0:1, :]
    for k in range(1, n):
        sa0 = sa0 + s_ref[k] * a_ref[0, k:k + 1, :]

    def step(t, sa):
        tn = jnp.minimum(t + 1, tt - 1)
        row = lambda ref, k: ref[t, k:k + 1, :]
        v = v_ref[t]
        y = jnp.zeros_like(v)
        sa_next = jnp.zeros_like(v)
        for k in range(n):
            s = s_ref[k] * row(w_ref, k) + sa * row(b_ref, k) + v * row(k_ref, k)
            s_ref[k] = s
            y = y + s * row(r_ref, k)
            sa_next = sa_next + s * a_ref[tn, k:k + 1, :]
        y_ref[t] = y
        return sa_next

    lax.fori_loop(0, tt, step, sa0)

    @pl.when(pl.program_id(1) == pl.num_programs(1) - 1)
    def _():
        so_ref[...] = swap(s_ref[...])


def _wkv_scan(r, w, k, v, a, b, s0, *, value_major):
    t, n, lanes = r.shape
    tt = min(SCAN_TIME_BLOCK, t)
    seq = pl.BlockSpec((tt, n, LANES), lambda g, i: (i, 0, g))
    if value_major:
        nb = s0.shape[3] // LANES
        st = pl.BlockSpec((None, n, n, LANES), lambda g, i: (g // nb, 0, 0, g % nb))
    else:
        st = pl.BlockSpec((n, n, LANES), lambda g, i: (0, 0, g))
    return pl.pallas_call(
        functools.partial(_wkv_scan_kernel, value_major=value_major),
        grid=(lanes // LANES, t // tt),
        in_specs=[seq] * 6 + [st],
        out_specs=(seq, st),
        out_shape=(jax.ShapeDtypeStruct((t, n, lanes), F32), jax.ShapeDtypeStruct(s0.shape, F32)),
        scratch_shapes=[pltpu.VMEM((n, n, LANES), F32)],
        compiler_params=_cparams(("parallel", "arbitrary")), name="wkv_scan",
    )(r, w, k, v, a, b, s0)


def _rwkv_post_kernel(y_ref, bonus_ref, g_ref, h_ref, p_ref, lnw_ref, lnb_ref, bd_ref, wo_ref, png_ref,
                      pgw_ref, ppw_ref, fg_ref, o_ref, *, time_minor):
    bd = bd_ref[...]
    y = y_ref[...].T if time_minor else y_ref[...]
    inv_n = 1.0 / RW_N
    yc = y - _head_sum(y, bd) * inv_n
    var = _head_sum(yc * yc, bd) * inv_n
    yn = yc * lax.rsqrt(var + RW_EPS) * lnw_ref[...] + lnb_ref[...]
    mix = ((yn + bonus_ref[...]) * g_ref[...]).astype(BF16)
    h = h_ref[...] + _dot(mix, wo_ref[...])
    h = _ple(h, p_ref[...], png_ref[...], pgw_ref[...], ppw_ref[...])
    o_ref[...] = _rms(h, fg_ref[...])


def _rwkv_post(y, bonus, g, h, p3d, layer, lnw, lnb, bd, wo, png, pgw, ppw, fg, *, time_minor):
    n, d = h.shape
    tm = min(ROW_TILE, n)
    row = lambda a: pl.BlockSpec((tm, a.shape[1]), lambda i: (i, 0))
    if time_minor:
        nt = y.shape[2] // tm
        y_spec = pl.BlockSpec((None, d, tm), lambda i: (i // nt, 0, i % nt))
    else:
        y_spec = row(y)
    ws = [lnw, lnb, bd, wo, png, pgw, ppw, fg]
    return pl.pallas_call(
        functools.partial(_rwkv_post_kernel, time_minor=time_minor),
        grid=(n // tm,),
        in_specs=[y_spec, row(bonus), row(g), row(h), _layer_rows(p3d, layer, tm)] + [_full(w.shape) for w in ws],
        out_specs=row(h),
        out_shape=jax.ShapeDtypeStruct((n, d), F32),
        compiler_params=_cparams(("parallel",)), name="rwkv_post",
    )(y, bonus, g, h, p3d, *ws)


def _rope_tables(pos, rows):
    half = ROPE_DIM // 2
    inv = ROPE_THETA ** (-jnp.arange(half, dtype=F32) / half)
    ang = pos.astype(F32)[:, None] * inv[None, :]
    cos = jnp.cos(ang)
    sin = jnp.sin(ang)
    cos = jnp.tile(jnp.concatenate([cos, cos], axis=-1), (rows // pos.shape[0], MLA_HEADS))
    sin = jnp.tile(jnp.concatenate([-sin, sin], axis=-1), (rows // pos.shape[0], MLA_HEADS))
    return cos, sin


def _swap_halves(w):
    half = w.shape[-1] // 2
    return jnp.concatenate([w[..., half:], w[..., :half]], axis=-1)


def _to_lanes(x, batch, seq, batch_minor):
    heads = x.shape[1] // RW_N
    order = (1, 3, 2, 0) if batch_minor else (1, 3, 0, 2)
    return x.reshape(batch, seq, heads, RW_N).transpose(order).reshape(seq, RW_N, batch * heads)


def _from_lanes(y, batch, seq, batch_minor):
    heads = y.shape[2] // batch
    if batch_minor:
        y = y.reshape(seq, RW_N, heads, batch).transpose(3, 0, 2, 1)
    else:
        y = y.reshape(seq, RW_N, batch, heads).transpose(2, 0, 3, 1)
    return y.reshape(batch * seq, heads * RW_N)


def _trunk(x, p, pos, W, hg_s0, wkv_s0, shift_s0, paged):
    batch, seq, d = x.shape
    n = batch * seq
    x2d = x.reshape(n, d)
    row = lambda v: v.reshape(1, -1)

    cos_tab, sin_tab = _rope_tables(pos, max(seq, min(ROW_TILE, n)))
    zhg, qlat, qrope, ckv, kr, ckvb, krb, mg = _even_pre(
        x2d, cos_tab, sin_tab, row(W['mix_norm'][0]), W['hg_lb_logits'], W['w_in_hg'], W['w_in_mla'],
        row(W['mla_q_norm'][0]), row(W['mla_kv_norm'][0]), W['w_uq'], W['w_uk'], lb_rows=1)
    g_hg = row(W['hg_norm'][0])
    if paged is None:
        ohg, s_hg = _hgrn_prompt(zhg, g_hg, batch, seq)
        ctx = _attn_prompt(qlat, qrope, ckvb, krb, batch, seq)
    else:
        ohg, s_hg = _hgrn_sample(zhg, hg_s0, g_hg, seq)
        cache_ckv, cache_krope, page_table = paged
        stack = lambda q: q.reshape(MLA_HEADS, batch, seq, q.shape[-1]).transpose(1, 0, 2, 3).reshape(
            batch, MLA_HEADS * seq, q.shape[-1])
        pad = lambda a: jnp.pad(a.reshape(batch, seq, a.shape[-1]), ((0, 0), (0, 2 * SUBLANES - seq), (0, 0)))
        ctx = _attn_sample(page_table, stack(qlat), stack(qrope), pad(ckvb), pad(krb).transpose(0, 2, 1),
                           cache_ckv, cache_krope, seq)
        ctx = ctx.reshape(batch, MLA_HEADS, seq, KV_LORA).transpose(0, 2, 1, 3).reshape(n, MLA_HEADS * KV_LORA)
    p3d = p.reshape(p.shape[0], n, p.shape[-1])
    h1, hn1 = _even_post(
        x2d, ohg, ctx, mg, p3d, 0, W['w_uv_bd'], W['w_out'], row(W['ple_norm'][0]),
        W['ple_gate_b'][0], W['ple_proj_b'][0], row(W['mix_norm'][1]))

    hn3 = hn1.reshape(batch, seq, d)
    heads = d // RW_N
    batch_lanes = batch % LANES == 0
    time_minor = seq % LANES == 0 and (batch * heads) % LANES == 0 and not batch_lanes
    if time_minor:
        tm = min(ROW_TILE, n)
        tile_last = hn3[:, tm - 1::tm, :]
        prev = jnp.concatenate([shift_s0[:, None, :], tile_last[:, :-1]], axis=1).reshape(n // tm, 1, d)
    else:
        prev = jnp.concatenate([shift_s0[:, None, :], hn3[:, :-1]], axis=1).reshape(n, d)
    *scan_in, bonus, g = _rwkv_pre(
        hn1, prev, W['rw_mu'][0], W['w_r'], W['w_k'], W['w_v'], W['w_g'], row(W['rw_w0'][0]), W['w_w1'],
        W['w_w2'], row(W['rw_a0'][0]), W['w_a1'], W['w_a2'], row(W['rw_k_k'][0]), row(W['rw_k_a'][0]),
        row(W['rw_r_k'][0]), W['head_bd'], batch=batch, time_minor=time_minor)
    if time_minor:
        scan_in = _swap_outer([a.reshape(batch * heads, RW_N, seq) for a in scan_in])
    else:
        scan_in = [_to_lanes(a, batch, seq, batch_lanes) for a in scan_in]
    if batch_lanes:
        y, s_wkv = _wkv_scan(*scan_in, wkv_s0.transpose(1, 2, 3, 0), value_major=True)
        s_wkv = s_wkv.transpose(3, 0, 1, 2)
    else:
        y, s_wkv = _wkv_scan(*scan_in, wkv_s0.transpose(3, 2, 0, 1).reshape(RW_N, RW_N, -1), value_major=False)
        s_wkv = s_wkv.reshape(RW_N, RW_N, batch, heads).transpose(2, 3, 1, 0)
    y = _swap_outer([y])[0].reshape(batch, d, seq) if time_minor else _from_lanes(y, batch, seq, batch_lanes)
    out = _rwkv_post(
        y, bonus, g, h1, p3d, 1, row(W['rw_ln_w'][0]), row(W['rw_ln_b'][0]), W['head_bd'],
        W['w_o'], row(W['ple_norm'][1]), W['ple_gate_b'][1], W['ple_proj_b'][1], row(W['final_norm']),
        time_minor=time_minor)
    return (out.reshape(batch, seq, d), ckv.reshape(1, batch, seq, KV_LORA), kr.reshape(1, batch, seq, ROPE_DIM),
            s_hg[None], s_wkv[None], hn3[:, -1][None])


def kernel(x_prompt, x_sample, cache_ckv, cache_krope, state_hgrn, state_wkv, state_shift, page_table, p_prompt, p_sample, mix_norm, ev_w_in, hg_lb_logits, hg_norm, mla_q_norm, mla_w_uq, mla_kv_norm, mla_w_uk, mla_w_uv, ev_w_out, rw_mu, rw_w_rkvg, rw_w0, rw_w1, rw_w2, rw_a0, rw_a1, rw_a2, rw_k_k, rw_k_a, rw_r_k, rw_ln_w, rw_ln_b, rw_w_o, ple_norm, ple_gate, ple_proj, final_norm):
    bf = lambda a: a.astype(BF16)
    d = x_prompt.shape[-1]
    w_in = ev_w_in[0]
    o = np.cumsum([0, HG_WIDTH, HG_WIDTH, HG_HEADS * HG_DV, HG_HEADS * HG_DV, Q_LORA, KV_LORA, ROPE_DIM, MLA_WIDTH])
    cq, ckv_w, kr_w, mg_w = (w_in[:, o[4]:o[5]], w_in[:, o[5]:o[6]], w_in[:, o[6]:o[7]], w_in[:, o[7]:o[8]])
    uq = mla_w_uq[0].reshape(Q_LORA, MLA_HEADS, NOPE_DIM + ROPE_DIM)
    uq_rope = uq[:, :, NOPE_DIM:]
    uv_bd = jnp.zeros((MLA_HEADS, KV_LORA, MLA_HEADS, V_DIM), F32)
    uv_bd = uv_bd.at[jnp.arange(MLA_HEADS), :, jnp.arange(MLA_HEADS), :].set(mla_w_uv[0].transpose(0, 2, 1))
    hid = np.arange(MXU_DIM) // RW_N
    W = dict(
        mix_norm=mix_norm, hg_lb_logits=hg_lb_logits, hg_norm=hg_norm, mla_q_norm=mla_q_norm,
        mla_kv_norm=mla_kv_norm, ple_norm=ple_norm, final_norm=final_norm, rw_mu=rw_mu, rw_w0=rw_w0, rw_a0=rw_a0,
        rw_k_k=rw_k_k, rw_k_a=rw_k_a, rw_ln_w=rw_ln_w, rw_ln_b=rw_ln_b, rw_r_k=rw_r_k.reshape(rw_r_k.shape[0], -1),
        w_in_hg=bf(w_in[:, :o[4]]),
        w_in_mla=bf(jnp.concatenate([cq, ckv_w, mg_w, kr_w, _swap_halves(kr_w)], axis=-1)),
        w_uq=bf(jnp.concatenate([uq[:, :, :NOPE_DIM].reshape(Q_LORA, -1), uq_rope.reshape(Q_LORA, -1),
                                 _swap_halves(uq_rope).reshape(Q_LORA, -1)], axis=-1)),
        w_uk=bf(mla_w_uk[0]),
        w_uv_bd=bf(uv_bd.reshape(MLA_HEADS * KV_LORA, MLA_WIDTH)),
        w_out=bf(ev_w_out[0]),
        ple_gate_b=bf(ple_gate), ple_proj_b=bf(ple_proj),
        w_r=bf(rw_w_rkvg[0, 0]), w_k=bf(rw_w_rkvg[0, 1]), w_v=bf(rw_w_rkvg[0, 2]), w_g=bf(rw_w_rkvg[0, 3]),
        w_w1=bf(rw_w1[0]), w_w2=bf(rw_w2[0]), w_a1=bf(rw_a1[0]), w_a2=bf(rw_a2[0]), w_o=bf(rw_w_o[0]),
        head_bd=jnp.asarray(hid[:, None] == hid[None, :], BF16),
    )
    bp, tp, _ = x_prompt.shape
    bs, ts, _ = x_sample.shape
    past_len = page_table.shape[1] * cache_ckv.shape[2]
    heads = d // RW_N
    yp, ckv_p, kr_p, hg_p, wkv_p, sh_p = _trunk(
        x_prompt, p_prompt, jnp.arange(tp), W, None, jnp.zeros((bp, heads, RW_N, RW_N), F32),
        jnp.zeros((bp, d), F32), None)
    ys, ckv_s, kr_s, hg_s, wkv_s, sh_s = _trunk(
        x_sample, p_sample, past_len + jnp.arange(ts), W, state_hgrn[0], state_wkv[0], state_shift[0],
        (cache_ckv.reshape(cache_ckv.shape[1:]), jnp.swapaxes(cache_krope.reshape(cache_krope.shape[1:]), 1, 2),
         page_table))
    return (yp, ys, ckv_p, kr_p, ckv_s, kr_s, hg_p, hg_s, wkv_p, wkv_s, sh_p, sh_s)
```

```python
import functools

import jax
import jax.numpy as jnp
import numpy as np
from jax import lax
from jax.experimental import pallas as pl
from jax.experimental.pallas import tpu as pltpu

F32 = jnp.float32
BF16 = jnp.bfloat16

NORM_EPS = 1e-6
HG_HEADS = 4
HG_DK = 128
HG_DV = 128
HG_WIDTH = HG_HEADS * HG_DK
MLA_HEADS = 8
Q_LORA = 384
KV_LORA = 256
NOPE_DIM = 64
ROPE_DIM = 32
V_DIM = 64
MLA_WIDTH = MLA_HEADS * V_DIM
MLA_SCALE = (NOPE_DIM + ROPE_DIM) ** -0.5
ROPE_THETA = 10000.0
RW_N = 64
RW_EPS = 64e-5

LANES = 128
SUBLANES = 8
MXU_DIM = 256
VMEM_LIMIT_BYTES = 56 * 1024 * 1024

ROW_TILE = 256
HG_CHUNK = 64
HG_BLOCK = SUBLANES
HG_TIME_BLOCK = 1024
HG_SEQS_PER_STEP = 2
ATT_BLOCK = 256
PAGES_PER_STEP = 32
PAGES_PER_BLOCK = 8
PAGE_RING = 3
SCAN_TIME_BLOCK = 32


def _cparams(sem):
    return pltpu.CompilerParams(dimension_semantics=sem, vmem_limit_bytes=VMEM_LIMIT_BYTES)


def _rms(x, g):
    return x * lax.rsqrt(jnp.mean(x * x, axis=-1, keepdims=True) + NORM_EPS) * g


def _sigmoid(x):
    return 1.0 / (1.0 + jnp.exp(-x))


def _silu(x):
    return x * _sigmoid(x)


def _dot(a, b):
    return jnp.dot(a, b, preferred_element_type=F32)


def _dot_nt(a, b):
    return lax.dot_general(a, b, (((1,), (1,)), ((), ())), preferred_element_type=F32)


def _head_sum(x, bd):
    hi = x.astype(BF16)
    lo = (x - hi.astype(F32)).astype(BF16)
    outs = []
    for c in range(x.shape[-1] // MXU_DIM):
        sl = slice(c * MXU_DIM, (c + 1) * MXU_DIM)
        outs.append(_dot(hi[:, sl], bd) + _dot(lo[:, sl], bd))
    return jnp.concatenate(outs, axis=-1)


def _full(shape):
    nd = len(shape)
    return pl.BlockSpec(shape, lambda *_: (0,) * nd)


def _even_pre_kernel(x_ref, cos_ref, sin_ref, g_ref, lbl_ref, w1_ref, w2_ref, qg_ref, kvg_ref, wuq_ref,
                     wuk_ref, zhg_ref, qlat_ref, qrope_ref, ckv_ref, kr_ref, ckvb_ref, krb_ref, mg_ref,
                     *, lb_rows):
    hn = _rms(x_ref[...], g_ref[...]).astype(BF16)
    z1 = _dot(hn, w1_ref[...])
    z2 = _dot(hn, w2_ref[...])
    lg = lbl_ref[...]
    e = jnp.exp(lg - jnp.max(lg, axis=0, keepdims=True))
    p = e / jnp.sum(e, axis=0, keepdims=True)
    lb = jnp.sum(p[:lb_rows], axis=0, keepdims=True)
    W = HG_WIDTH
    f = lb + (1.0 - lb) * _sigmoid(z1[:, W:2 * W])
    zhg_ref[:, 0:W] = _silu(z1[:, 0:W])
    zhg_ref[:, W:2 * W] = jnp.log(f)
    zhg_ref[:, 2 * W:3 * W] = z1[:, 2 * W:3 * W]
    zhg_ref[:, 3 * W:4 * W] = _silu(z1[:, 3 * W:4 * W])
    o_kv = Q_LORA
    o_mg = o_kv + KV_LORA
    o_kr = o_mg + MLA_WIDTH
    o_krs = o_kr + ROPE_DIM
    cqn = _rms(z2[:, 0:o_kv], qg_ref[...]).astype(BF16)
    qf = _dot(cqn, wuq_ref[...])
    cos = cos_ref[...]
    sin = sin_ref[...]
    n_nope = MLA_HEADS * NOPE_DIM
    n_rope = MLA_HEADS * ROPE_DIM
    qr = (qf[:, n_nope:n_nope + n_rope] * cos + qf[:, n_nope + n_rope:n_nope + 2 * n_rope] * sin) * MLA_SCALE
    for h in range(MLA_HEADS):
        qrope_ref[h] = qr[:, h * ROPE_DIM:(h + 1) * ROPE_DIM].astype(BF16)
        qn = qf[:, h * NOPE_DIM:(h + 1) * NOPE_DIM].astype(BF16)
        qlat_ref[h] = (_dot(qn, wuk_ref[h]) * MLA_SCALE).astype(BF16)
    ckv = _rms(z2[:, o_kv:o_mg], kvg_ref[...])
    ckv_ref[...] = ckv
    ckvb_ref[...] = ckv.astype(BF16)
    kr = z2[:, o_kr:o_krs] * cos[:, :ROPE_DIM] + z2[:, o_krs:o_krs + ROPE_DIM] * sin[:, :ROPE_DIM]
    kr_ref[...] = kr
    krb_ref[...] = kr.astype(BF16)
    mg_ref[...] = _silu(z2[:, o_mg:o_kr])


def _even_pre(x2d, cos_tab, sin_tab, g, lb_logits, w1, w2, qg, kvg, wuq, wuk, *, lb_rows):
    n, d = x2d.shape
    tm = min(ROW_TILE, n)
    n_tab = cos_tab.shape[0] // tm
    row = lambda i: (i, 0)
    tab = lambda i: (i % n_tab, 0)
    out_shape = (
        jax.ShapeDtypeStruct((n, 4 * HG_WIDTH), F32),
        jax.ShapeDtypeStruct((MLA_HEADS, n, KV_LORA), BF16),
        jax.ShapeDtypeStruct((MLA_HEADS, n, ROPE_DIM), BF16),
        jax.ShapeDtypeStruct((n, KV_LORA), F32),
        jax.ShapeDtypeStruct((n, ROPE_DIM), F32),
        jax.ShapeDtypeStruct((n, KV_LORA), BF16),
        jax.ShapeDtypeStruct((n, ROPE_DIM), BF16),
        jax.ShapeDtypeStruct((n, MLA_WIDTH), F32),
    )
    out_specs = (
        pl.BlockSpec((tm, 4 * HG_WIDTH), row),
        pl.BlockSpec((MLA_HEADS, tm, KV_LORA), lambda i: (0, i, 0)),
        pl.BlockSpec((MLA_HEADS, tm, ROPE_DIM), lambda i: (0, i, 0)),
        pl.BlockSpec((tm, KV_LORA), row),
        pl.BlockSpec((tm, ROPE_DIM), row),
        pl.BlockSpec((tm, KV_LORA), row),
        pl.BlockSpec((tm, ROPE_DIM), row),
        pl.BlockSpec((tm, MLA_WIDTH), row),
    )
    in_specs = [
        pl.BlockSpec((tm, d), row),
        pl.BlockSpec((tm, cos_tab.shape[1]), tab),
        pl.BlockSpec((tm, sin_tab.shape[1]), tab),
        _full(g.shape), _full(lb_logits.shape), _full(w1.shape), _full(w2.shape), _full(qg.shape),
        _full(kvg.shape), _full(wuq.shape), _full(wuk.shape),
    ]
    return pl.pallas_call(
        functools.partial(_even_pre_kernel, lb_rows=lb_rows),
        grid=(n // tm,), in_specs=in_specs, out_specs=out_specs, out_shape=out_shape,
        compiler_params=_cparams(("parallel",)), name="even_pre",
    )(x2d, cos_tab, sin_tab, g, lb_logits, w1, w2, qg, kvg, wuq, wuk)


def _tril_ones(n, block):
    r = lax.broadcasted_iota(jnp.int32, (n, n), 0)
    c = lax.broadcasted_iota(jnp.int32, (n, n), 1)
    return ((r >= c) & ((r // block) == (c // block))).astype(F32)


def _hgrn_exact_blocks(q, k, v, b, block):
    rows = q.shape[0]
    rid = lax.broadcasted_iota(jnp.int32, (rows, 1), 0) % block
    o = jnp.sum(q * k, axis=-1, keepdims=True) * v
    for d in range(1, block):
        kd = pltpu.roll(k, d, 0)
        bd = pltpu.roll(b, d, 0)
        vd = pltpu.roll(v, d, 0)
        w = jnp.sum(q * kd * jnp.exp(jnp.minimum(b - bd, 0.0)), axis=-1, keepdims=True)
        o = o + jnp.where(rid >= d, w, 0.0) * vd
    return o


def _hgrn_finish(o, g, gate):
    return (o * lax.rsqrt(jnp.mean(o * o, axis=-1, keepdims=True) + NORM_EPS) * g * gate).astype(BF16)


def _hgrn_prompt_kernel(q_ref, lf_ref, v_ref, gt_ref, g_ref, o_ref, s_ref, st_ref):
    C = HG_CHUNK
    nseq, tb, _ = q_ref.shape
    tril = _tril_ones(C, C)
    rid = lax.broadcasted_iota(jnp.int32, (C, 1), 0)
    rr = lax.broadcasted_iota(jnp.int32, (C, C), 0)
    cc = lax.broadcasted_iota(jnp.int32, (C, C), 1)
    g = g_ref[...]

    @pl.when(pl.program_id(1) == 0)
    def _():
        st_ref[...] = jnp.zeros(st_ref.shape, F32)

    def decay(x):
        x['k'] = 1.0 - jnp.exp(x['lf'])
        x['b'] = jnp.dot(tril, x['lf'], precision=lax.Precision.HIGHEST, preferred_element_type=F32)

    def scores(x):
        q, k, b = x['q'], x['k'], x['b']
        x['o'] = _dot_nt((q * jnp.exp(b)).astype(BF16), x['st'].astype(BF16))
        x['att'] = []
        m = HG_BLOCK
        while 2 * m <= C:
            nb = C // (2 * m)
            b3 = b.reshape(nb, 2 * m, HG_DK)
            ref = jnp.broadcast_to(b3[:, m - 1:m, :], (nb, 2 * m, HG_DK)).reshape(C, HG_DK)
            upper = (rid % (2 * m)) >= m
            qt = q * jnp.where(upper, jnp.exp(jnp.minimum(b - ref, 0.0)), 0.0)
            kt = k * jnp.where(upper, 0.0, jnp.exp(jnp.minimum(ref - b, 0.0)))
            x['att'].append((m, _dot_nt(qt.astype(BF16), kt.astype(BF16))))
            m *= 2
        bend = b[C - 1:C, :]
        x['st'] = x['st'] * jnp.exp(bend) + _dot(x['v'].T.astype(BF16), (k * jnp.exp(bend - b)).astype(BF16))
        x['o'] = x['o'] + _hgrn_exact_blocks(q, k, x['v'], b, HG_BLOCK)

    def values(x):
        att = jnp.zeros((C, C), F32)
        for m, a in x['att']:
            if 2 * m < C:
                a = jnp.where((rr // (2 * m)) == (cc // (2 * m)), a, 0.0)
            att = att + a
        x['pv'] = _dot(att.astype(BF16), x['v'].astype(BF16))

    def chunk(c, carry):
        sl = pl.ds(pl.multiple_of(c * C, C), C)
        xs = [dict(s=s, h=h, hs=slice(h * HG_DK, (h + 1) * HG_DK)) for s in range(nseq) for h in range(HG_HEADS)]
        for x in xs:
            s, h, hs = x['s'], x['h'], x['hs']
            x.update(st=st_ref[s, h], q=q_ref[s, sl, hs], lf=lf_ref[s, sl, hs], v=v_ref[s, sl, hs],
                     gate=gt_ref[s, sl, hs])
        for stage in (decay, scores, values):
            for x in xs:
                stage(x)
        for x in xs:
            o_ref[x['s'], sl, x['hs']] = _hgrn_finish(x['o'] + x['pv'], g, x['gate'])
            st_ref[x['s'], x['h']] = x['st']
        return carry

    lax.fori_loop(0, tb // C, chunk, 0)

    @pl.when(pl.program_id(1) == pl.num_programs(1) - 1)
    def _():
        for s in range(nseq):
            for h in range(HG_HEADS):
                s_ref[s, h] = st_ref[s, h].T


def _hgrn_prompt(zhg, g, batch, seq):
    nseq = HG_SEQS_PER_STEP if batch % HG_SEQS_PER_STEP == 0 else 1
    tb = min(HG_TIME_BLOCK // nseq, seq)
    z3 = zhg.reshape(batch, seq, 4 * HG_WIDTH)
    blk = lambda off: pl.BlockSpec((nseq, tb, HG_WIDTH), lambda b, i: (b, i, off))
    ohg, s_hg = pl.pallas_call(
        _hgrn_prompt_kernel,
        grid=(batch // nseq, seq // tb),
        in_specs=[blk(0), blk(1), blk(2), blk(3), _full(g.shape)],
        out_specs=(pl.BlockSpec((nseq, tb, HG_HEADS * HG_DV), lambda b, i: (b, i, 0)),
                   pl.BlockSpec((nseq, HG_HEADS, HG_DK, HG_DV), lambda b, i: (b, 0, 0, 0))),
        out_shape=(jax.ShapeDtypeStruct((batch, seq, HG_HEADS * HG_DV), BF16),
                   jax.ShapeDtypeStruct((batch, HG_HEADS, HG_DK, HG_DV), F32)),
        scratch_shapes=[pltpu.VMEM((nseq, HG_HEADS, HG_DV, HG_DK), F32)],
        compiler_params=_cparams(("parallel", "arbitrary")), name="hgrn_prompt",
    )(z3, z3, z3, z3, g)
    return ohg.reshape(batch * seq, HG_HEADS * HG_DV), s_hg


def _hgrn_sample_kernel(z_ref, s0_ref, g_ref, o_ref, s_ref, *, seq):
    rows = z_ref.shape[0]
    nb = rows // seq
    tril = _tril_ones(rows, seq)
    rb = lax.broadcasted_iota(jnp.int32, (rows, 1), 0) // seq
    cb = lax.broadcasted_iota(jnp.int32, (1, rows), 1) // seq
    g = g_ref[...]
    W = HG_WIDTH
    for h in range(HG_HEADS):
        hs = slice(h * HG_DK, (h + 1) * HG_DK)
        q = z_ref[:, hs]
        lf = z_ref[:, W + h * HG_DK:W + (h + 1) * HG_DK]
        v = z_ref[:, 2 * W + h * HG_DV:2 * W + (h + 1) * HG_DV]
        gate = z_ref[:, 3 * W + h * HG_DV:3 * W + (h + 1) * HG_DV]
        k = 1.0 - jnp.exp(lf)
        b = jnp.dot(tril, lf, precision=lax.Precision.HIGHEST, preferred_element_type=F32)
        o = _hgrn_exact_blocks(q, k, v, b, seq)
        qe = (q * jnp.exp(b)).astype(BF16)
        bt = b.T
        vb = v.astype(BF16)
        for i in range(nb):
            s0 = s0_ref[i, h]
            o = o + jnp.where(rb == i, _dot(qe, s0.astype(BF16)), 0.0)
            last = i * seq + seq - 1
            bend_row = b[last:last + 1, :]
            kt = jnp.where(rb == i, k * jnp.exp(bend_row - b), 0.0)
            ktt = kt.T.astype(BF16)
            s_ref[i, h] = s0 * jnp.exp(bt[:, last:last + 1]) + _dot(ktt, vb)
        o_ref[:, hs] = _hgrn_finish(o, g, gate)


def _hgrn_sample(zhg, s0, g, seq):
    n = zhg.shape[0]
    nb = 8
    rows = nb * seq
    return pl.pallas_call(
        functools.partial(_hgrn_sample_kernel, seq=seq),
        grid=(n // rows,),
        in_specs=[pl.BlockSpec((rows, 4 * HG_WIDTH), lambda i: (i, 0)),
                  pl.BlockSpec((nb, HG_HEADS, HG_DK, HG_DV), lambda i: (i, 0, 0, 0)),
                  _full(g.shape)],
        out_specs=(pl.BlockSpec((rows, HG_HEADS * HG_DV), lambda i: (i, 0)),
                   pl.BlockSpec((nb, HG_HEADS, HG_DK, HG_DV), lambda i: (i, 0, 0, 0))),
        out_shape=(jax.ShapeDtypeStruct((n, HG_HEADS * HG_DV), BF16),
                   jax.ShapeDtypeStruct(s0.shape, F32)),
        compiler_params=_cparams(("parallel",)), name="hgrn_sample",
    )(zhg, s0, g)


def _lane_tile(x, width):
    if width <= LANES:
        return x[:, :width]
    return jnp.concatenate([x] * (width // LANES), axis=1)


def _softmax_init(m_ref, l_ref, acc_ref):
    m_ref[...] = jnp.full(m_ref.shape, -jnp.inf, F32)
    l_ref[...] = jnp.zeros(l_ref.shape, F32)
    acc_ref[...] = jnp.zeros(acc_ref.shape, F32)


def _online_step(s, kv, m_old, l_old, acc_old):
    m_new = jnp.maximum(m_old, jnp.max(s, axis=-1, keepdims=True))
    alpha = jnp.exp(m_old - m_new)
    p = jnp.exp(s - _lane_tile(m_new, s.shape[1]))
    l_new = alpha * l_old + jnp.sum(p, axis=-1, keepdims=True)
    acc_new = _lane_tile(alpha, acc_old.shape[1]) * acc_old + _dot(p.astype(BF16), kv)
    return m_new, l_new, acc_new


def _online_update(s, kv, rows, m_ref, l_ref, acc_ref):
    m_ref[rows, :], l_ref[rows, :], acc_ref[rows, :] = _online_step(
        s, kv, m_ref[rows, :], l_ref[rows, :], acc_ref[rows, :])


def _attn_prompt_kernel(ql_ref, qr_ref, kv_ref, kr_ref, o_ref, m_ref, l_ref, acc_ref):
    tq = ql_ref.shape[1]
    i = pl.program_id(1)
    _softmax_init(m_ref, l_ref, acc_ref)
    n_split = 2
    hs = MLA_HEADS // n_split
    mh = hs * tq
    qpos = lax.broadcasted_iota(jnp.int32, (mh, tq), 0) % tq
    kpos = lax.broadcasted_iota(jnp.int32, (mh, tq), 1)

    def block(j, masked):
        sl = pl.ds(pl.multiple_of(j * tq, tq), tq)
        kv = kv_ref[sl, :]
        kr = kr_ref[sl, :]
        scores = []
        for r in range(n_split):
            ql = ql_ref[r * hs:(r + 1) * hs].reshape(mh, KV_LORA)
            qr = qr_ref[r * hs:(r + 1) * hs].reshape(mh, ROPE_DIM)
            s = _dot_nt(ql, kv) + _dot_nt(qr, kr)
            scores.append(jnp.where(kpos <= qpos, s, -jnp.inf) if masked else s)
        for r, s in enumerate(scores):
            _online_update(s, kv, slice(r * mh, (r + 1) * mh), m_ref, l_ref, acc_ref)

    def body(j, carry):
        block(j, False)
        return carry

    lax.fori_loop(0, i, body, 0)
    block(i, True)
    for h in range(MLA_HEADS):
        rows = slice(h * tq, (h + 1) * tq)
        out = acc_ref[rows, :] / _lane_tile(l_ref[rows, :], KV_LORA)
        o_ref[:, h * KV_LORA:(h + 1) * KV_LORA] = out.astype(BF16)


def _attn_prompt(qlat, qrope, ckvb, krb, batch, seq):
    tq = ATT_BLOCK
    nq = seq // tq
    rows = MLA_HEADS * tq
    return pl.pallas_call(
        _attn_prompt_kernel,
        grid=(batch, nq),
        in_specs=[pl.BlockSpec((MLA_HEADS, tq, KV_LORA), lambda b, i: (0, b * nq + i, 0)),
                  pl.BlockSpec((MLA_HEADS, tq, ROPE_DIM), lambda b, i: (0, b * nq + i, 0)),
                  pl.BlockSpec((seq, KV_LORA), lambda b, i: (b, 0)),
                  pl.BlockSpec((seq, ROPE_DIM), lambda b, i: (b, 0))],
        out_specs=pl.BlockSpec((tq, MLA_HEADS * KV_LORA), lambda b, i: (b * nq + i, 0)),
        out_shape=jax.ShapeDtypeStruct((batch * seq, MLA_HEADS * KV_LORA), BF16),
        scratch_shapes=[pltpu.VMEM((rows, LANES), F32), pltpu.VMEM((rows, LANES), F32),
                        pltpu.VMEM((rows, KV_LORA), F32)],
        compiler_params=_cparams(("parallel", "arbitrary")), name="attn_prompt",
    )(qlat, qrope, ckvb, krb)


class _PageRing:
    def __init__(self, pt_ref, ckv_hbm, kr_hbm, kvbuf, krbuf, sems, units_per_seq):
        self.pt_ref, self.ckv_hbm, self.kr_hbm = pt_ref, ckv_hbm, kr_hbm
        self.kvbuf, self.krbuf, self.sems = kvbuf, krbuf, sems
        self.units_per_seq = units_per_seq
        self.pages = krbuf.shape[1]
        self.page = kvbuf.shape[1] // self.pages

    def _copies(self, u):
        slot = u % PAGE_RING
        seq_id = u // self.units_per_seq
        first = (u % self.units_per_seq) * self.pages
        out = []
        for j in range(self.pages):
            pg = self.pt_ref[seq_id, first + j]
            out.append(pltpu.make_async_copy(
                self.ckv_hbm.at[pg], self.kvbuf.at[slot, pl.ds(j * self.page, self.page), :], self.sems.at[0, slot]))
            out.append(pltpu.make_async_copy(self.kr_hbm.at[pg], self.krbuf.at[slot, j], self.sems.at[1, slot]))
        return out

    def prime(self, total):
        for u in range(PAGE_RING - 1):
            @pl.when(u < total)
            def _():
                for c in self._copies(u):
                    c.start()

    def wait(self, u):
        for c in self._copies(u):
            c.wait()

    def refill(self, u, total):
        @pl.when(u + PAGE_RING - 1 < total)
        def _():
            for c in self._copies(u + PAGE_RING - 1):
                c.start()

    def attend(self, u, ql, qr, state):
        slot = u % PAGE_RING
        sub = min(PAGES_PER_BLOCK, self.pages)
        blocks = []
        for c in range(0, self.pages, sub):
            kv = self.kvbuf[slot, c * self.page:(c + sub) * self.page, :].astype(BF16)
            kr = jnp.concatenate([self.krbuf[slot, c + j].astype(BF16) for j in range(sub)], axis=1)
            blocks.append((_dot_nt(ql, kv) + _dot(qr, kr), kv))
        for s, kv in blocks:
            state = _online_step(s, kv, *state)
        return state


def _attend_new_rows(ql, qr, kvn, krn, seq, state):
    sn = _dot_nt(ql, kvn) + _dot(qr, krn)
    qpos = lax.broadcasted_iota(jnp.int32, sn.shape, 0) % seq
    kpos = lax.broadcasted_iota(jnp.int32, sn.shape, 1)
    _, l, acc = _online_step(jnp.where(kpos <= qpos, sn, -jnp.inf), kvn, *state)
    return (acc / _lane_tile(l, KV_LORA)).astype(BF16)


def _attn_sample_kernel(pt_ref, ql_ref, qr_ref, kvn_ref, krn_ref, ckv_hbm, kr_hbm, o_ref,
                        kvbuf, krbuf, sems, m_ref, l_ref, acc_ref, *, seq):
    n_groups = pl.num_programs(1)
    g = pl.program_id(1)
    unit = pl.program_id(0) * n_groups + g
    total = pl.num_programs(0) * n_groups
    ring = _PageRing(pt_ref, ckv_hbm, kr_hbm, kvbuf, krbuf, sems, n_groups)

    @pl.when(unit == 0)
    def _():
        ring.prime(total)

    @pl.when(g == 0)
    def _():
        _softmax_init(m_ref, l_ref, acc_ref)

    ring.wait(unit)
    ql = ql_ref[...]
    qr = qr_ref[...]
    state = ring.attend(unit, ql, qr, (m_ref[...], l_ref[...], acc_ref[...]))
    m_ref[...], l_ref[...], acc_ref[...] = state
    ring.refill(unit, total)

    @pl.when(g == n_groups - 1)
    def _():
        o_ref[...] = _attend_new_rows(ql, qr, kvn_ref[...], krn_ref[...], seq, state)


def _page_ring_scratch(pages, page, rows):
    return [pltpu.VMEM((PAGE_RING, pages * page, KV_LORA), F32),
            pltpu.VMEM((PAGE_RING, pages, ROPE_DIM, page), F32),
            pltpu.SemaphoreType.DMA((2, PAGE_RING)),
            pltpu.VMEM((rows, LANES), F32), pltpu.VMEM((rows, LANES), F32), pltpu.VMEM((rows, KV_LORA), F32)]


def _attn_sample(page_table, qlat, qrope, kv_new, kr_new_t, cache_ckv, cache_krope_t, seq):
    batch, n_pages = page_table.shape
    G = min(PAGES_PER_STEP, n_pages)
    page = cache_ckv.shape[1]
    rows = qlat.shape[1]
    npad = kv_new.shape[1]

    per_b = lambda r, width: pl.BlockSpec((None, r, width), lambda b, g, pt: (b, 0, 0))
    in_hbm = pl.BlockSpec(memory_space=pl.ANY)
    grid_spec = pltpu.PrefetchScalarGridSpec(
        num_scalar_prefetch=1,
        grid=(batch, n_pages // G),
        in_specs=[per_b(rows, KV_LORA), per_b(rows, ROPE_DIM), per_b(npad, KV_LORA), per_b(ROPE_DIM, npad),
                  in_hbm, in_hbm],
        out_specs=per_b(rows, KV_LORA),
        scratch_shapes=_page_ring_scratch(G, page, rows),
    )
    return pl.pallas_call(
        functools.partial(_attn_sample_kernel, seq=seq),
        grid_spec=grid_spec,
        out_shape=jax.ShapeDtypeStruct((batch, rows, KV_LORA), BF16),
        compiler_params=_cparams(("arbitrary", "arbitrary")), name="attn_sample",
    )(page_table, qlat, qrope, kv_new, kr_new_t, cache_ckv, cache_krope_t)


def _ple(h, p, png, pgw, ppw):
    gate = _sigmoid(_dot(_rms(h, png).astype(BF16), pgw))
    return h + gate * _dot(p.astype(BF16), ppw)


def _even_post_kernel(x_ref, ohg_ref, ctx_ref, mg_ref, p_ref, wuv_ref, wout_ref, png_ref, pgw_ref, ppw_ref,
                      ng_ref, h_ref, hn_ref):
    o_mla = (_dot(ctx_ref[...], wuv_ref[...]) * mg_ref[...]).astype(BF16)
    n_hg = HG_HEADS * HG_DV
    h = x_ref[...] + _dot(ohg_ref[...], wout_ref[0:n_hg, :]) + _dot(o_mla, wout_ref[n_hg:, :])
    h = _ple(h, p_ref[...], png_ref[...], pgw_ref[...], ppw_ref[...])
    h_ref[...] = h
    hn_ref[...] = _rms(h, ng_ref[...])


def _layer_rows(p3d, layer, tm):
    return pl.BlockSpec((None, tm, p3d.shape[2]), lambda i: (layer, i, 0))


def _even_post(x2d, ohg, ctx, mg, p3d, layer, wuv, wout, png, pgw, ppw, ng):
    n, d = x2d.shape
    tm = min(ROW_TILE, n)
    row = lambda a: pl.BlockSpec((tm, a.shape[1]), lambda i: (i, 0))
    return pl.pallas_call(
        _even_post_kernel,
        grid=(n // tm,),
        in_specs=[row(x2d), row(ohg), row(ctx), row(mg), _layer_rows(p3d, layer, tm), _full(wuv.shape),
                  _full(wout.shape), _full(png.shape), _full(pgw.shape), _full(ppw.shape), _full(ng.shape)],
        out_specs=(row(x2d), row(x2d)),
        out_shape=(jax.ShapeDtypeStruct((n, d), F32), jax.ShapeDtypeStruct((n, d), F32)),
        compiler_params=_cparams(("parallel",)), name="even_post",
    )(x2d, ohg, ctx, mg, p3d, wuv, wout, png, pgw, ppw, ng)


def _rwkv_pre_kernel(hn_ref, pv_ref, mu_ref, wr_ref, wk_ref, wv_ref, wg_ref, w0_ref, w1_ref, w2_ref, a0_ref,
                     a1_ref, a2_ref, kk_ref, ka_ref, rk_ref, bd_ref,
                     r_ref, w_ref, k_ref, v_ref, na_ref, b_ref, bonus_ref, g_ref, *, time_minor):
    def put(ref, x):
        if time_minor:
            xt = x.T
            for j in range(ref.shape[0]):
                ref[j] = xt[:, j * LANES:(j + 1) * LANES]
        else:
            ref[...] = x

    hn = hn_ref[...]
    if time_minor:
        first = lax.broadcasted_iota(jnp.int32, (hn.shape[0], 1), 0) == 0
        prev = jnp.where(first, pv_ref[...], pltpu.roll(hn, 1, 0))
    else:
        prev = pv_ref[...]
    dlt = prev - hn
    mix = lambda j: (hn + dlt * mu_ref[j:j + 1, :]).astype(BF16)
    w_mid = _dot(mix(4), w1_ref[...])
    a_mid = _dot(mix(5), a1_ref[...])
    r = _dot(mix(0), wr_ref[...])
    k = _dot(mix(1), wk_ref[...])
    v = _dot(mix(2), wv_ref[...])
    g = _dot(mix(3), wg_ref[...])
    wl = w0_ref[...] + _dot(jnp.tanh(w_mid).astype(BF16), w2_ref[...])
    w_log = -(jnp.maximum(-wl, 0.0) + jnp.log(1.0 + jnp.exp(-jnp.abs(wl)))) - 0.5
    a = _sigmoid(a0_ref[...] + _dot(a_mid.astype(BF16), a2_ref[...]))
    kk = k * kk_ref[...]
    kk = kk / jnp.maximum(jnp.sqrt(_head_sum(kk * kk, bd_ref[...])), 1e-12)
    k_mod = k * (1.0 + (a - 1.0) * ka_ref[...])
    put(r_ref, r)
    put(w_ref, jnp.exp(-jnp.exp(w_log)))
    put(k_ref, k_mod)
    put(v_ref, v)
    put(na_ref, -kk)
    put(b_ref, kk * a)
    bonus_ref[...] = _head_sum(r * k_mod * rk_ref[...], bd_ref[...]) * v
    g_ref[...] = _silu(g)


def _rwkv_pre(hn, prev, mu, wr, wk, wv, wg, w0, w1, w2, a0, a1, a2, kk, ka, rk, bd, *, batch, time_minor):
    n, d = hn.shape
    tm = min(ROW_TILE, n)
    seq = n // batch
    nt = seq // tm if time_minor else 1
    row = pl.BlockSpec((tm, d), lambda i: (i, 0))
    ws = [mu, wr, wk, wv, wg, w0, w1, w2, a0, a1, a2, kk, ka, rk, bd]
    if time_minor:
        scan_spec = pl.BlockSpec((None, tm // LANES, d, LANES), lambda i: (i // nt, i % nt, 0, 0))
        scan_shape = jax.ShapeDtypeStruct((batch, seq // LANES, d, LANES), F32)
        prev_spec = pl.BlockSpec((None, 1, d), lambda i: (i, 0, 0))
    else:
        scan_spec, scan_shape, prev_spec = row, jax.ShapeDtypeStruct((n, d), F32), row
    return pl.pallas_call(
        functools.partial(_rwkv_pre_kernel, time_minor=time_minor),
        grid=(n // tm,),
        in_specs=[row, prev_spec] + [_full(w.shape) for w in ws],
        out_specs=(scan_spec,) * 6 + (row, row),
        out_shape=(scan_shape,) * 6 + (jax.ShapeDtypeStruct((n, d), F32),) * 2,
        compiler_params=_cparams(("parallel",)), name="rwkv_pre",
    )(hn, prev, *ws)


def _retile_kernel(*refs):
    n = len(refs) // 2
    for src, dst in zip(refs[:n], refs[n:]):
        kb = src.shape[-2]
        src2 = src.reshape(LANES * kb, LANES)
        dst2 = dst.reshape(LANES * kb, LANES)
        for j in range(kb):
            dst2[pl.ds(j, LANES, stride=kb), :] = src2[pl.ds(j, LANES, stride=kb), :].T


def _retile(arrays, to_scan, batch=None):
    kb = SUBLANES
    n = len(arrays)
    if to_scan:
        nb, tb, nh, k, _ = arrays[0].shape
        out_shape = jax.ShapeDtypeStruct((tb * LANES, k, LANES), F32)
    else:
        t, k, _ = arrays[0].shape
        nb, tb, nh = batch, t // LANES, LANES // batch
        out_shape = jax.ShapeDtypeStruct((nb, tb, nh, k, LANES), F32)
    tiled = pl.BlockSpec((nb, None, nh, kb, LANES), lambda j, i: (0, i, 0, j, 0))
    scan = pl.BlockSpec((LANES, kb, LANES), lambda j, i: (i, j, 0))
    return pl.pallas_call(
        _retile_kernel,
        grid=(k // kb, tb),
        in_specs=[tiled if to_scan else scan] * n,
        out_specs=(scan if to_scan else tiled,) * n,
        out_shape=(out_shape,) * n,
        compiler_params=_cparams(("parallel", "parallel")), name="retile",
    )(*arrays)


def _wkv_first_sa(s_ref, a_ref):
    sa = s_ref[0] * a_ref[0, 0:1, :]
    for k in range(1, s_ref.shape[0]):
        sa = sa + s_ref[k] * a_ref[0, k:k + 1, :]
    return sa


def _wkv_step(t, sa, r_ref, w_ref, k_ref, v_ref, a_ref, b_ref, y_ref, s_ref):
    tt = r_ref.shape[0]
    tn = jnp.minimum(t + 1, tt - 1)
    row = lambda ref, k: ref[t, k:k + 1, :]
    v = v_ref[t]
    y = jnp.zeros_like(v)
    sa_next = jnp.zeros_like(v)
    for k in range(s_ref.shape[0]):
        s = s_ref[k] * row(w_ref, k) + sa * row(b_ref, k) + v * row(k_ref, k)
        s_ref[k] = s
        y = y + s * row(r_ref, k)
        sa_next = sa_next + s * a_ref[tn, k:k + 1, :]
    y_ref[t] = y
    return sa_next


def _wkv_scan_kernel(r_ref, w_ref, k_ref, v_ref, a_ref, b_ref, s0_ref, y_ref, so_ref, s_ref, *, value_major):
    n = s_ref.shape[0]

    @pl.when(pl.program_id(1) == 0)
    def _():
        if value_major:
            for k in range(n):
                s_ref[k] = s0_ref[:, k, :]
        else:
            s_ref[...] = s0_ref[...]

    step = functools.partial(_wkv_step, r_ref=r_ref, w_ref=w_ref, k_ref=k_ref, v_ref=v_ref, a_ref=a_ref,
                             b_ref=b_ref, y_ref=y_ref, s_ref=s_ref)
    lax.fori_loop(0, r_ref.shape[0], step, _wkv_first_sa(s_ref, a_ref))

    @pl.when(pl.program_id(1) == pl.num_programs(1) - 1)
    def _():
        if value_major:
            for k in range(n):
                so_ref[:, k, :] = s_ref[k]
        else:
            so_ref[...] = s_ref[...]


def _wkv_scan(r, w, k, v, a, b, s0, *, value_major):
    t, n, lanes = r.shape
    tt = min(SCAN_TIME_BLOCK, t)
    seq = pl.BlockSpec((tt, n, LANES), lambda g, i: (i, 0, g))
    if value_major:
        nb = s0.shape[3] // LANES
        st = pl.BlockSpec((None, n, n, LANES), lambda g, i: (g // nb, 0, 0, g % nb))
    else:
        st = pl.BlockSpec((n, n, LANES), lambda g, i: (0, 0, g))
    return pl.pallas_call(
        functools.partial(_wkv_scan_kernel, value_major=value_major),
        grid=(lanes // LANES, t // tt),
        in_specs=[seq] * 6 + [st],
        out_specs=(seq, st),
        out_shape=(jax.ShapeDtypeStruct((t, n, lanes), F32), jax.ShapeDtypeStruct(s0.shape, F32)),
        scratch_shapes=[pltpu.VMEM((n, n, LANES), F32)],
        compiler_params=_cparams(("parallel", "arbitrary")), name="wkv_scan",
    )(r, w, k, v, a, b, s0)


def _rwkv_post_kernel(y_ref, bonus_ref, g_ref, h_ref, p_ref, lnw_ref, lnb_ref, bd_ref, wo_ref, png_ref,
                      pgw_ref, ppw_ref, fg_ref, o_ref, *, time_minor):
    bd = bd_ref[...]
    if time_minor:
        y = jnp.concatenate([y_ref[j] for j in range(y_ref.shape[0])], axis=1).T
    else:
        y = y_ref[...]
    inv_n = 1.0 / RW_N
    yc = y - _head_sum(y, bd) * inv_n
    var = _head_sum(yc * yc, bd) * inv_n
    yn = yc * lax.rsqrt(var + RW_EPS) * lnw_ref[...] + lnb_ref[...]
    mix = ((yn + bonus_ref[...]) * g_ref[...]).astype(BF16)
    h = h_ref[...] + _dot(mix, wo_ref[...])
    h = _ple(h, p_ref[...], png_ref[...], pgw_ref[...], ppw_ref[...])
    o_ref[...] = _rms(h, fg_ref[...])


def _rwkv_post(y, bonus, g, h, p3d, layer, lnw, lnb, bd, wo, png, pgw, ppw, fg, *, time_minor):
    n, d = h.shape
    tm = min(ROW_TILE, n)
    row = lambda a: pl.BlockSpec((tm, a.shape[1]), lambda i: (i, 0))
    if time_minor:
        nt = y.shape[1] * LANES // tm
        y_spec = pl.BlockSpec((None, tm // LANES, d, LANES), lambda i: (i // nt, i % nt, 0, 0))
    else:
        y_spec = row(y)
    ws = [lnw, lnb, bd, wo, png, pgw, ppw, fg]
    return pl.pallas_call(
        functools.partial(_rwkv_post_kernel, time_minor=time_minor),
        grid=(n // tm,),
        in_specs=[y_spec, row(bonus), row(g), row(h), _layer_rows(p3d, layer, tm)] + [_full(w.shape) for w in ws],
        out_specs=row(h),
        out_shape=jax.ShapeDtypeStruct((n, d), F32),
        compiler_params=_cparams(("parallel",)), name="rwkv_post",
    )(y, bonus, g, h, p3d, *ws)


def _rope_tables(pos, rows):
    half = ROPE_DIM // 2
    inv = ROPE_THETA ** (-jnp.arange(half, dtype=F32) / half)
    ang = pos.astype(F32)[:, None] * inv[None, :]
    cos = jnp.cos(ang)
    sin = jnp.sin(ang)
    cos = jnp.tile(jnp.concatenate([cos, cos], axis=-1), (rows // pos.shape[0], MLA_HEADS))
    sin = jnp.tile(jnp.concatenate([-sin, sin], axis=-1), (rows // pos.shape[0], MLA_HEADS))
    return cos, sin


def _swap_halves(w):
    half = w.shape[-1] // 2
    return jnp.concatenate([w[..., half:], w[..., :half]], axis=-1)


def _to_lanes(x, batch, seq, batch_minor):
    heads = x.shape[1] // RW_N
    order = (1, 3, 2, 0) if batch_minor else (1, 3, 0, 2)
    return x.reshape(batch, seq, heads, RW_N).transpose(order).reshape(seq, RW_N, batch * heads)


def _from_lanes(y, batch, seq, batch_minor):
    heads = y.shape[2] // batch
    if batch_minor:
        y = y.reshape(seq, RW_N, heads, batch).transpose(3, 0, 2, 1)
    else:
        y = y.reshape(seq, RW_N, batch, heads).transpose(2, 0, 3, 1)
    return y.reshape(batch * seq, heads * RW_N)


def _row(v):
    return v.reshape(1, -1)


def _even_mixers(x, pos, W, hg_s0, paged):
    batch, seq, d = x.shape
    n = batch * seq
    x2d = x.reshape(n, d)
    row = _row
    cos_tab, sin_tab = _rope_tables(pos, max(seq, min(ROW_TILE, n)))
    zhg, qlat, qrope, ckv, kr, ckvb, krb, mg = _even_pre(
        x2d, cos_tab, sin_tab, row(W['mix_norm'][0]), W['hg_lb_logits'], W['w_in_hg'], W['w_in_mla'],
        row(W['mla_q_norm'][0]), row(W['mla_kv_norm'][0]), W['w_uq'], W['w_uk'], lb_rows=1)
    g_hg = row(W['hg_norm'][0])
    c = dict(batch=batch, seq=seq, x2d=x2d, mg=mg, ckv=ckv, kr=kr)
    if paged is None:
        c['ohg'], c['s_hg'] = _hgrn_prompt(zhg, g_hg, batch, seq)
        c['ctx'] = _attn_prompt(qlat, qrope, ckvb, krb, batch, seq)
    else:
        c['ohg'], c['s_hg'] = _hgrn_sample(zhg, hg_s0, g_hg, seq)
        cache_ckv, cache_krope_t, page_table = paged
        stack = lambda q: q.reshape(MLA_HEADS, batch, seq, q.shape[-1]).transpose(1, 0, 2, 3).reshape(
            batch, MLA_HEADS * seq, q.shape[-1])
        pad = lambda a: jnp.pad(a.reshape(batch, seq, a.shape[-1]), ((0, 0), (0, 2 * SUBLANES - seq), (0, 0)))
        c['attn_args'] = (page_table, stack(qlat), stack(qrope), pad(ckvb), pad(krb).transpose(0, 2, 1),
                          cache_ckv, cache_krope_t, seq)
    return c


def _stacked_ctx_to_rows(ctx, batch, seq):
    return ctx.reshape(batch, MLA_HEADS, seq, KV_LORA).transpose(0, 2, 1, 3).reshape(batch * seq, MLA_HEADS * KV_LORA)


def _rwkv_operands(c, p, W, wkv_s0, shift_s0):
    batch, seq, x2d = c['batch'], c['seq'], c['x2d']
    n, d = x2d.shape
    row = _row
    p3d = p.reshape(p.shape[0], n, p.shape[-1])
    h1, hn1 = _even_post(
        x2d, c['ohg'], c['ctx'], c['mg'], p3d, 0, W['w_uv_bd'], W['w_out'], row(W['ple_norm'][0]),
        W['ple_gate_b'][0], W['ple_proj_b'][0], row(W['mix_norm'][1]))
    hn3 = hn1.reshape(batch, seq, d)
    heads = d // RW_N
    batch_lanes = batch % LANES == 0
    time_minor = seq % LANES == 0 and batch * heads == LANES
    if time_minor:
        tm = min(ROW_TILE, n)
        tile_last = hn3[:, tm - 1::tm, :]
        prev = jnp.concatenate([shift_s0[:, None, :], tile_last[:, :-1]], axis=1).reshape(n // tm, 1, d)
    else:
        prev = jnp.concatenate([shift_s0[:, None, :], hn3[:, :-1]], axis=1).reshape(n, d)
    *scan_in, bonus, g = _rwkv_pre(
        hn1, prev, W['rw_mu'][0], W['w_r'], W['w_k'], W['w_v'], W['w_g'], row(W['rw_w0'][0]), W['w_w1'],
        W['w_w2'], row(W['rw_a0'][0]), W['w_a1'], W['w_a2'], row(W['rw_k_k'][0]), row(W['rw_k_a'][0]),
        row(W['rw_r_k'][0]), W['head_bd'], batch=batch, time_minor=time_minor)
    if time_minor:
        scan_in = _retile([a.reshape(batch, seq // LANES, heads, RW_N, LANES) for a in scan_in], True)
    else:
        scan_in = [_to_lanes(a, batch, seq, batch_lanes) for a in scan_in]
    if batch_lanes:
        scan_s0 = wkv_s0.transpose(1, 2, 3, 0)
    else:
        scan_s0 = wkv_s0.transpose(3, 2, 0, 1).reshape(RW_N, RW_N, -1)
    c.update(scan_in=scan_in, scan_s0=scan_s0, batch_lanes=batch_lanes, time_minor=time_minor, bonus=bonus, g=g,
             h1=h1, hn3=hn3, p3d=p3d)
    return c


def _group_outputs(c, y, s_wkv, W):
    batch, seq, h1 = c['batch'], c['seq'], c['h1']
    d = h1.shape[1]
    heads = d // RW_N
    row = _row
    time_minor, batch_lanes = c['time_minor'], c['batch_lanes']
    if batch_lanes:
        s_wkv = s_wkv.transpose(3, 0, 1, 2)
    else:
        s_wkv = s_wkv.reshape(RW_N, RW_N, batch, heads).transpose(2, 3, 1, 0)
    if time_minor:
        y = _retile([y], False, batch)[0].reshape(batch, seq // LANES, d, LANES)
    else:
        y = _from_lanes(y, batch, seq, batch_lanes)
    out = _rwkv_post(
        y, c['bonus'], c['g'], h1, c['p3d'], 1, row(W['rw_ln_w'][0]), row(W['rw_ln_b'][0]), W['head_bd'],
        W['w_o'], row(W['ple_norm'][1]), W['ple_gate_b'][1], W['ple_proj_b'][1], row(W['final_norm']),
        time_minor=time_minor)
    return (out.reshape(batch, seq, d), c['ckv'].reshape(1, batch, seq, KV_LORA),
            c['kr'].reshape(1, batch, seq, ROPE_DIM), c['s_hg'][None], s_wkv[None], c['hn3'][:, -1][None])


def kernel(x_prompt, x_sample, cache_ckv, cache_krope, state_hgrn, state_wkv, state_shift, page_table, p_prompt, p_sample, mix_norm, ev_w_in, hg_lb_logits, hg_norm, mla_q_norm, mla_w_uq, mla_kv_norm, mla_w_uk, mla_w_uv, ev_w_out, rw_mu, rw_w_rkvg, rw_w0, rw_w1, rw_w2, rw_a0, rw_a1, rw_a2, rw_k_k, rw_k_a, rw_r_k, rw_ln_w, rw_ln_b, rw_w_o, ple_norm, ple_gate, ple_proj, final_norm):
    bf = lambda a: a.astype(BF16)
    d = x_prompt.shape[-1]
    w_in = ev_w_in[0]
    o = np.cumsum([0, HG_WIDTH, HG_WIDTH, HG_HEADS * HG_DV, HG_HEADS * HG_DV, Q_LORA, KV_LORA, ROPE_DIM, MLA_WIDTH])
    cq, ckv_w, kr_w, mg_w = (w_in[:, o[4]:o[5]], w_in[:, o[5]:o[6]], w_in[:, o[6]:o[7]], w_in[:, o[7]:o[8]])
    uq = mla_w_uq[0].reshape(Q_LORA, MLA_HEADS, NOPE_DIM + ROPE_DIM)
    uq_rope = uq[:, :, NOPE_DIM:]
    uv_bd = jnp.zeros((MLA_HEADS, KV_LORA, MLA_HEADS, V_DIM), F32)
    uv_bd = uv_bd.at[jnp.arange(MLA_HEADS), :, jnp.arange(MLA_HEADS), :].set(mla_w_uv[0].transpose(0, 2, 1))
    hid = np.arange(MXU_DIM) // RW_N
    W = dict(
        mix_norm=mix_norm, hg_lb_logits=hg_lb_logits, hg_norm=hg_norm, mla_q_norm=mla_q_norm,
        mla_kv_norm=mla_kv_norm, ple_norm=ple_norm, final_norm=final_norm, rw_mu=rw_mu, rw_w0=rw_w0, rw_a0=rw_a0,
        rw_k_k=rw_k_k, rw_k_a=rw_k_a, rw_ln_w=rw_ln_w, rw_ln_b=rw_ln_b, rw_r_k=rw_r_k.reshape(rw_r_k.shape[0], -1),
        w_in_hg=bf(w_in[:, :o[4]]),
        w_in_mla=bf(jnp.concatenate([cq, ckv_w, mg_w, kr_w, _swap_halves(kr_w)], axis=-1)),
        w_uq=bf(jnp.concatenate([uq[:, :, :NOPE_DIM].reshape(Q_LORA, -1), uq_rope.reshape(Q_LORA, -1),
                                 _swap_halves(uq_rope).reshape(Q_LORA, -1)], axis=-1)),
        w_uk=bf(mla_w_uk[0]),
        w_uv_bd=bf(uv_bd.reshape(MLA_HEADS * KV_LORA, MLA_WIDTH)),
        w_out=bf(ev_w_out[0]),
        ple_gate_b=bf(ple_gate), ple_proj_b=bf(ple_proj),
        w_r=bf(rw_w_rkvg[0, 0]), w_k=bf(rw_w_rkvg[0, 1]), w_v=bf(rw_w_rkvg[0, 2]), w_g=bf(rw_w_rkvg[0, 3]),
        w_w1=bf(rw_w1[0]), w_w2=bf(rw_w2[0]), w_a1=bf(rw_a1[0]), w_a2=bf(rw_a2[0]), w_o=bf(rw_w_o[0]),
        head_bd=jnp.asarray(hid[:, None] == hid[None, :], BF16),
    )
    bp, tp, _ = x_prompt.shape
    bs, ts, _ = x_sample.shape
    past_len = page_table.shape[1] * cache_ckv.shape[2]
    heads = d // RW_N
    paged = (cache_ckv.reshape(cache_ckv.shape[1:]), jnp.swapaxes(cache_krope.reshape(cache_krope.shape[1:]), 1, 2),
             page_table)
    cp = _even_mixers(x_prompt, jnp.arange(tp), W, None, None)
    cp = _rwkv_operands(cp, p_prompt, W, jnp.zeros((bp, heads, RW_N, RW_N), F32), jnp.zeros((bp, d), F32))
    cs = _even_mixers(x_sample, past_len + jnp.arange(ts), W, state_hgrn[0], paged)
    y_p, wkv_p = _wkv_scan(*cp['scan_in'], cp['scan_s0'], value_major=cp['batch_lanes'])
    yp, ckv_p, kr_p, hg_p, wkv_p, sh_p = _group_outputs(cp, y_p, wkv_p, W)
    cs['ctx'] = _stacked_ctx_to_rows(_attn_sample(*cs['attn_args']), bs, ts)
    cs = _rwkv_operands(cs, p_sample, W, state_wkv[0], state_shift[0])
    y_s, wkv_s = _wkv_scan(*cs['scan_in'], cs['scan_s0'], value_major=cs['batch_lanes'])
    ys, ckv_s, kr_s, hg_s, wkv_s, sh_s = _group_outputs(cs, y_s, wkv_s, W)
    return (yp, ys, ckv_p, kr_p, ckv_s, kr_s, hg_p, hg_s, wkv_p, wkv_s, sh_p, sh_s)
```

```python
import functools

import jax
import jax.numpy as jnp
import numpy as np
from jax import lax
from jax.experimental import pallas as pl
from jax.experimental.pallas import tpu as pltpu

F32 = jnp.float32
BF16 = jnp.bfloat16

NORM_EPS = 1e-6
HG_HEADS = 4
HG_DK = 128
HG_DV = 128
HG_WIDTH = HG_HEADS * HG_DK
MLA_HEADS = 8
Q_LORA = 384
KV_LORA = 256
NOPE_DIM = 64
ROPE_DIM = 32
V_DIM = 64
MLA_WIDTH = MLA_HEADS * V_DIM
MLA_SCALE = (NOPE_DIM + ROPE_DIM) ** -0.5
ROPE_THETA = 10000.0
RW_N = 64
RW_EPS = 64e-5

LANES = 128
SUBLANES = 8
MXU_DIM = 256
VMEM_LIMIT_BYTES = 56 * 1024 * 1024

ROW_TILE = 256
HG_CHUNK = 64
HG_BLOCK = SUBLANES
HG_TIME_BLOCK = 1024
HG_SEQS_PER_STEP = 2
ATT_BLOCK = 256
PAGES_PER_STEP = 32
PAGES_PER_BLOCK = 8
PAGE_RING = 4
SCAN_TIME_BLOCK = 64


def _cparams(sem):
    return pltpu.CompilerParams(dimension_semantics=sem, vmem_limit_bytes=VMEM_LIMIT_BYTES)


def _rms(x, g):
    return x * lax.rsqrt(jnp.mean(x * x, axis=-1, keepdims=True) + NORM_EPS) * g


def _sigmoid(x):
    return 1.0 / (1.0 + jnp.exp(-x))


def _silu(x):
    return x * _sigmoid(x)


def _dot(a, b):
    return jnp.dot(a, b, preferred_element_type=F32)


def _dot_nt(a, b):
    return lax.dot_general(a, b, (((1,), (1,)), ((), ())), preferred_element_type=F32)


def _head_sum(x, bd):
    hi = x.astype(BF16)
    lo = (x - hi.astype(F32)).astype(BF16)
    outs = []
    for c in range(x.shape[-1] // MXU_DIM):
        sl = slice(c * MXU_DIM, (c + 1) * MXU_DIM)
        outs.append(_dot(hi[:, sl], bd) + _dot(lo[:, sl], bd))
    return jnp.concatenate(outs, axis=-1)


def _full(shape):
    nd = len(shape)
    return pl.BlockSpec(shape, lambda *_: (0,) * nd)


def _even_pre_kernel(x_ref, cos_ref, sin_ref, g_ref, lbl_ref, w1_ref, w2_ref, qg_ref, kvg_ref, wuq_ref,
                     wuk_ref, zhg_ref, qlat_ref, qrope_ref, ckv_ref, kr_ref, ckvb_ref, krb_ref, mg_ref,
                     *, lb_rows):
    hn = _rms(x_ref[...], g_ref[...]).astype(BF16)
    z2 = _dot(hn, w2_ref[...])
    z1 = _dot(hn, w1_ref[...])
    lg = lbl_ref[...]
    e = jnp.exp(lg - jnp.max(lg, axis=0, keepdims=True))
    p = e / jnp.sum(e, axis=0, keepdims=True)
    lb = jnp.sum(p[:lb_rows], axis=0, keepdims=True)
    W = HG_WIDTH
    f = lb + (1.0 - lb) * _sigmoid(z1[:, W:2 * W])
    zhg_ref[:, 0:W] = _silu(z1[:, 0:W])
    zhg_ref[:, W:2 * W] = jnp.log(f)
    zhg_ref[:, 2 * W:3 * W] = z1[:, 2 * W:3 * W]
    zhg_ref[:, 3 * W:4 * W] = _silu(z1[:, 3 * W:4 * W])
    o_kv = Q_LORA
    o_mg = o_kv + KV_LORA
    o_kr = o_mg + MLA_WIDTH
    o_krs = o_kr + ROPE_DIM
    cqn = _rms(z2[:, 0:o_kv], qg_ref[...]).astype(BF16)
    qf = _dot(cqn, wuq_ref[...])
    cos = cos_ref[...]
    sin = sin_ref[...]
    n_nope = MLA_HEADS * NOPE_DIM
    n_rope = MLA_HEADS * ROPE_DIM
    qr = (qf[:, n_nope:n_nope + n_rope] * cos + qf[:, n_nope + n_rope:n_nope + 2 * n_rope] * sin) * MLA_SCALE
    for h in range(MLA_HEADS):
        qrope_ref[h] = qr[:, h * ROPE_DIM:(h + 1) * ROPE_DIM].astype(BF16)
        qn = qf[:, h * NOPE_DIM:(h + 1) * NOPE_DIM].astype(BF16)
        qlat_ref[h] = (_dot(qn, wuk_ref[h]) * MLA_SCALE).astype(BF16)
    ckv = _rms(z2[:, o_kv:o_mg], kvg_ref[...])
    ckv_ref[...] = ckv
    ckvb_ref[...] = ckv.astype(BF16)
    kr = z2[:, o_kr:o_krs] * cos[:, :ROPE_DIM] + z2[:, o_krs:o_krs + ROPE_DIM] * sin[:, :ROPE_DIM]
    kr_ref[...] = kr
    krb_ref[...] = kr.astype(BF16)
    mg_ref[...] = _silu(z2[:, o_mg:o_kr])


def _even_pre(x2d, cos_tab, sin_tab, g, lb_logits, w1, w2, qg, kvg, wuq, wuk, *, lb_rows):
    n, d = x2d.shape
    tm = min(ROW_TILE, n)
    n_tab = cos_tab.shape[0] // tm
    row = lambda i: (i, 0)
    tab = lambda i: (i % n_tab, 0)
    out_shape = (
        jax.ShapeDtypeStruct((n, 4 * HG_WIDTH), F32),
        jax.ShapeDtypeStruct((MLA_HEADS, n, KV_LORA), BF16),
        jax.ShapeDtypeStruct((MLA_HEADS, n, ROPE_DIM), BF16),
        jax.ShapeDtypeStruct((n, KV_LORA), F32),
        jax.ShapeDtypeStruct((n, ROPE_DIM), F32),
        jax.ShapeDtypeStruct((n, KV_LORA), BF16),
        jax.ShapeDtypeStruct((n, ROPE_DIM), BF16),
        jax.ShapeDtypeStruct((n, MLA_WIDTH), F32),
    )
    out_specs = (
        pl.BlockSpec((tm, 4 * HG_WIDTH), row),
        pl.BlockSpec((MLA_HEADS, tm, KV_LORA), lambda i: (0, i, 0)),
        pl.BlockSpec((MLA_HEADS, tm, ROPE_DIM), lambda i: (0, i, 0)),
        pl.BlockSpec((tm, KV_LORA), row),
        pl.BlockSpec((tm, ROPE_DIM), row),
        pl.BlockSpec((tm, KV_LORA), row),
        pl.BlockSpec((tm, ROPE_DIM), row),
        pl.BlockSpec((tm, MLA_WIDTH), row),
    )
    in_specs = [
        pl.BlockSpec((tm, d), row),
        pl.BlockSpec((tm, cos_tab.shape[1]), tab),
        pl.BlockSpec((tm, sin_tab.shape[1]), tab),
        _full(g.shape), _full(lb_logits.shape), _full(w1.shape), _full(w2.shape), _full(qg.shape),
        _full(kvg.shape), _full(wuq.shape), _full(wuk.shape),
    ]
    return pl.pallas_call(
        functools.partial(_even_pre_kernel, lb_rows=lb_rows),
        grid=(n // tm,), in_specs=in_specs, out_specs=out_specs, out_shape=out_shape,
        compiler_params=_cparams(("parallel",)), name="even_pre",
    )(x2d, cos_tab, sin_tab, g, lb_logits, w1, w2, qg, kvg, wuq, wuk)


def _tril_ones(n, block):
    r = lax.broadcasted_iota(jnp.int32, (n, n), 0)
    c = lax.broadcasted_iota(jnp.int32, (n, n), 1)
    return ((r >= c) & ((r // block) == (c // block))).astype(F32)


def _hgrn_exact_blocks(q, k, v, b, block):
    rows = q.shape[0]
    rid = lax.broadcasted_iota(jnp.int32, (rows, 1), 0) % block
    o = jnp.sum(q * k, axis=-1, keepdims=True) * v
    for d in range(1, block):
        kd = pltpu.roll(k, d, 0)
        bd = pltpu.roll(b, d, 0)
        vd = pltpu.roll(v, d, 0)
        w = jnp.sum(q * kd * jnp.exp(jnp.minimum(b - bd, 0.0)), axis=-1, keepdims=True)
        o = o + jnp.where(rid >= d, w, 0.0) * vd
    return o


def _hgrn_finish(o, g, gate):
    return (o * lax.rsqrt(jnp.mean(o * o, axis=-1, keepdims=True) + NORM_EPS) * g * gate).astype(BF16)


def _hgrn_prompt_kernel(q_ref, lf_ref, v_ref, gt_ref, g_ref, o_ref, s_ref, st_ref):
    C = HG_CHUNK
    nseq, tb, _ = q_ref.shape
    tril = _tril_ones(C, C)
    rid = lax.broadcasted_iota(jnp.int32, (C, 1), 0)
    rr = lax.broadcasted_iota(jnp.int32, (C, C), 0)
    cc = lax.broadcasted_iota(jnp.int32, (C, C), 1)
    g = g_ref[...]

    @pl.when(pl.program_id(1) == 0)
    def _():
        st_ref[...] = jnp.zeros(st_ref.shape, F32)

    def decay(x):
        x['k'] = 1.0 - jnp.exp(x['lf'])
        x['b'] = jnp.dot(tril, x['lf'], precision=lax.Precision.HIGHEST, preferred_element_type=F32)

    def scores(x):
        q, k, b = x['q'], x['k'], x['b']
        x['o'] = _dot_nt((q * jnp.exp(b)).astype(BF16), x['st'].astype(BF16))
        x['att'] = []
        m = HG_BLOCK
        while 2 * m <= C:
            nb = C // (2 * m)
            b3 = b.reshape(nb, 2 * m, HG_DK)
            ref = jnp.broadcast_to(b3[:, m - 1:m, :], (nb, 2 * m, HG_DK)).reshape(C, HG_DK)
            upper = (rid % (2 * m)) >= m
            qt = q * jnp.where(upper, jnp.exp(jnp.minimum(b - ref, 0.0)), 0.0)
            kt = k * jnp.where(upper, 0.0, jnp.exp(jnp.minimum(ref - b, 0.0)))
            x['att'].append((m, _dot_nt(qt.astype(BF16), kt.astype(BF16))))
            m *= 2
        bend = b[C - 1:C, :]
        x['st'] = x['st'] * jnp.exp(bend) + _dot(x['v'].T.astype(BF16), (k * jnp.exp(bend - b)).astype(BF16))
        x['o'] = x['o'] + _hgrn_exact_blocks(q, k, x['v'], b, HG_BLOCK)

    def values(x):
        att = jnp.zeros((C, C), F32)
        for m, a in x['att']:
            if 2 * m < C:
                a = jnp.where((rr // (2 * m)) == (cc // (2 * m)), a, 0.0)
            att = att + a
        x['pv'] = _dot(att.astype(BF16), x['v'].astype(BF16))

    def chunk(c, carry):
        sl = pl.ds(pl.multiple_of(c * C, C), C)
        xs = [dict(s=s, h=h, hs=slice(h * HG_DK, (h + 1) * HG_DK)) for s in range(nseq) for h in range(HG_HEADS)]
        for x in xs:
            s, h, hs = x['s'], x['h'], x['hs']
            x.update(st=st_ref[s, h], q=q_ref[s, sl, hs], lf=lf_ref[s, sl, hs], v=v_ref[s, sl, hs],
                     gate=gt_ref[s, sl, hs])
        for stage in (decay, scores, values):
            for x in xs:
                stage(x)
        for x in xs:
            o_ref[x['s'], sl, x['hs']] = _hgrn_finish(x['o'] + x['pv'], g, x['gate'])
            st_ref[x['s'], x['h']] = x['st']
        return carry

    lax.fori_loop(0, tb // C, chunk, 0)

    @pl.when(pl.program_id(1) == pl.num_programs(1) - 1)
    def _():
        for s in range(nseq):
            for h in range(HG_HEADS):
                s_ref[s, h] = st_ref[s, h].T


def _hgrn_prompt(zhg, g, batch, seq):
    nseq = HG_SEQS_PER_STEP if batch % HG_SEQS_PER_STEP == 0 else 1
    tb = min(HG_TIME_BLOCK // nseq, seq)
    z3 = zhg.reshape(batch, seq, 4 * HG_WIDTH)
    blk = lambda off: pl.BlockSpec((nseq, tb, HG_WIDTH), lambda b, i: (b, i, off))
    ohg, s_hg = pl.pallas_call(
        _hgrn_prompt_kernel,
        grid=(batch // nseq, seq // tb),
        in_specs=[blk(0), blk(1), blk(2), blk(3), _full(g.shape)],
        out_specs=(pl.BlockSpec((nseq, tb, HG_HEADS * HG_DV), lambda b, i: (b, i, 0)),
                   pl.BlockSpec((nseq, HG_HEADS, HG_DK, HG_DV), lambda b, i: (b, 0, 0, 0))),
        out_shape=(jax.ShapeDtypeStruct((batch, seq, HG_HEADS * HG_DV), BF16),
                   jax.ShapeDtypeStruct((batch, HG_HEADS, HG_DK, HG_DV), F32)),
        scratch_shapes=[pltpu.VMEM((nseq, HG_HEADS, HG_DV, HG_DK), F32)],
        compiler_params=_cparams(("parallel", "arbitrary")), name="hgrn_prompt",
    )(z3, z3, z3, z3, g)
    return ohg.reshape(batch * seq, HG_HEADS * HG_DV), s_hg


def _hgrn_sample_kernel(z_ref, s0_ref, g_ref, o_ref, s_ref, *, seq):
    rows = z_ref.shape[0]
    nb = rows // seq
    tril = _tril_ones(rows, seq)
    rb = lax.broadcasted_iota(jnp.int32, (rows, 1), 0) // seq
    cb = lax.broadcasted_iota(jnp.int32, (1, rows), 1) // seq
    g = g_ref[...]
    W = HG_WIDTH
    for h in range(HG_HEADS):
        hs = slice(h * HG_DK, (h + 1) * HG_DK)
        q = z_ref[:, hs]
        lf = z_ref[:, W + h * HG_DK:W + (h + 1) * HG_DK]
        v = z_ref[:, 2 * W + h * HG_DV:2 * W + (h + 1) * HG_DV]
        gate = z_ref[:, 3 * W + h * HG_DV:3 * W + (h + 1) * HG_DV]
        k = 1.0 - jnp.exp(lf)
        b = jnp.dot(tril, lf, precision=lax.Precision.HIGHEST, preferred_element_type=F32)
        o = _hgrn_exact_blocks(q, k, v, b, seq)
        qe = (q * jnp.exp(b)).astype(BF16)
        bt = b.T
        vb = v.astype(BF16)
        for i in range(nb):
            s0 = s0_ref[i, h]
            o = o + jnp.where(rb == i, _dot(qe, s0.astype(BF16)), 0.0)
            last = i * seq + seq - 1
            bend_row = b[last:last + 1, :]
            kt = jnp.where(rb == i, k * jnp.exp(bend_row - b), 0.0)
            ktt = kt.T.astype(BF16)
            s_ref[i, h] = s0 * jnp.exp(bt[:, last:last + 1]) + _dot(ktt, vb)
        o_ref[:, hs] = _hgrn_finish(o, g, gate)


def _hgrn_sample(zhg, s0, g, seq):
    n = zhg.shape[0]
    nb = 8
    rows = nb * seq
    return pl.pallas_call(
        functools.partial(_hgrn_sample_kernel, seq=seq),
        grid=(n // rows,),
        in_specs=[pl.BlockSpec((rows, 4 * HG_WIDTH), lambda i: (i, 0)),
                  pl.BlockSpec((nb, HG_HEADS, HG_DK, HG_DV), lambda i: (i, 0, 0, 0)),
                  _full(g.shape)],
        out_specs=(pl.BlockSpec((rows, HG_HEADS * HG_DV), lambda i: (i, 0)),
                   pl.BlockSpec((nb, HG_HEADS, HG_DK, HG_DV), lambda i: (i, 0, 0, 0))),
        out_shape=(jax.ShapeDtypeStruct((n, HG_HEADS * HG_DV), BF16),
                   jax.ShapeDtypeStruct(s0.shape, F32)),
        compiler_params=_cparams(("parallel",)), name="hgrn_sample",
    )(zhg, s0, g)


def _lane_tile(x, width):
    if width <= LANES:
        return x[:, :width]
    return jnp.concatenate([x] * (width // LANES), axis=1)


def _softmax_init(m_ref, l_ref, acc_ref):
    m_ref[...] = jnp.full(m_ref.shape, -jnp.inf, F32)
    l_ref[...] = jnp.zeros(l_ref.shape, F32)
    acc_ref[...] = jnp.zeros(acc_ref.shape, F32)


def _online_step(s, kv, m_old, l_old, acc_old):
    m_new = jnp.maximum(m_old, jnp.max(s, axis=-1, keepdims=True))
    alpha = jnp.exp(m_old - m_new)
    p = jnp.exp(s - _lane_tile(m_new, s.shape[1]))
    l_new = alpha * l_old + jnp.sum(p, axis=-1, keepdims=True)
    acc_new = _lane_tile(alpha, acc_old.shape[1]) * acc_old + _dot(p.astype(BF16), kv)
    return m_new, l_new, acc_new


def _online_update(s, kv, rows, m_ref, l_ref, acc_ref):
    m_ref[rows, :], l_ref[rows, :], acc_ref[rows, :] = _online_step(
        s, kv, m_ref[rows, :], l_ref[rows, :], acc_ref[rows, :])


def _attn_prompt_kernel(ql_ref, qr_ref, kv_ref, kr_ref, o_ref, m_ref, l_ref, acc_ref):
    tq = ql_ref.shape[1]
    i = pl.program_id(1)
    _softmax_init(m_ref, l_ref, acc_ref)
    n_split = 2
    hs = MLA_HEADS // n_split
    mh = hs * tq
    qpos = lax.broadcasted_iota(jnp.int32, (mh, tq), 0) % tq
    kpos = lax.broadcasted_iota(jnp.int32, (mh, tq), 1)

    def block(j, masked):
        sl = pl.ds(pl.multiple_of(j * tq, tq), tq)
        kv = kv_ref[sl, :]
        kr = kr_ref[sl, :]
        scores = []
        for r in range(n_split):
            ql = ql_ref[r * hs:(r + 1) * hs].reshape(mh, KV_LORA)
            qr = qr_ref[r * hs:(r + 1) * hs].reshape(mh, ROPE_DIM)
            s = _dot_nt(ql, kv) + _dot_nt(qr, kr)
            scores.append(jnp.where(kpos <= qpos, s, -jnp.inf) if masked else s)
        for r, s in enumerate(scores):
            _online_update(s, kv, slice(r * mh, (r + 1) * mh), m_ref, l_ref, acc_ref)

    def body(j, carry):
        block(j, False)
        return carry

    lax.fori_loop(0, i, body, 0)
    block(i, True)
    for h in range(MLA_HEADS):
        rows = slice(h * tq, (h + 1) * tq)
        out = acc_ref[rows, :] / _lane_tile(l_ref[rows, :], KV_LORA)
        o_ref[:, h * KV_LORA:(h + 1) * KV_LORA] = out.astype(BF16)


def _attn_prompt(qlat, qrope, ckvb, krb, batch, seq):
    tq = ATT_BLOCK
    nq = seq // tq
    rows = MLA_HEADS * tq
    return pl.pallas_call(
        _attn_prompt_kernel,
        grid=(batch, nq),
        in_specs=[pl.BlockSpec((MLA_HEADS, tq, KV_LORA), lambda b, i: (0, b * nq + i, 0)),
                  pl.BlockSpec((MLA_HEADS, tq, ROPE_DIM), lambda b, i: (0, b * nq + i, 0)),
                  pl.BlockSpec((seq, KV_LORA), lambda b, i: (b, 0)),
                  pl.BlockSpec((seq, ROPE_DIM), lambda b, i: (b, 0))],
        out_specs=pl.BlockSpec((tq, MLA_HEADS * KV_LORA), lambda b, i: (b * nq + i, 0)),
        out_shape=jax.ShapeDtypeStruct((batch * seq, MLA_HEADS * KV_LORA), BF16),
        scratch_shapes=[pltpu.VMEM((rows, LANES), F32), pltpu.VMEM((rows, LANES), F32),
                        pltpu.VMEM((rows, KV_LORA), F32)],
        compiler_params=_cparams(("parallel", "arbitrary")), name="attn_prompt",
    )(qlat, qrope, ckvb, krb)


class _PageRing:
    def __init__(self, pt_ref, ckv_hbm, kr_hbm, kvbuf, krbuf, sems, units_per_seq):
        self.pt_ref, self.ckv_hbm, self.kr_hbm = pt_ref, ckv_hbm, kr_hbm
        self.kvbuf, self.krbuf, self.sems = kvbuf, krbuf, sems
        self.units_per_seq = units_per_seq
        self.pages = krbuf.shape[1]
        self.page = kvbuf.shape[1] // self.pages

    def _copies(self, u):
        slot = u % PAGE_RING
        seq_id = u // self.units_per_seq
        first = (u % self.units_per_seq) * self.pages
        out = []
        for j in range(self.pages):
            pg = self.pt_ref[seq_id, first + j]
            out.append(pltpu.make_async_copy(
                self.ckv_hbm.at[pg], self.kvbuf.at[slot, pl.ds(j * self.page, self.page), :], self.sems.at[0, slot]))
            out.append(pltpu.make_async_copy(self.kr_hbm.at[pg], self.krbuf.at[slot, j], self.sems.at[1, slot]))
        return out

    def prime(self, total):
        for u in range(PAGE_RING - 1):
            @pl.when(u < total)
            def _():
                for c in self._copies(u):
                    c.start()

    def wait(self, u):
        for c in self._copies(u):
            c.wait()

    def refill(self, u, total):
        @pl.when(u + PAGE_RING - 1 < total)
        def _():
            for c in self._copies(u + PAGE_RING - 1):
                c.start()

    def attend(self, u, ql, qr, state):
        slot = u % PAGE_RING
        sub = min(PAGES_PER_BLOCK, self.pages)
        blocks = []
        for c in range(0, self.pages, sub):
            kv = self.kvbuf[slot, c * self.page:(c + sub) * self.page, :].astype(BF16)
            kr = jnp.concatenate([self.krbuf[slot, c + j].astype(BF16) for j in range(sub)], axis=1)
            blocks.append((_dot_nt(ql, kv) + _dot(qr, kr), kv))
        for s, kv in blocks:
            state = _online_step(s, kv, *state)
        return state


def _attend_new_rows(ql, qr, kvn, krn, seq, state):
    sn = _dot_nt(ql, kvn) + _dot(qr, krn)
    qpos = lax.broadcasted_iota(jnp.int32, sn.shape, 0) % seq
    kpos = lax.broadcasted_iota(jnp.int32, sn.shape, 1)
    _, l, acc = _online_step(jnp.where(kpos <= qpos, sn, -jnp.inf), kvn, *state)
    return (acc / _lane_tile(l, KV_LORA)).astype(BF16)


def _attn_sample_kernel(pt_ref, ql_ref, qr_ref, kvn_ref, krn_ref, ckv_hbm, kr_hbm, o_ref,
                        kvbuf, krbuf, sems, m_ref, l_ref, acc_ref, *, seq):
    n_groups = pl.num_programs(1)
    g = pl.program_id(1)
    unit = pl.program_id(0) * n_groups + g
    total = pl.num_programs(0) * n_groups
    ring = _PageRing(pt_ref, ckv_hbm, kr_hbm, kvbuf, krbuf, sems, n_groups)

    @pl.when(unit == 0)
    def _():
        ring.prime(total)

    @pl.when(g == 0)
    def _():
        _softmax_init(m_ref, l_ref, acc_ref)

    ring.wait(unit)
    ql = ql_ref[...]
    qr = qr_ref[...]
    state = ring.attend(unit, ql, qr, (m_ref[...], l_ref[...], acc_ref[...]))
    m_ref[...], l_ref[...], acc_ref[...] = state
    ring.refill(unit, total)

    @pl.when(g == n_groups - 1)
    def _():
        o_ref[...] = _attend_new_rows(ql, qr, kvn_ref[...], krn_ref[...], seq, state)


def _page_ring_scratch(pages, page, rows):
    return [pltpu.VMEM((PAGE_RING, pages * page, KV_LORA), F32),
            pltpu.VMEM((PAGE_RING, pages, ROPE_DIM, page), F32),
            pltpu.SemaphoreType.DMA((2, PAGE_RING)),
            pltpu.VMEM((rows, LANES), F32), pltpu.VMEM((rows, LANES), F32), pltpu.VMEM((rows, KV_LORA), F32)]


def _attn_sample(page_table, qlat, qrope, kv_new, kr_new_t, cache_ckv, cache_krope_t, seq):
    batch, n_pages = page_table.shape
    G = min(PAGES_PER_STEP, n_pages)
    page = cache_ckv.shape[1]
    rows = qlat.shape[1]
    npad = kv_new.shape[1]

    per_b = lambda r, width: pl.BlockSpec((None, r, width), lambda b, g, pt: (b, 0, 0))
    in_hbm = pl.BlockSpec(memory_space=pl.ANY)
    grid_spec = pltpu.PrefetchScalarGridSpec(
        num_scalar_prefetch=1,
        grid=(batch, n_pages // G),
        in_specs=[per_b(rows, KV_LORA), per_b(rows, ROPE_DIM), per_b(npad, KV_LORA), per_b(ROPE_DIM, npad),
                  in_hbm, in_hbm],
        out_specs=per_b(rows, KV_LORA),
        scratch_shapes=_page_ring_scratch(G, page, rows),
    )
    return pl.pallas_call(
        functools.partial(_attn_sample_kernel, seq=seq),
        grid_spec=grid_spec,
        out_shape=jax.ShapeDtypeStruct((batch, rows, KV_LORA), BF16),
        compiler_params=_cparams(("arbitrary", "arbitrary")), name="attn_sample",
    )(page_table, qlat, qrope, kv_new, kr_new_t, cache_ckv, cache_krope_t)


def _ple(h, emb, png, pgw):
    gate = _sigmoid(_dot(_rms(h, png).astype(BF16), pgw))
    return h + gate * emb


def _even_post_kernel(x_ref, ohg_ref, ctx_ref, mg_ref, p_ref, wuv_ref, wout_ref, png_ref, pgw_ref, ppw_ref,
                      ng_ref, h_ref, hn_ref, last_ref):
    n_hg = HG_HEADS * HG_DV
    ctx_up = _dot(ctx_ref[...], wuv_ref[...])
    hg_out = _dot(ohg_ref[...], wout_ref[0:n_hg, :])
    emb = _dot(p_ref[...].astype(BF16), ppw_ref[...])
    o_mla = (ctx_up * mg_ref[...]).astype(BF16)
    h = x_ref[...] + hg_out + _dot(o_mla, wout_ref[n_hg:, :])
    h = _ple(h, emb, png_ref[...], pgw_ref[...])
    h_ref[...] = h
    hn = _rms(h, ng_ref[...])
    hn_ref[...] = hn
    last_ref[...] = hn[-1:, :]


def _layer_rows(p3d, layer, tm):
    return pl.BlockSpec((None, tm, p3d.shape[2]), lambda i: (layer, i, 0))


def _even_post(x2d, ohg, ctx, mg, p3d, layer, wuv, wout, png, pgw, ppw, ng):
    n, d = x2d.shape
    tm = min(ROW_TILE, n)
    row = lambda a: pl.BlockSpec((tm, a.shape[1]), lambda i: (i, 0))
    return pl.pallas_call(
        _even_post_kernel,
        grid=(n // tm,),
        in_specs=[row(x2d), row(ohg), row(ctx), row(mg), _layer_rows(p3d, layer, tm), _full(wuv.shape),
                  _full(wout.shape), _full(png.shape), _full(pgw.shape), _full(ppw.shape), _full(ng.shape)],
        out_specs=(row(x2d), row(x2d), pl.BlockSpec((None, 1, d), lambda i: (i, 0, 0))),
        out_shape=(jax.ShapeDtypeStruct((n, d), F32), jax.ShapeDtypeStruct((n, d), F32),
                   jax.ShapeDtypeStruct((n // tm, 1, d), F32)),
        compiler_params=_cparams(("parallel",)), name="even_post",
    )(x2d, ohg, ctx, mg, p3d, wuv, wout, png, pgw, ppw, ng)


def _rwkv_pre_kernel(hn_ref, pv_ref, mu_ref, wr_ref, wk_ref, wv_ref, wg_ref, w0_ref, w1_ref, w2_ref, a0_ref,
                     a1_ref, a2_ref, kk_ref, ka_ref, rk_ref, bd_ref,
                     r_ref, w_ref, k_ref, v_ref, na_ref, b_ref, bonus_ref, g_ref, *, time_minor):
    def put(ref, x):
        if time_minor:
            xt = x.T
            for j in range(ref.shape[0]):
                ref[j] = xt[:, j * LANES:(j + 1) * LANES]
        else:
            ref[...] = x

    hn = hn_ref[...]
    if time_minor:
        first = lax.broadcasted_iota(jnp.int32, (hn.shape[0], 1), 0) == 0
        prev = jnp.where(first, pv_ref[...], pltpu.roll(hn, 1, 0))
    else:
        prev = pv_ref[...]
    dlt = prev - hn
    mix = lambda j: (hn + dlt * mu_ref[j:j + 1, :]).astype(BF16)
    w_mid = _dot(mix(4), w1_ref[...])
    a_mid = _dot(mix(5), a1_ref[...])
    r = _dot(mix(0), wr_ref[...])
    k = _dot(mix(1), wk_ref[...])
    v = _dot(mix(2), wv_ref[...])
    g = _dot(mix(3), wg_ref[...])
    wl = w0_ref[...] + _dot(jnp.tanh(w_mid).astype(BF16), w2_ref[...])
    w_log = -(jnp.maximum(-wl, 0.0) + jnp.log(1.0 + jnp.exp(-jnp.abs(wl)))) - 0.5
    a = _sigmoid(a0_ref[...] + _dot(a_mid.astype(BF16), a2_ref[...]))
    kk = k * kk_ref[...]
    kk = kk / jnp.maximum(jnp.sqrt(_head_sum(kk * kk, bd_ref[...])), 1e-12)
    k_mod = k * (1.0 + (a - 1.0) * ka_ref[...])
    put(r_ref, r)
    put(w_ref, jnp.exp(-jnp.exp(w_log)))
    put(k_ref, k_mod)
    put(v_ref, v)
    put(na_ref, -kk)
    put(b_ref, kk * a)
    bonus_ref[...] = _head_sum(r * k_mod * rk_ref[...], bd_ref[...]) * v
    g_ref[...] = _silu(g)


def _rwkv_pre(hn, prev, mu, wr, wk, wv, wg, w0, w1, w2, a0, a1, a2, kk, ka, rk, bd, *, batch, time_minor):
    n, d = hn.shape
    tm = min(ROW_TILE, n)
    seq = n // batch
    nt = seq // tm if time_minor else 1
    row = pl.BlockSpec((tm, d), lambda i: (i, 0))
    ws = [mu, wr, wk, wv, wg, w0, w1, w2, a0, a1, a2, kk, ka, rk, bd]
    if time_minor:
        scan_spec = pl.BlockSpec((None, tm // LANES, d, LANES), lambda i: (i // nt, i % nt, 0, 0))
        scan_shape = jax.ShapeDtypeStruct((batch, seq // LANES, d, LANES), F32)
        prev_spec = pl.BlockSpec((None, 1, d), lambda i: (i, 0, 0))
    else:
        scan_spec, scan_shape, prev_spec = row, jax.ShapeDtypeStruct((n, d), F32), row
    return pl.pallas_call(
        functools.partial(_rwkv_pre_kernel, time_minor=time_minor),
        grid=(n // tm,),
        in_specs=[row, prev_spec] + [_full(w.shape) for w in ws],
        out_specs=(scan_spec,) * 6 + (row, row),
        out_shape=(scan_shape,) * 6 + (jax.ShapeDtypeStruct((n, d), F32),) * 2,
        compiler_params=_cparams(("parallel",)), name="rwkv_pre",
    )(hn, prev, *ws)


def _retile_kernel(*refs):
    n = len(refs) // 2
    for src, dst in zip(refs[:n], refs[n:]):
        kb = src.shape[-2]
        src2 = src.reshape(LANES * kb, LANES)
        dst2 = dst.reshape(LANES * kb, LANES)
        for j in range(kb):
            dst2[pl.ds(j, LANES, stride=kb), :] = src2[pl.ds(j, LANES, stride=kb), :].T


def _retile(arrays, to_scan, batch=None):
    kb = SUBLANES
    n = len(arrays)
    if to_scan:
        nb, tb, nh, k, _ = arrays[0].shape
        out_shape = jax.ShapeDtypeStruct((tb * LANES, k, LANES), F32)
    else:
        t, k, _ = arrays[0].shape
        nb, tb, nh = batch, t // LANES, LANES // batch
        out_shape = jax.ShapeDtypeStruct((nb, tb, nh, k, LANES), F32)
    tiled = pl.BlockSpec((nb, None, nh, kb, LANES), lambda j, i: (0, i, 0, j, 0))
    scan = pl.BlockSpec((LANES, kb, LANES), lambda j, i: (i, j, 0))
    return pl.pallas_call(
        _retile_kernel,
        grid=(k // kb, tb),
        in_specs=[tiled if to_scan else scan] * n,
        out_specs=(scan if to_scan else tiled,) * n,
        out_shape=(out_shape,) * n,
        compiler_params=_cparams(("parallel", "parallel")), name="retile",
    )(*arrays)


def _wkv_first_sa(s_ref, a_ref):
    sa = s_ref[0] * a_ref[0, 0:1, :]
    for k in range(1, s_ref.shape[0]):
        sa = sa + s_ref[k] * a_ref[0, k:k + 1, :]
    return sa


def _wkv_step(t, sa, r_ref, w_ref, k_ref, v_ref, a_ref, b_ref, y_ref, s_ref):
    tt = r_ref.shape[0]
    tn = jnp.minimum(t + 1, tt - 1)
    row = lambda ref, k: ref[t, k:k + 1, :]
    v = v_ref[t]
    y = jnp.zeros_like(v)
    sa_next = jnp.zeros_like(v)
    for k in range(s_ref.shape[0]):
        s = s_ref[k] * row(w_ref, k) + sa * row(b_ref, k) + v * row(k_ref, k)
        s_ref[k] = s
        y = y + s * row(r_ref, k)
        sa_next = sa_next + s * a_ref[tn, k:k + 1, :]
    y_ref[t] = y
    return sa_next


def _wkv_scan_kernel(r_ref, w_ref, k_ref, v_ref, a_ref, b_ref, s0_ref, y_ref, so_ref, s_ref, *, value_major):
    @pl.when(pl.program_id(1) == 0)
    def _():
        s_ref[...] = jnp.swapaxes(s0_ref[...], 0, 1) if value_major else s0_ref[...]

    step = functools.partial(_wkv_step, r_ref=r_ref, w_ref=w_ref, k_ref=k_ref, v_ref=v_ref, a_ref=a_ref,
                             b_ref=b_ref, y_ref=y_ref, s_ref=s_ref)
    lax.fori_loop(0, r_ref.shape[0], step, _wkv_first_sa(s_ref, a_ref))

    @pl.when(pl.program_id(1) == pl.num_programs(1) - 1)
    def _():
        so_ref[...] = jnp.swapaxes(s_ref[...], 0, 1) if value_major else s_ref[...]


def _wkv_scan(r, w, k, v, a, b, s0, *, value_major):
    t, n, lanes = r.shape
    tt = min(SCAN_TIME_BLOCK, t)
    seq = pl.BlockSpec((tt, n, LANES), lambda g, i: (i, 0, g))
    if value_major:
        nb = s0.shape[3] // LANES
        st = pl.BlockSpec((None, n, n, LANES), lambda g, i: (g // nb, 0, 0, g % nb))
    else:
        st = pl.BlockSpec((n, n, LANES), lambda g, i: (0, 0, g))
    return pl.pallas_call(
        functools.partial(_wkv_scan_kernel, value_major=value_major),
        grid=(lanes // LANES, t // tt),
        in_specs=[seq] * 6 + [st],
        out_specs=(seq, st),
        out_shape=(jax.ShapeDtypeStruct((t, n, lanes), F32), jax.ShapeDtypeStruct(s0.shape, F32)),
        scratch_shapes=[pltpu.VMEM((n, n, LANES), F32)],
        compiler_params=_cparams(("parallel", "arbitrary")), name="wkv_scan",
    )(r, w, k, v, a, b, s0)


def _rwkv_post_kernel(y_ref, bonus_ref, g_ref, h_ref, p_ref, lnw_ref, lnb_ref, bd_ref, wo_ref, png_ref,
                      pgw_ref, ppw_ref, fg_ref, o_ref, *, time_minor):
    bd = bd_ref[...]
    emb = _dot(p_ref[...].astype(BF16), ppw_ref[...])
    if time_minor:
        y = jnp.concatenate([y_ref[j] for j in range(y_ref.shape[0])], axis=1).T
    else:
        y = y_ref[...]
    inv_n = 1.0 / RW_N
    yc = y - _head_sum(y, bd) * inv_n
    var = _head_sum(yc * yc, bd) * inv_n
    yn = yc * lax.rsqrt(var + RW_EPS) * lnw_ref[...] + lnb_ref[...]
    mix = ((yn + bonus_ref[...]) * g_ref[...]).astype(BF16)
    h = h_ref[...] + _dot(mix, wo_ref[...])
    h = _ple(h, emb, png_ref[...], pgw_ref[...])
    o_ref[...] = _rms(h, fg_ref[...])


def _rwkv_post(y, bonus, g, h, p3d, layer, lnw, lnb, bd, wo, png, pgw, ppw, fg, *, time_minor):
    n, d = h.shape
    tm = min(ROW_TILE, n)
    row = lambda a: pl.BlockSpec((tm, a.shape[1]), lambda i: (i, 0))
    if time_minor:
        nt = y.shape[1] * LANES // tm
        y_spec = pl.BlockSpec((None, tm // LANES, d, LANES), lambda i: (i // nt, i % nt, 0, 0))
    else:
        y_spec = row(y)
    ws = [lnw, lnb, bd, wo, png, pgw, ppw, fg]
    return pl.pallas_call(
        functools.partial(_rwkv_post_kernel, time_minor=time_minor),
        grid=(n // tm,),
        in_specs=[y_spec, row(bonus), row(g), row(h), _layer_rows(p3d, layer, tm)] + [_full(w.shape) for w in ws],
        out_specs=row(h),
        out_shape=jax.ShapeDtypeStruct((n, d), F32),
        compiler_params=_cparams(("parallel",)), name="rwkv_post",
    )(y, bonus, g, h, p3d, *ws)


def _rope_tables(pos, rows):
    half = ROPE_DIM // 2
    inv = ROPE_THETA ** (-jnp.arange(half, dtype=F32) / half)
    ang = pos.astype(F32)[:, None] * inv[None, :]
    cos = jnp.cos(ang)
    sin = jnp.sin(ang)
    cos = jnp.tile(jnp.concatenate([cos, cos], axis=-1), (rows // pos.shape[0], MLA_HEADS))
    sin = jnp.tile(jnp.concatenate([-sin, sin], axis=-1), (rows // pos.shape[0], MLA_HEADS))
    return cos, sin


def _swap_halves(w):
    half = w.shape[-1] // 2
    return jnp.concatenate([w[..., half:], w[..., :half]], axis=-1)


def _to_lanes(x, batch, seq, batch_minor):
    heads = x.shape[1] // RW_N
    order = (1, 3, 2, 0) if batch_minor else (1, 3, 0, 2)
    return x.reshape(batch, seq, heads, RW_N).transpose(order).reshape(seq, RW_N, batch * heads)


def _from_lanes(y, batch, seq, batch_minor):
    heads = y.shape[2] // batch
    if batch_minor:
        y = y.reshape(seq, RW_N, heads, batch).transpose(3, 0, 2, 1)
    else:
        y = y.reshape(seq, RW_N, batch, heads).transpose(2, 0, 3, 1)
    return y.reshape(batch * seq, heads * RW_N)


def _row(v):
    return v.reshape(1, -1)


def _even_mixers(x, pos, W, hg_s0, paged):
    batch, seq, d = x.shape
    n = batch * seq
    x2d = x.reshape(n, d)
    row = _row
    cos_tab, sin_tab = _rope_tables(pos, max(seq, min(ROW_TILE, n)))
    zhg, qlat, qrope, ckv, kr, ckvb, krb, mg = _even_pre(
        x2d, cos_tab, sin_tab, row(W['mix_norm'][0]), W['hg_lb_logits'], W['w_in_hg'], W['w_in_mla'],
        row(W['mla_q_norm'][0]), row(W['mla_kv_norm'][0]), W['w_uq'], W['w_uk'], lb_rows=1)
    g_hg = row(W['hg_norm'][0])
    c = dict(batch=batch, seq=seq, x2d=x2d, mg=mg, ckv=ckv, kr=kr)
    if paged is None:
        c['ohg'], c['s_hg'] = _hgrn_prompt(zhg, g_hg, batch, seq)
        c['ctx'] = _attn_prompt(qlat, qrope, ckvb, krb, batch, seq)
    else:
        c['ohg'], c['s_hg'] = _hgrn_sample(zhg, hg_s0, g_hg, seq)
        cache_ckv, cache_krope_t, page_table = paged
        stack = lambda q: q.reshape(MLA_HEADS, batch, seq, q.shape[-1]).transpose(1, 0, 2, 3).reshape(
            batch, MLA_HEADS * seq, q.shape[-1])
        pad = lambda a: jnp.pad(a.reshape(batch, seq, a.shape[-1]), ((0, 0), (0, 2 * SUBLANES - seq), (0, 0)))
        c['attn_args'] = (page_table, stack(qlat), stack(qrope), pad(ckvb), pad(krb).transpose(0, 2, 1),
                          cache_ckv, cache_krope_t, seq)
    return c


def _stacked_ctx_to_rows(ctx, batch, seq):
    return ctx.reshape(batch, MLA_HEADS, seq, KV_LORA).transpose(0, 2, 1, 3).reshape(batch * seq, MLA_HEADS * KV_LORA)


def _rwkv_operands(c, p, W, wkv_s0, shift_s0):
    batch, seq, x2d = c['batch'], c['seq'], c['x2d']
    n, d = x2d.shape
    row = _row
    p3d = p.reshape(p.shape[0], n, p.shape[-1])
    h1, hn1, tile_last = _even_post(
        x2d, c['ohg'], c['ctx'], c['mg'], p3d, 0, W['w_uv_bd'], W['w_out'], row(W['ple_norm'][0]),
        W['ple_gate_b'][0], W['ple_proj_b'][0], row(W['mix_norm'][1]))
    hn3 = hn1.reshape(batch, seq, d)
    heads = d // RW_N
    batch_lanes = batch % LANES == 0
    time_minor = seq % LANES == 0 and batch * heads == LANES
    if time_minor:
        tile_last = tile_last.reshape(batch, -1, d)
        prev = jnp.concatenate([shift_s0[:, None, :], tile_last[:, :-1]], axis=1).reshape(-1, 1, d)
    else:
        prev = jnp.concatenate([shift_s0[:, None, :], hn3[:, :-1]], axis=1).reshape(n, d)
    *scan_in, bonus, g = _rwkv_pre(
        hn1, prev, W['rw_mu'][0], W['w_r'], W['w_k'], W['w_v'], W['w_g'], row(W['rw_w0'][0]), W['w_w1'],
        W['w_w2'], row(W['rw_a0'][0]), W['w_a1'], W['w_a2'], row(W['rw_k_k'][0]), row(W['rw_k_a'][0]),
        row(W['rw_r_k'][0]), W['head_bd'], batch=batch, time_minor=time_minor)
    if time_minor:
        scan_in = _retile([a.reshape(batch, seq // LANES, heads, RW_N, LANES) for a in scan_in], True)
    else:
        scan_in = [_to_lanes(a, batch, seq, batch_lanes) for a in scan_in]
    if batch_lanes:
        scan_s0 = wkv_s0.transpose(1, 2, 3, 0)
    else:
        scan_s0 = wkv_s0.transpose(3, 2, 0, 1).reshape(RW_N, RW_N, -1)
    c.update(scan_in=scan_in, scan_s0=scan_s0, batch_lanes=batch_lanes, time_minor=time_minor, bonus=bonus, g=g,
             h1=h1, hn3=hn3, p3d=p3d)
    return c


def _group_outputs(c, y, s_wkv, W):
    batch, seq, h1 = c['batch'], c['seq'], c['h1']
    d = h1.shape[1]
    heads = d // RW_N
    row = _row
    time_minor, batch_lanes = c['time_minor'], c['batch_lanes']
    if batch_lanes:
        s_wkv = s_wkv.transpose(3, 0, 1, 2)
    else:
        s_wkv = s_wkv.reshape(RW_N, RW_N, batch, heads).transpose(2, 3, 1, 0)
    if time_minor:
        y = _retile([y], False, batch)[0].reshape(batch, seq // LANES, d, LANES)
    else:
        y = _from_lanes(y, batch, seq, batch_lanes)
    out = _rwkv_post(
        y, c['bonus'], c['g'], h1, c['p3d'], 1, row(W['rw_ln_w'][0]), row(W['rw_ln_b'][0]), W['head_bd'],
        W['w_o'], row(W['ple_norm'][1]), W['ple_gate_b'][1], W['ple_proj_b'][1], row(W['final_norm']),
        time_minor=time_minor)
    return (out.reshape(batch, seq, d), c['ckv'].reshape(1, batch, seq, KV_LORA),
            c['kr'].reshape(1, batch, seq, ROPE_DIM), c['s_hg'][None], s_wkv[None], c['hn3'][:, -1][None])


def kernel(x_prompt, x_sample, cache_ckv, cache_krope, state_hgrn, state_wkv, state_shift, page_table, p_prompt, p_sample, mix_norm, ev_w_in, hg_lb_logits, hg_norm, mla_q_norm, mla_w_uq, mla_kv_norm, mla_w_uk, mla_w_uv, ev_w_out, rw_mu, rw_w_rkvg, rw_w0, rw_w1, rw_w2, rw_a0, rw_a1, rw_a2, rw_k_k, rw_k_a, rw_r_k, rw_ln_w, rw_ln_b, rw_w_o, ple_norm, ple_gate, ple_proj, final_norm):
    bf = lambda a: a.astype(BF16)
    d = x_prompt.shape[-1]
    w_in = ev_w_in[0]
    o = np.cumsum([0, HG_WIDTH, HG_WIDTH, HG_HEADS * HG_DV, HG_HEADS * HG_DV, Q_LORA, KV_LORA, ROPE_DIM, MLA_WIDTH])
    cq, ckv_w, kr_w, mg_w = (w_in[:, o[4]:o[5]], w_in[:, o[5]:o[6]], w_in[:, o[6]:o[7]], w_in[:, o[7]:o[8]])
    uq = mla_w_uq[0].reshape(Q_LORA, MLA_HEADS, NOPE_DIM + ROPE_DIM)
    uq_rope = uq[:, :, NOPE_DIM:]
    uv_bd = jnp.zeros((MLA_HEADS, KV_LORA, MLA_HEADS, V_DIM), F32)
    uv_bd = uv_bd.at[jnp.arange(MLA_HEADS), :, jnp.arange(MLA_HEADS), :].set(mla_w_uv[0].transpose(0, 2, 1))
    hid = np.arange(MXU_DIM) // RW_N
    W = dict(
        mix_norm=mix_norm, hg_lb_logits=hg_lb_logits, hg_norm=hg_norm, mla_q_norm=mla_q_norm,
        mla_kv_norm=mla_kv_norm, ple_norm=ple_norm, final_norm=final_norm, rw_mu=rw_mu, rw_w0=rw_w0, rw_a0=rw_a0,
        rw_k_k=rw_k_k, rw_k_a=rw_k_a, rw_ln_w=rw_ln_w, rw_ln_b=rw_ln_b, rw_r_k=rw_r_k.reshape(rw_r_k.shape[0], -1),
        w_in_hg=bf(w_in[:, :o[4]]),
        w_in_mla=bf(jnp.concatenate([cq, ckv_w, mg_w, kr_w, _swap_halves(kr_w)], axis=-1)),
        w_uq=bf(jnp.concatenate([uq[:, :, :NOPE_DIM].reshape(Q_LORA, -1), uq_rope.reshape(Q_LORA, -1),
                                 _swap_halves(uq_rope).reshape(Q_LORA, -1)], axis=-1)),
        w_uk=bf(mla_w_uk[0]),
        w_uv_bd=bf(uv_bd.reshape(MLA_HEADS * KV_LORA, MLA_WIDTH)),
        w_out=bf(ev_w_out[0]),
        ple_gate_b=bf(ple_gate), ple_proj_b=bf(ple_proj),
        w_r=bf(rw_w_rkvg[0, 0]), w_k=bf(rw_w_rkvg[0, 1]), w_v=bf(rw_w_rkvg[0, 2]), w_g=bf(rw_w_rkvg[0, 3]),
        w_w1=bf(rw_w1[0]), w_w2=bf(rw_w2[0]), w_a1=bf(rw_a1[0]), w_a2=bf(rw_a2[0]), w_o=bf(rw_w_o[0]),
        head_bd=jnp.asarray(hid[:, None] == hid[None, :], BF16),
    )
    bp, tp, _ = x_prompt.shape
    bs, ts, _ = x_sample.shape
    past_len = page_table.shape[1] * cache_ckv.shape[2]
    heads = d // RW_N
    paged = (cache_ckv.reshape(cache_ckv.shape[1:]), jnp.swapaxes(cache_krope.reshape(cache_krope.shape[1:]), 1, 2),
             page_table)
    cp = _even_mixers(x_prompt, jnp.arange(tp), W, None, None)
    cp = _rwkv_operands(cp, p_prompt, W, jnp.zeros((bp, heads, RW_N, RW_N), F32), jnp.zeros((bp, d), F32))
    cs = _even_mixers(x_sample, past_len + jnp.arange(ts), W, state_hgrn[0], paged)
    y_p, wkv_p = _wkv_scan(*cp['scan_in'], cp['scan_s0'], value_major=cp['batch_lanes'])
    yp, ckv_p, kr_p, hg_p, wkv_p, sh_p = _group_outputs(cp, y_p, wkv_p, W)
    cs['ctx'] = _stacked_ctx_to_rows(_attn_sample(*cs['attn_args']), bs, ts)
    cs = _rwkv_operands(cs, p_sample, W, state_wkv[0], state_shift[0])
    y_s, wkv_s = _wkv_scan(*cs['scan_in'], cs['scan_s0'], value_major=cs['batch_lanes'])
    ys, ckv_s, kr_s, hg_s, wkv_s, sh_s = _group_outputs(cs, y_s, wkv_s, W)
    return (yp, ys, ckv_p, kr_p, ckv_s, kr_s, hg_p, hg_s, wkv_p, wkv_s, sh_p, sh_s)
```

```python
import functools

import jax
import jax.numpy as jnp
import numpy as np
from jax import lax
from jax.experimental import pallas as pl
from jax.experimental.pallas import tpu as pltpu

F32 = jnp.float32
BF16 = jnp.bfloat16

NORM_EPS = 1e-6
HG_HEADS = 4
HG_DK = 128
HG_DV = 128
HG_WIDTH = HG_HEADS * HG_DK
MLA_HEADS = 8
Q_LORA = 384
KV_LORA = 256
NOPE_DIM = 64
ROPE_DIM = 32
V_DIM = 64
MLA_WIDTH = MLA_HEADS * V_DIM
MLA_SCALE = (NOPE_DIM + ROPE_DIM) ** -0.5
ROPE_THETA = 10000.0
RW_N = 64
RW_EPS = 64e-5

LANES = 128
SUBLANES = 8
MXU_DIM = 256
VMEM_LIMIT_BYTES = 56 * 1024 * 1024

ROW_TILE = 256
HG_CHUNK = 64
HG_BLOCK = SUBLANES
HG_TIME_BLOCK = 1024
HG_SEQS_PER_STEP = 2
ATT_BLOCK = 256
PAGES_PER_STEP = 32
PAGES_PER_BLOCK = 8
PAGE_RING = 4
SCAN_TIME_BLOCK = 64
RETILE_TIME_BLOCKS = 2


def _cparams(sem):
    return pltpu.CompilerParams(dimension_semantics=sem, vmem_limit_bytes=VMEM_LIMIT_BYTES)


def _rms(x, g):
    return x * lax.rsqrt(jnp.mean(x * x, axis=-1, keepdims=True) + NORM_EPS) * g


def _sigmoid(x):
    return 1.0 / (1.0 + jnp.exp(-x))


def _silu(x):
    return x * _sigmoid(x)


def _dot(a, b):
    return jnp.dot(a, b, preferred_element_type=F32)


def _dot_nt(a, b):
    return lax.dot_general(a, b, (((1,), (1,)), ((), ())), preferred_element_type=F32)


def _head_sum(x, bd):
    hi = x.astype(BF16)
    lo = (x - hi.astype(F32)).astype(BF16)
    outs = []
    for c in range(x.shape[-1] // MXU_DIM):
        sl = slice(c * MXU_DIM, (c + 1) * MXU_DIM)
        outs.append(_dot(hi[:, sl], bd) + _dot(lo[:, sl], bd))
    return jnp.concatenate(outs, axis=-1)


def _full(shape):
    nd = len(shape)
    return pl.BlockSpec(shape, lambda *_: (0,) * nd)


def _even_pre_kernel(x_ref, cos_ref, sin_ref, g_ref, lbl_ref, w1_ref, w2_ref, qg_ref, kvg_ref, wuq_ref,
                     wuk_ref, zhg_ref, qlat_ref, qrope_ref, ckv_ref, kr_ref, ckvb_ref, krb_ref, mg_ref,
                     *, lb_rows):
    hn = _rms(x_ref[...], g_ref[...]).astype(BF16)
    z2 = _dot(hn, w2_ref[...])
    z1 = _dot(hn, w1_ref[...])
    lg = lbl_ref[...]
    e = jnp.exp(lg - jnp.max(lg, axis=0, keepdims=True))
    p = e / jnp.sum(e, axis=0, keepdims=True)
    lb = jnp.sum(p[:lb_rows], axis=0, keepdims=True)
    W = HG_WIDTH
    f = lb + (1.0 - lb) * _sigmoid(z1[:, W:2 * W])
    zhg_ref[:, 0:W] = _silu(z1[:, 0:W])
    zhg_ref[:, W:2 * W] = jnp.log(f)
    zhg_ref[:, 2 * W:3 * W] = z1[:, 2 * W:3 * W]
    zhg_ref[:, 3 * W:4 * W] = _silu(z1[:, 3 * W:4 * W])
    o_kv = Q_LORA
    o_mg = o_kv + KV_LORA
    o_kr = o_mg + MLA_WIDTH
    o_krs = o_kr + ROPE_DIM
    cqn = _rms(z2[:, 0:o_kv], qg_ref[...]).astype(BF16)
    qf = _dot(cqn, wuq_ref[...])
    cos = cos_ref[...]
    sin = sin_ref[...]
    n_nope = MLA_HEADS * NOPE_DIM
    n_rope = MLA_HEADS * ROPE_DIM
    qr = (qf[:, n_nope:n_nope + n_rope] * cos + qf[:, n_nope + n_rope:n_nope + 2 * n_rope] * sin) * MLA_SCALE
    for h in range(MLA_HEADS):
        qrope_ref[h] = qr[:, h * ROPE_DIM:(h + 1) * ROPE_DIM].astype(BF16)
        qn = qf[:, h * NOPE_DIM:(h + 1) * NOPE_DIM].astype(BF16)
        qlat_ref[h] = (_dot(qn, wuk_ref[h]) * MLA_SCALE).astype(BF16)
    ckv = _rms(z2[:, o_kv:o_mg], kvg_ref[...])
    ckv_ref[...] = ckv
    ckvb_ref[...] = ckv.astype(BF16)
    kr = z2[:, o_kr:o_krs] * cos[:, :ROPE_DIM] + z2[:, o_krs:o_krs + ROPE_DIM] * sin[:, :ROPE_DIM]
    kr_ref[...] = kr
    krb_ref[...] = kr.astype(BF16)
    mg_ref[...] = _silu(z2[:, o_mg:o_kr]).astype(BF16)


def _even_pre(x2d, cos_tab, sin_tab, g, lb_logits, w1, w2, qg, kvg, wuq, wuk, *, lb_rows):
    n, d = x2d.shape
    tm = min(ROW_TILE, n)
    n_tab = cos_tab.shape[0] // tm
    row = lambda i: (i, 0)
    tab = lambda i: (i % n_tab, 0)
    out_shape = (
        jax.ShapeDtypeStruct((n, 4 * HG_WIDTH), F32),
        jax.ShapeDtypeStruct((MLA_HEADS, n, KV_LORA), BF16),
        jax.ShapeDtypeStruct((MLA_HEADS, n, ROPE_DIM), BF16),
        jax.ShapeDtypeStruct((n, KV_LORA), F32),
        jax.ShapeDtypeStruct((n, ROPE_DIM), F32),
        jax.ShapeDtypeStruct((n, KV_LORA), BF16),
        jax.ShapeDtypeStruct((n, ROPE_DIM), BF16),
        jax.ShapeDtypeStruct((n, MLA_WIDTH), BF16),
    )
    out_specs = (
        pl.BlockSpec((tm, 4 * HG_WIDTH), row),
        pl.BlockSpec((MLA_HEADS, tm, KV_LORA), lambda i: (0, i, 0)),
        pl.BlockSpec((MLA_HEADS, tm, ROPE_DIM), lambda i: (0, i, 0)),
        pl.BlockSpec((tm, KV_LORA), row),
        pl.BlockSpec((tm, ROPE_DIM), row),
        pl.BlockSpec((tm, KV_LORA), row),
        pl.BlockSpec((tm, ROPE_DIM), row),
        pl.BlockSpec((tm, MLA_WIDTH), row),
    )
    in_specs = [
        pl.BlockSpec((tm, d), row),
        pl.BlockSpec((tm, cos_tab.shape[1]), tab),
        pl.BlockSpec((tm, sin_tab.shape[1]), tab),
        _full(g.shape), _full(lb_logits.shape), _full(w1.shape), _full(w2.shape), _full(qg.shape),
        _full(kvg.shape), _full(wuq.shape), _full(wuk.shape),
    ]
    return pl.pallas_call(
        functools.partial(_even_pre_kernel, lb_rows=lb_rows),
        grid=(n // tm,), in_specs=in_specs, out_specs=out_specs, out_shape=out_shape,
        compiler_params=_cparams(("parallel",)), name="even_pre",
    )(x2d, cos_tab, sin_tab, g, lb_logits, w1, w2, qg, kvg, wuq, wuk)


def _tril_ones(n, block):
    r = lax.broadcasted_iota(jnp.int32, (n, n), 0)
    c = lax.broadcasted_iota(jnp.int32, (n, n), 1)
    return ((r >= c) & ((r // block) == (c // block))).astype(F32)


def _hgrn_exact_blocks(q, k, v, b, block):
    rows = q.shape[0]
    rid = lax.broadcasted_iota(jnp.int32, (rows, 1), 0) % block
    o = jnp.sum(q * k, axis=-1, keepdims=True) * v
    for d in range(1, block):
        kd = pltpu.roll(k, d, 0)
        bd = pltpu.roll(b, d, 0)
        vd = pltpu.roll(v, d, 0)
        w = jnp.sum(q * kd * jnp.exp(jnp.minimum(b - bd, 0.0)), axis=-1, keepdims=True)
        o = o + jnp.where(rid >= d, w, 0.0) * vd
    return o


def _hgrn_finish(o, g, gate):
    return (o * lax.rsqrt(jnp.mean(o * o, axis=-1, keepdims=True) + NORM_EPS) * g * gate).astype(BF16)


def _hgrn_prompt_kernel(q_ref, lf_ref, v_ref, gt_ref, g_ref, o_ref, s_ref, st_ref):
    C = HG_CHUNK
    nseq, tb, _ = q_ref.shape
    tril = _tril_ones(C, C)
    rid = lax.broadcasted_iota(jnp.int32, (C, 1), 0)
    rr = lax.broadcasted_iota(jnp.int32, (C, C), 0)
    cc = lax.broadcasted_iota(jnp.int32, (C, C), 1)
    g = g_ref[...]

    @pl.when(pl.program_id(1) == 0)
    def _():
        st_ref[...] = jnp.zeros(st_ref.shape, F32)

    def decay(x):
        x['k'] = 1.0 - jnp.exp(x['lf'])
        x['b'] = jnp.dot(tril, x['lf'], precision=lax.Precision.HIGHEST, preferred_element_type=F32)

    def scores(x):
        q, k, b = x['q'], x['k'], x['b']
        x['o'] = _dot_nt((q * jnp.exp(b)).astype(BF16), x['st'].astype(BF16))
        x['att'] = []
        m = HG_BLOCK
        while 2 * m <= C:
            nb = C // (2 * m)
            b3 = b.reshape(nb, 2 * m, HG_DK)
            ref = jnp.broadcast_to(b3[:, m - 1:m, :], (nb, 2 * m, HG_DK)).reshape(C, HG_DK)
            upper = (rid % (2 * m)) >= m
            qt = q * jnp.where(upper, jnp.exp(jnp.minimum(b - ref, 0.0)), 0.0)
            kt = k * jnp.where(upper, 0.0, jnp.exp(jnp.minimum(ref - b, 0.0)))
            x['att'].append((m, _dot_nt(qt.astype(BF16), kt.astype(BF16))))
            m *= 2
        bend = b[C - 1:C, :]
        x['st'] = x['st'] * jnp.exp(bend) + _dot(x['v'].T.astype(BF16), (k * jnp.exp(bend - b)).astype(BF16))
        x['o'] = x['o'] + _hgrn_exact_blocks(q, k, x['v'], b, HG_BLOCK)

    def values(x):
        att = jnp.zeros((C, C), F32)
        for m, a in x['att']:
            if 2 * m < C:
                a = jnp.where((rr // (2 * m)) == (cc // (2 * m)), a, 0.0)
            att = att + a
        x['pv'] = _dot(att.astype(BF16), x['v'].astype(BF16))

    def chunk(c, carry):
        sl = pl.ds(pl.multiple_of(c * C, C), C)
        xs = [dict(s=s, h=h, hs=slice(h * HG_DK, (h + 1) * HG_DK)) for s in range(nseq) for h in range(HG_HEADS)]
        for x in xs:
            s, h, hs = x['s'], x['h'], x['hs']
            x.update(st=st_ref[s, h], q=q_ref[s, sl, hs], lf=lf_ref[s, sl, hs], v=v_ref[s, sl, hs],
                     gate=gt_ref[s, sl, hs])
        for stage in (decay, scores, values):
            for x in xs:
                stage(x)
        for x in xs:
            o_ref[x['s'], sl, x['hs']] = _hgrn_finish(x['o'] + x['pv'], g, x['gate'])
            st_ref[x['s'], x['h']] = x['st']
        return carry

    lax.fori_loop(0, tb // C, chunk, 0)

    @pl.when(pl.program_id(1) == pl.num_programs(1) - 1)
    def _():
        for s in range(nseq):
            for h in range(HG_HEADS):
                s_ref[s, h] = st_ref[s, h].T


def _hgrn_prompt(zhg, g, batch, seq):
    nseq = HG_SEQS_PER_STEP if batch % HG_SEQS_PER_STEP == 0 else 1
    tb = min(HG_TIME_BLOCK // nseq, seq)
    z3 = zhg.reshape(batch, seq, 4 * HG_WIDTH)
    blk = lambda off: pl.BlockSpec((nseq, tb, HG_WIDTH), lambda b, i: (b, i, off))
    ohg, s_hg = pl.pallas_call(
        _hgrn_prompt_kernel,
        grid=(batch // nseq, seq // tb),
        in_specs=[blk(0), blk(1), blk(2), blk(3), _full(g.shape)],
        out_specs=(pl.BlockSpec((nseq, tb, HG_HEADS * HG_DV), lambda b, i: (b, i, 0)),
                   pl.BlockSpec((nseq, HG_HEADS, HG_DK, HG_DV), lambda b, i: (b, 0, 0, 0))),
        out_shape=(jax.ShapeDtypeStruct((batch, seq, HG_HEADS * HG_DV), BF16),
                   jax.ShapeDtypeStruct((batch, HG_HEADS, HG_DK, HG_DV), F32)),
        scratch_shapes=[pltpu.VMEM((nseq, HG_HEADS, HG_DV, HG_DK), F32)],
        compiler_params=_cparams(("parallel", "arbitrary")), name="hgrn_prompt",
    )(z3, z3, z3, z3, g)
    return ohg.reshape(batch * seq, HG_HEADS * HG_DV), s_hg


def _hgrn_sample_kernel(z_ref, s0_ref, g_ref, o_ref, s_ref, *, seq):
    rows = z_ref.shape[0]
    nb = rows // seq
    tril = _tril_ones(rows, seq)
    rb = lax.broadcasted_iota(jnp.int32, (rows, 1), 0) // seq
    cb = lax.broadcasted_iota(jnp.int32, (1, rows), 1) // seq
    g = g_ref[...]
    W = HG_WIDTH
    for h in range(HG_HEADS):
        hs = slice(h * HG_DK, (h + 1) * HG_DK)
        q = z_ref[:, hs]
        lf = z_ref[:, W + h * HG_DK:W + (h + 1) * HG_DK]
        v = z_ref[:, 2 * W + h * HG_DV:2 * W + (h + 1) * HG_DV]
        gate = z_ref[:, 3 * W + h * HG_DV:3 * W + (h + 1) * HG_DV]
        k = 1.0 - jnp.exp(lf)
        b = jnp.dot(tril, lf, precision=lax.Precision.HIGHEST, preferred_element_type=F32)
        o = _hgrn_exact_blocks(q, k, v, b, seq)
        qe = (q * jnp.exp(b)).astype(BF16)
        bt = b.T
        vb = v.astype(BF16)
        for i in range(nb):
            s0 = s0_ref[i, h]
            o = o + jnp.where(rb == i, _dot(qe, s0.astype(BF16)), 0.0)
            last = i * seq + seq - 1
            bend_row = b[last:last + 1, :]
            kt = jnp.where(rb == i, k * jnp.exp(bend_row - b), 0.0)
            ktt = kt.T.astype(BF16)
            s_ref[i, h] = s0 * jnp.exp(bt[:, last:last + 1]) + _dot(ktt, vb)
        o_ref[:, hs] = _hgrn_finish(o, g, gate)


def _hgrn_sample(zhg, s0, g, seq):
    n = zhg.shape[0]
    nb = 8
    rows = nb * seq
    return pl.pallas_call(
        functools.partial(_hgrn_sample_kernel, seq=seq),
        grid=(n // rows,),
        in_specs=[pl.BlockSpec((rows, 4 * HG_WIDTH), lambda i: (i, 0)),
                  pl.BlockSpec((nb, HG_HEADS, HG_DK, HG_DV), lambda i: (i, 0, 0, 0)),
                  _full(g.shape)],
        out_specs=(pl.BlockSpec((rows, HG_HEADS * HG_DV), lambda i: (i, 0)),
                   pl.BlockSpec((nb, HG_HEADS, HG_DK, HG_DV), lambda i: (i, 0, 0, 0))),
        out_shape=(jax.ShapeDtypeStruct((n, HG_HEADS * HG_DV), BF16),
                   jax.ShapeDtypeStruct(s0.shape, F32)),
        compiler_params=_cparams(("parallel",)), name="hgrn_sample",
    )(zhg, s0, g)


def _lane_tile(x, width):
    if width <= LANES:
        return x[:, :width]
    return jnp.concatenate([x] * (width // LANES), axis=1)


def _softmax_init(m_ref, l_ref, acc_ref):
    m_ref[...] = jnp.full(m_ref.shape, -jnp.inf, F32)
    l_ref[...] = jnp.zeros(l_ref.shape, F32)
    acc_ref[...] = jnp.zeros(acc_ref.shape, F32)


def _online_step(s, kv, m_old, l_old, acc_old):
    m_new = jnp.maximum(m_old, jnp.max(s, axis=-1, keepdims=True))
    alpha = jnp.exp(m_old - m_new)
    p = jnp.exp(s - _lane_tile(m_new, s.shape[1]))
    l_new = alpha * l_old + jnp.sum(p, axis=-1, keepdims=True)
    acc_new = _lane_tile(alpha, acc_old.shape[1]) * acc_old + _dot(p.astype(BF16), kv)
    return m_new, l_new, acc_new


def _online_update(s, kv, rows, m_ref, l_ref, acc_ref):
    m_ref[rows, :], l_ref[rows, :], acc_ref[rows, :] = _online_step(
        s, kv, m_ref[rows, :], l_ref[rows, :], acc_ref[rows, :])


def _attn_prompt_kernel(ql_ref, qr_ref, kv_ref, kr_ref, o_ref, m_ref, l_ref, acc_ref):
    tq = ql_ref.shape[1]
    i = pl.program_id(1)
    _softmax_init(m_ref, l_ref, acc_ref)
    n_split = 2
    hs = MLA_HEADS // n_split
    mh = hs * tq
    qpos = lax.broadcasted_iota(jnp.int32, (mh, tq), 0) % tq
    kpos = lax.broadcasted_iota(jnp.int32, (mh, tq), 1)

    def block(j, masked):
        sl = pl.ds(pl.multiple_of(j * tq, tq), tq)
        kv = kv_ref[sl, :]
        kr = kr_ref[sl, :]
        scores = []
        for r in range(n_split):
            ql = ql_ref[r * hs:(r + 1) * hs].reshape(mh, KV_LORA)
            qr = qr_ref[r * hs:(r + 1) * hs].reshape(mh, ROPE_DIM)
            s = _dot_nt(ql, kv) + _dot_nt(qr, kr)
            scores.append(jnp.where(kpos <= qpos, s, -jnp.inf) if masked else s)
        for r, s in enumerate(scores):
            _online_update(s, kv, slice(r * mh, (r + 1) * mh), m_ref, l_ref, acc_ref)

    def body(j, carry):
        block(j, False)
        return carry

    lax.fori_loop(0, i, body, 0)
    block(i, True)
    for h in range(MLA_HEADS):
        rows = slice(h * tq, (h + 1) * tq)
        out = acc_ref[rows, :] / _lane_tile(l_ref[rows, :], KV_LORA)
        o_ref[:, h * KV_LORA:(h + 1) * KV_LORA] = out.astype(BF16)


def _attn_prompt(qlat, qrope, ckvb, krb, batch, seq):
    tq = ATT_BLOCK
    nq = seq // tq
    rows = MLA_HEADS * tq
    return pl.pallas_call(
        _attn_prompt_kernel,
        grid=(batch, nq),
        in_specs=[pl.BlockSpec((MLA_HEADS, tq, KV_LORA), lambda b, i: (0, b * nq + i, 0)),
                  pl.BlockSpec((MLA_HEADS, tq, ROPE_DIM), lambda b, i: (0, b * nq + i, 0)),
                  pl.BlockSpec((seq, KV_LORA), lambda b, i: (b, 0)),
                  pl.BlockSpec((seq, ROPE_DIM), lambda b, i: (b, 0))],
        out_specs=pl.BlockSpec((tq, MLA_HEADS * KV_LORA), lambda b, i: (b * nq + i, 0)),
        out_shape=jax.ShapeDtypeStruct((batch * seq, MLA_HEADS * KV_LORA), BF16),
        scratch_shapes=[pltpu.VMEM((rows, LANES), F32), pltpu.VMEM((rows, LANES), F32),
                        pltpu.VMEM((rows, KV_LORA), F32)],
        compiler_params=_cparams(("parallel", "arbitrary")), name="attn_prompt",
    )(qlat, qrope, ckvb, krb)


class _PageRing:
    def __init__(self, pt_ref, ckv_hbm, kr_hbm, kvbuf, krbuf, sems, units_per_seq):
        self.pt_ref, self.ckv_hbm, self.kr_hbm = pt_ref, ckv_hbm, kr_hbm
        self.kvbuf, self.krbuf, self.sems = kvbuf, krbuf, sems
        self.units_per_seq = units_per_seq
        self.pages = krbuf.shape[1]
        self.page = kvbuf.shape[1] // self.pages

    def _copies(self, u):
        slot = u % PAGE_RING
        seq_id = u // self.units_per_seq
        first = (u % self.units_per_seq) * self.pages
        out = []
        for j in range(self.pages):
            pg = self.pt_ref[seq_id, first + j]
            out.append(pltpu.make_async_copy(
                self.ckv_hbm.at[pg], self.kvbuf.at[slot, pl.ds(j * self.page, self.page), :], self.sems.at[0, slot]))
            out.append(pltpu.make_async_copy(self.kr_hbm.at[pg], self.krbuf.at[slot, j], self.sems.at[1, slot]))
        return out

    def prime(self, total):
        for u in range(PAGE_RING - 1):
            @pl.when(u < total)
            def _():
                for c in self._copies(u):
                    c.start()

    def wait(self, u):
        for c in self._copies(u):
            c.wait()

    def refill(self, u, total):
        @pl.when(u + PAGE_RING - 1 < total)
        def _():
            for c in self._copies(u + PAGE_RING - 1):
                c.start()

    def attend(self, u, ql, qr, state):
        slot = u % PAGE_RING
        sub = min(PAGES_PER_BLOCK, self.pages)
        blocks = []
        for c in range(0, self.pages, sub):
            kv = self.kvbuf[slot, c * self.page:(c + sub) * self.page, :].astype(BF16)
            kr = jnp.concatenate([self.krbuf[slot, c + j].astype(BF16) for j in range(sub)], axis=1)
            blocks.append((_dot_nt(ql, kv) + _dot(qr, kr), kv))
        for s, kv in blocks:
            state = _online_step(s, kv, *state)
        return state


def _attend_new_rows(ql, qr, kvn, krn, seq, state):
    sn = _dot_nt(ql, kvn) + _dot(qr, krn)
    qpos = lax.broadcasted_iota(jnp.int32, sn.shape, 0) % seq
    kpos = lax.broadcasted_iota(jnp.int32, sn.shape, 1)
    _, l, acc = _online_step(jnp.where(kpos <= qpos, sn, -jnp.inf), kvn, *state)
    return (acc / _lane_tile(l, KV_LORA)).astype(BF16)


def _attn_sample_kernel(pt_ref, ql_ref, qr_ref, kvn_ref, krn_ref, ckv_hbm, kr_hbm, o_ref,
                        kvbuf, krbuf, sems, m_ref, l_ref, acc_ref, *, seq):
    n_groups = pl.num_programs(1)
    g = pl.program_id(1)
    unit = pl.program_id(0) * n_groups + g
    total = pl.num_programs(0) * n_groups
    ring = _PageRing(pt_ref, ckv_hbm, kr_hbm, kvbuf, krbuf, sems, n_groups)

    @pl.when(unit == 0)
    def _():
        ring.prime(total)

    @pl.when(g == 0)
    def _():
        _softmax_init(m_ref, l_ref, acc_ref)

    ring.wait(unit)
    ql = ql_ref[...]
    qr = qr_ref[...]
    state = ring.attend(unit, ql, qr, (m_ref[...], l_ref[...], acc_ref[...]))
    m_ref[...], l_ref[...], acc_ref[...] = state
    ring.refill(unit, total)

    @pl.when(g == n_groups - 1)
    def _():
        o_ref[...] = _attend_new_rows(ql, qr, kvn_ref[...], krn_ref[...], seq, state)


def _page_ring_scratch(pages, page, rows):
    return [pltpu.VMEM((PAGE_RING, pages * page, KV_LORA), F32),
            pltpu.VMEM((PAGE_RING, pages, ROPE_DIM, page), F32),
            pltpu.SemaphoreType.DMA((2, PAGE_RING)),
            pltpu.VMEM((rows, LANES), F32), pltpu.VMEM((rows, LANES), F32), pltpu.VMEM((rows, KV_LORA), F32)]


def _attn_sample(page_table, qlat, qrope, kv_new, kr_new_t, cache_ckv, cache_krope_t, seq):
    batch, n_pages = page_table.shape
    G = min(PAGES_PER_STEP, n_pages)
    page = cache_ckv.shape[1]
    rows = qlat.shape[1]
    npad = kv_new.shape[1]

    per_b = lambda r, width: pl.BlockSpec((None, r, width), lambda b, g, pt: (b, 0, 0))
    in_hbm = pl.BlockSpec(memory_space=pl.ANY)
    grid_spec = pltpu.PrefetchScalarGridSpec(
        num_scalar_prefetch=1,
        grid=(batch, n_pages // G),
        in_specs=[per_b(rows, KV_LORA), per_b(rows, ROPE_DIM), per_b(npad, KV_LORA), per_b(ROPE_DIM, npad),
                  in_hbm, in_hbm],
        out_specs=per_b(rows, KV_LORA),
        scratch_shapes=_page_ring_scratch(G, page, rows),
    )
    return pl.pallas_call(
        functools.partial(_attn_sample_kernel, seq=seq),
        grid_spec=grid_spec,
        out_shape=jax.ShapeDtypeStruct((batch, rows, KV_LORA), BF16),
        compiler_params=_cparams(("arbitrary", "arbitrary")), name="attn_sample",
    )(page_table, qlat, qrope, kv_new, kr_new_t, cache_ckv, cache_krope_t)


def _ple(h, emb, png, pgw):
    gate = _sigmoid(_dot(_rms(h, png).astype(BF16), pgw))
    return h + gate * emb


def _even_post_kernel(x_ref, ohg_ref, ctx_ref, mg_ref, p_ref, wuv_ref, wout_ref, png_ref, pgw_ref, ppw_ref,
                      ng_ref, h_ref, hn_ref, last_ref):
    n_hg = HG_HEADS * HG_DV
    ctx_up = _dot(ctx_ref[...], wuv_ref[...])
    hg_out = _dot(ohg_ref[...], wout_ref[0:n_hg, :])
    emb = _dot(p_ref[...].astype(BF16), ppw_ref[...])
    o_mla = (ctx_up * mg_ref[...]).astype(BF16)
    h = x_ref[...] + hg_out + _dot(o_mla, wout_ref[n_hg:, :])
    h = _ple(h, emb, png_ref[...], pgw_ref[...])
    h_ref[...] = h
    hn = _rms(h, ng_ref[...])
    hn_ref[...] = hn
    last_ref[...] = hn[-1:, :]


def _layer_rows(p3d, layer, tm):
    return pl.BlockSpec((None, tm, p3d.shape[2]), lambda i: (layer, i, 0))


def _even_post(x2d, ohg, ctx, mg, p3d, layer, wuv, wout, png, pgw, ppw, ng):
    n, d = x2d.shape
    tm = min(ROW_TILE, n)
    row = lambda a: pl.BlockSpec((tm, a.shape[1]), lambda i: (i, 0))
    return pl.pallas_call(
        _even_post_kernel,
        grid=(n // tm,),
        in_specs=[row(x2d), row(ohg), row(ctx), row(mg), _layer_rows(p3d, layer, tm), _full(wuv.shape),
                  _full(wout.shape), _full(png.shape), _full(pgw.shape), _full(ppw.shape), _full(ng.shape)],
        out_specs=(row(x2d), row(x2d), pl.BlockSpec((None, 1, d), lambda i: (i, 0, 0))),
        out_shape=(jax.ShapeDtypeStruct((n, d), F32), jax.ShapeDtypeStruct((n, d), F32),
                   jax.ShapeDtypeStruct((n // tm, 1, d), F32)),
        compiler_params=_cparams(("parallel",)), name="even_post",
    )(x2d, ohg, ctx, mg, p3d, wuv, wout, png, pgw, ppw, ng)


def _rwkv_pre_kernel(hn_ref, pv_ref, mu_ref, wr_ref, wk_ref, wv_ref, wg_ref, w0_ref, w1_ref, w2_ref, a0_ref,
                     a1_ref, a2_ref, kk_ref, ka_ref, rk_ref, bd_ref,
                     r_ref, w_ref, k_ref, v_ref, na_ref, b_ref, bonus_ref, g_ref, *, time_minor):
    def put(ref, x):
        if time_minor:
            xt = x.T
            for j in range(ref.shape[0]):
                ref[j] = xt[:, j * LANES:(j + 1) * LANES]
        else:
            ref[...] = x

    hn = hn_ref[...]
    if time_minor:
        first = lax.broadcasted_iota(jnp.int32, (hn.shape[0], 1), 0) == 0
        prev = jnp.where(first, pv_ref[...], pltpu.roll(hn, 1, 0))
    else:
        prev = pv_ref[...]
    dlt = prev - hn
    mix = lambda j: (hn + dlt * mu_ref[j:j + 1, :]).astype(BF16)
    w_mid = _dot(mix(4), w1_ref[...])
    a_mid = _dot(mix(5), a1_ref[...])
    r = _dot(mix(0), wr_ref[...])
    k = _dot(mix(1), wk_ref[...])
    v = _dot(mix(2), wv_ref[...])
    g = _dot(mix(3), wg_ref[...])
    wl = w0_ref[...] + _dot(jnp.tanh(w_mid).astype(BF16), w2_ref[...])
    w_log = -(jnp.maximum(-wl, 0.0) + jnp.log(1.0 + jnp.exp(-jnp.abs(wl)))) - 0.5
    a = _sigmoid(a0_ref[...] + _dot(a_mid.astype(BF16), a2_ref[...]))
    kk = k * kk_ref[...]
    kk = kk / jnp.maximum(jnp.sqrt(_head_sum(kk * kk, bd_ref[...])), 1e-12)
    k_mod = k * (1.0 + (a - 1.0) * ka_ref[...])
    put(r_ref, r)
    put(w_ref, jnp.exp(-jnp.exp(w_log)))
    put(k_ref, k_mod)
    put(v_ref, v)
    put(na_ref, -kk)
    put(b_ref, kk * a)
    bonus_ref[...] = (_head_sum(r * k_mod * rk_ref[...], bd_ref[...]) * v).astype(BF16)
    g_ref[...] = _silu(g).astype(BF16)


def _rwkv_pre(hn, prev, mu, wr, wk, wv, wg, w0, w1, w2, a0, a1, a2, kk, ka, rk, bd, *, batch, time_minor):
    n, d = hn.shape
    tm = min(ROW_TILE, n)
    seq = n // batch
    nt = seq // tm if time_minor else 1
    row = pl.BlockSpec((tm, d), lambda i: (i, 0))
    ws = [mu, wr, wk, wv, wg, w0, w1, w2, a0, a1, a2, kk, ka, rk, bd]
    if time_minor:
        scan_spec = pl.BlockSpec((tm // LANES, None, d, LANES), lambda i: (i % nt, i // nt, 0, 0))
        scan_shape = jax.ShapeDtypeStruct((seq // LANES, batch, d, LANES), F32)
        prev_spec = pl.BlockSpec((None, 1, d), lambda i: (i, 0, 0))
    else:
        scan_spec, scan_shape, prev_spec = row, jax.ShapeDtypeStruct((n, d), F32), row
    return pl.pallas_call(
        functools.partial(_rwkv_pre_kernel, time_minor=time_minor),
        grid=(n // tm,),
        in_specs=[row, prev_spec] + [_full(w.shape) for w in ws],
        out_specs=(scan_spec,) * 6 + (row, row),
        out_shape=(scan_shape,) * 6 + (jax.ShapeDtypeStruct((n, d), BF16),) * 2,
        compiler_params=_cparams(("parallel",)), name="rwkv_pre",
    )(hn, prev, *ws)


def _retile_kernel(*refs):
    n = len(refs) // 2
    for src, dst in zip(refs[:n], refs[n:]):
        kb = src.shape[-2]
        rows = int(np.prod(src.shape)) // LANES
        src2 = src.reshape(rows, LANES)
        dst2 = dst.reshape(rows, LANES)
        for first in range(0, rows, LANES * kb):
            for j in range(kb):
                sel = pl.ds(first + j, LANES, stride=kb)
                dst2[sel, :] = src2[sel, :].T


def _retile(arrays, to_scan, batch=None):
    kb = SUBLANES
    n = len(arrays)
    if to_scan:
        tb, nb, nh, k, _ = arrays[0].shape
        out_shape = jax.ShapeDtypeStruct((tb * LANES, k, LANES), F32)
    else:
        t, k, _ = arrays[0].shape
        nb, tb, nh = batch, t // LANES, LANES // batch
        out_shape = jax.ShapeDtypeStruct((tb, nb, nh, k, LANES), F32)
    u = RETILE_TIME_BLOCKS if tb % RETILE_TIME_BLOCKS == 0 else 1
    tiled = pl.BlockSpec((u, nb, nh, kb, LANES), lambda j, i: (i, 0, 0, j, 0))
    scan = pl.BlockSpec((u * LANES, kb, LANES), lambda j, i: (i, j, 0))
    return pl.pallas_call(
        _retile_kernel,
        grid=(k // kb, tb // u),
        in_specs=[tiled if to_scan else scan] * n,
        out_specs=(scan if to_scan else tiled,) * n,
        out_shape=(out_shape,) * n,
        compiler_params=_cparams(("parallel", "parallel")), name="retile",
    )(*arrays)


def _wkv_first_sa(s_ref, a_ref):
    sa = s_ref[0] * a_ref[0, 0:1, :]
    for k in range(1, s_ref.shape[0]):
        sa = sa + s_ref[k] * a_ref[0, k:k + 1, :]
    return sa


def _wkv_step(t, sa, r_ref, w_ref, k_ref, v_ref, a_ref, b_ref, y_ref, s_ref):
    tt = r_ref.shape[0]
    tn = jnp.minimum(t + 1, tt - 1)
    row = lambda ref, k: ref[t, k:k + 1, :]
    v = v_ref[t]
    y = jnp.zeros_like(v)
    sa_next = jnp.zeros_like(v)
    for k in range(s_ref.shape[0]):
        s = s_ref[k] * row(w_ref, k) + sa * row(b_ref, k) + v * row(k_ref, k)
        s_ref[k] = s
        y = y + s * row(r_ref, k)
        sa_next = sa_next + s * a_ref[tn, k:k + 1, :]
    y_ref[t] = y
    return sa_next


def _wkv_scan_kernel(r_ref, w_ref, k_ref, v_ref, a_ref, b_ref, s0_ref, y_ref, so_ref, s_ref, *, value_major):
    @pl.when(pl.program_id(1) == 0)
    def _():
        s_ref[...] = jnp.swapaxes(s0_ref[...], 0, 1) if value_major else s0_ref[...]

    step = functools.partial(_wkv_step, r_ref=r_ref, w_ref=w_ref, k_ref=k_ref, v_ref=v_ref, a_ref=a_ref,
                             b_ref=b_ref, y_ref=y_ref, s_ref=s_ref)
    lax.fori_loop(0, r_ref.shape[0], step, _wkv_first_sa(s_ref, a_ref))

    @pl.when(pl.program_id(1) == pl.num_programs(1) - 1)
    def _():
        so_ref[...] = jnp.swapaxes(s_ref[...], 0, 1) if value_major else s_ref[...]


def _wkv_scan(r, w, k, v, a, b, s0, *, value_major):
    t, n, lanes = r.shape
    tt = min(SCAN_TIME_BLOCK, t)
    seq = pl.BlockSpec((tt, n, LANES), lambda g, i: (i, 0, g))
    if value_major:
        nb = s0.shape[3] // LANES
        st = pl.BlockSpec((None, n, n, LANES), lambda g, i: (g // nb, 0, 0, g % nb))
    else:
        st = pl.BlockSpec((n, n, LANES), lambda g, i: (0, 0, g))
    return pl.pallas_call(
        functools.partial(_wkv_scan_kernel, value_major=value_major),
        grid=(lanes // LANES, t // tt),
        in_specs=[seq] * 6 + [st],
        out_specs=(seq, st),
        out_shape=(jax.ShapeDtypeStruct((t, n, lanes), F32), jax.ShapeDtypeStruct(s0.shape, F32)),
        scratch_shapes=[pltpu.VMEM((n, n, LANES), F32)],
        compiler_params=_cparams(("parallel", "arbitrary")), name="wkv_scan",
    )(r, w, k, v, a, b, s0)


def _rwkv_post_kernel(y_ref, bonus_ref, g_ref, h_ref, p_ref, lnw_ref, lnb_ref, bd_ref, wo_ref, png_ref,
                      pgw_ref, ppw_ref, fg_ref, o_ref, *, time_minor):
    bd = bd_ref[...]
    emb = _dot(p_ref[...].astype(BF16), ppw_ref[...])
    if time_minor:
        y = jnp.concatenate([y_ref[j] for j in range(y_ref.shape[0])], axis=1).T
    else:
        y = y_ref[...]
    inv_n = 1.0 / RW_N
    yc = y - _head_sum(y, bd) * inv_n
    var = _head_sum(yc * yc, bd) * inv_n
    yn = yc * lax.rsqrt(var + RW_EPS) * lnw_ref[...] + lnb_ref[...]
    mix = ((yn + bonus_ref[...]) * g_ref[...]).astype(BF16)
    h = h_ref[...] + _dot(mix, wo_ref[...])
    h = _ple(h, emb, png_ref[...], pgw_ref[...])
    o_ref[...] = _rms(h, fg_ref[...])


def _rwkv_post(y, bonus, g, h, p3d, layer, lnw, lnb, bd, wo, png, pgw, ppw, fg, *, time_minor):
    n, d = h.shape
    tm = min(ROW_TILE, n)
    row = lambda a: pl.BlockSpec((tm, a.shape[1]), lambda i: (i, 0))
    if time_minor:
        nt = y.shape[0] * LANES // tm
        y_spec = pl.BlockSpec((tm // LANES, None, d, LANES), lambda i: (i % nt, i // nt, 0, 0))
    else:
        y_spec = row(y)
    ws = [lnw, lnb, bd, wo, png, pgw, ppw, fg]
    return pl.pallas_call(
        functools.partial(_rwkv_post_kernel, time_minor=time_minor),
        grid=(n // tm,),
        in_specs=[y_spec, row(bonus), row(g), row(h), _layer_rows(p3d, layer, tm)] + [_full(w.shape) for w in ws],
        out_specs=row(h),
        out_shape=jax.ShapeDtypeStruct((n, d), F32),
        compiler_params=_cparams(("parallel",)), name="rwkv_post",
    )(y, bonus, g, h, p3d, *ws)


def _rope_tables(pos, rows):
    half = ROPE_DIM // 2
    inv = ROPE_THETA ** (-jnp.arange(half, dtype=F32) / half)
    ang = pos.astype(F32)[:, None] * inv[None, :]
    cos = jnp.cos(ang)
    sin = jnp.sin(ang)
    cos = jnp.tile(jnp.concatenate([cos, cos], axis=-1), (rows // pos.shape[0], MLA_HEADS))
    sin = jnp.tile(jnp.concatenate([-sin, sin], axis=-1), (rows // pos.shape[0], MLA_HEADS))
    return cos, sin


def _swap_halves(w):
    half = w.shape[-1] // 2
    return jnp.concatenate([w[..., half:], w[..., :half]], axis=-1)


def _to_lanes(x, batch, seq, batch_minor):
    heads = x.shape[1] // RW_N
    order = (1, 3, 2, 0) if batch_minor else (1, 3, 0, 2)
    return x.reshape(batch, seq, heads, RW_N).transpose(order).reshape(seq, RW_N, batch * heads)


def _from_lanes(y, batch, seq, batch_minor):
    heads = y.shape[2] // batch
    if batch_minor:
        y = y.reshape(seq, RW_N, heads, batch).transpose(3, 0, 2, 1)
    else:
        y = y.reshape(seq, RW_N, batch, heads).transpose(2, 0, 3, 1)
    return y.reshape(batch * seq, heads * RW_N)


def _row(v):
    return v.reshape(1, -1)


def _even_mixers(x, pos, W, hg_s0, paged):
    batch, seq, d = x.shape
    n = batch * seq
    x2d = x.reshape(n, d)
    row = _row
    cos_tab, sin_tab = _rope_tables(pos, max(seq, min(ROW_TILE, n)))
    zhg, qlat, qrope, ckv, kr, ckvb, krb, mg = _even_pre(
        x2d, cos_tab, sin_tab, row(W['mix_norm'][0]), W['hg_lb_logits'], W['w_in_hg'], W['w_in_mla'],
        row(W['mla_q_norm'][0]), row(W['mla_kv_norm'][0]), W['w_uq'], W['w_uk'], lb_rows=1)
    g_hg = row(W['hg_norm'][0])
    c = dict(batch=batch, seq=seq, x2d=x2d, mg=mg, ckv=ckv, kr=kr)
    if paged is None:
        c['ohg'], c['s_hg'] = _hgrn_prompt(zhg, g_hg, batch, seq)
        c['ctx'] = _attn_prompt(qlat, qrope, ckvb, krb, batch, seq)
    else:
        c['ohg'], c['s_hg'] = _hgrn_sample(zhg, hg_s0, g_hg, seq)
        cache_ckv, cache_krope_t, page_table = paged
        stack = lambda q: q.reshape(MLA_HEADS, batch, seq, q.shape[-1]).transpose(1, 0, 2, 3).reshape(
            batch, MLA_HEADS * seq, q.shape[-1])
        pad = lambda a: jnp.pad(a.reshape(batch, seq, a.shape[-1]), ((0, 0), (0, 2 * SUBLANES - seq), (0, 0)))
        c['attn_args'] = (page_table, stack(qlat), stack(qrope), pad(ckvb), pad(krb).transpose(0, 2, 1),
                          cache_ckv, cache_krope_t, seq)
    return c


def _stacked_ctx_to_rows(ctx, batch, seq):
    return ctx.reshape(batch, MLA_HEADS, seq, KV_LORA).transpose(0, 2, 1, 3).reshape(batch * seq, MLA_HEADS * KV_LORA)


def _rwkv_operands(c, p, W, wkv_s0, shift_s0):
    batch, seq, x2d = c['batch'], c['seq'], c['x2d']
    n, d = x2d.shape
    row = _row
    p3d = p.reshape(p.shape[0], n, p.shape[-1])
    h1, hn1, tile_last = _even_post(
        x2d, c['ohg'], c['ctx'], c['mg'], p3d, 0, W['w_uv_bd'], W['w_out'], row(W['ple_norm'][0]),
        W['ple_gate_b'][0], W['ple_proj_b'][0], row(W['mix_norm'][1]))
    hn3 = hn1.reshape(batch, seq, d)
    heads = d // RW_N
    batch_lanes = batch % LANES == 0
    time_minor = seq % LANES == 0 and batch * heads == LANES
    if time_minor:
        tile_last = tile_last.reshape(batch, -1, d)
        prev = jnp.concatenate([shift_s0[:, None, :], tile_last[:, :-1]], axis=1).reshape(-1, 1, d)
    else:
        prev = jnp.concatenate([shift_s0[:, None, :], hn3[:, :-1]], axis=1).reshape(n, d)
    *scan_in, bonus, g = _rwkv_pre(
        hn1, prev, W['rw_mu'][0], W['w_r'], W['w_k'], W['w_v'], W['w_g'], row(W['rw_w0'][0]), W['w_w1'],
        W['w_w2'], row(W['rw_a0'][0]), W['w_a1'], W['w_a2'], row(W['rw_k_k'][0]), row(W['rw_k_a'][0]),
        row(W['rw_r_k'][0]), W['head_bd'], batch=batch, time_minor=time_minor)
    if time_minor:
        scan_in = _retile([a.reshape(seq // LANES, batch, heads, RW_N, LANES) for a in scan_in], True)
    else:
        scan_in = [_to_lanes(a, batch, seq, batch_lanes) for a in scan_in]
    if batch_lanes:
        scan_s0 = wkv_s0.transpose(1, 2, 3, 0)
    else:
        scan_s0 = wkv_s0.transpose(3, 2, 0, 1).reshape(RW_N, RW_N, -1)
    c.update(scan_in=scan_in, scan_s0=scan_s0, batch_lanes=batch_lanes, time_minor=time_minor, bonus=bonus, g=g,
             h1=h1, hn3=hn3, p3d=p3d)
    return c


def _group_outputs(c, y, s_wkv, W):
    batch, seq, h1 = c['batch'], c['seq'], c['h1']
    d = h1.shape[1]
    heads = d // RW_N
    row = _row
    time_minor, batch_lanes = c['time_minor'], c['batch_lanes']
    if batch_lanes:
        s_wkv = s_wkv.transpose(3, 0, 1, 2)
    else:
        s_wkv = s_wkv.reshape(RW_N, RW_N, batch, heads).transpose(2, 3, 1, 0)
    if time_minor:
        y = _retile([y], False, batch)[0].reshape(seq // LANES, batch, d, LANES)
    else:
        y = _from_lanes(y, batch, seq, batch_lanes)
    out = _rwkv_post(
        y, c['bonus'], c['g'], h1, c['p3d'], 1, row(W['rw_ln_w'][0]), row(W['rw_ln_b'][0]), W['head_bd'],
        W['w_o'], row(W['ple_norm'][1]), W['ple_gate_b'][1], W['ple_proj_b'][1], row(W['final_norm']),
        time_minor=time_minor)
    return (out.reshape(batch, seq, d), c['ckv'].reshape(1, batch, seq, KV_LORA),
            c['kr'].reshape(1, batch, seq, ROPE_DIM), c['s_hg'][None], s_wkv[None], c['hn3'][:, -1][None])


def kernel(x_prompt, x_sample, cache_ckv, cache_krope, state_hgrn, state_wkv, state_shift, page_table, p_prompt, p_sample, mix_norm, ev_w_in, hg_lb_logits, hg_norm, mla_q_norm, mla_w_uq, mla_kv_norm, mla_w_uk, mla_w_uv, ev_w_out, rw_mu, rw_w_rkvg, rw_w0, rw_w1, rw_w2, rw_a0, rw_a1, rw_a2, rw_k_k, rw_k_a, rw_r_k, rw_ln_w, rw_ln_b, rw_w_o, ple_norm, ple_gate, ple_proj, final_norm):
    bf = lambda a: a.astype(BF16)
    d = x_prompt.shape[-1]
    w_in = ev_w_in[0]
    o = np.cumsum([0, HG_WIDTH, HG_WIDTH, HG_HEADS * HG_DV, HG_HEADS * HG_DV, Q_LORA, KV_LORA, ROPE_DIM, MLA_WIDTH])
    cq, ckv_w, kr_w, mg_w = (w_in[:, o[4]:o[5]], w_in[:, o[5]:o[6]], w_in[:, o[6]:o[7]], w_in[:, o[7]:o[8]])
    uq = mla_w_uq[0].reshape(Q_LORA, MLA_HEADS, NOPE_DIM + ROPE_DIM)
    uq_rope = uq[:, :, NOPE_DIM:]
    uv_bd = jnp.zeros((MLA_HEADS, KV_LORA, MLA_HEADS, V_DIM), F32)
    uv_bd = uv_bd.at[jnp.arange(MLA_HEADS), :, jnp.arange(MLA_HEADS), :].set(mla_w_uv[0].transpose(0, 2, 1))
    hid = np.arange(MXU_DIM) // RW_N
    W = dict(
        mix_norm=mix_norm, hg_lb_logits=hg_lb_logits, hg_norm=hg_norm, mla_q_norm=mla_q_norm,
        mla_kv_norm=mla_kv_norm, ple_norm=ple_norm, final_norm=final_norm, rw_mu=rw_mu, rw_w0=rw_w0, rw_a0=rw_a0,
        rw_k_k=rw_k_k, rw_k_a=rw_k_a, rw_ln_w=rw_ln_w, rw_ln_b=rw_ln_b, rw_r_k=rw_r_k.reshape(rw_r_k.shape[0], -1),
        w_in_hg=bf(w_in[:, :o[4]]),
        w_in_mla=bf(jnp.concatenate([cq, ckv_w, mg_w, kr_w, _swap_halves(kr_w)], axis=-1)),
        w_uq=bf(jnp.concatenate([uq[:, :, :NOPE_DIM].reshape(Q_LORA, -1), uq_rope.reshape(Q_LORA, -1),
                                 _swap_halves(uq_rope).reshape(Q_LORA, -1)], axis=-1)),
        w_uk=bf(mla_w_uk[0]),
        w_uv_bd=bf(uv_bd.reshape(MLA_HEADS * KV_LORA, MLA_WIDTH)),
        w_out=bf(ev_w_out[0]),
        ple_gate_b=bf(ple_gate), ple_proj_b=bf(ple_proj),
        w_r=bf(rw_w_rkvg[0, 0]), w_k=bf(rw_w_rkvg[0, 1]), w_v=bf(rw_w_rkvg[0, 2]), w_g=bf(rw_w_rkvg[0, 3]),
        w_w1=bf(rw_w1[0]), w_w2=bf(rw_w2[0]), w_a1=bf(rw_a1[0]), w_a2=bf(rw_a2[0]), w_o=bf(rw_w_o[0]),
        head_bd=jnp.asarray(hid[:, None] == hid[None, :], BF16),
    )
    bp, tp, _ = x_prompt.shape
    bs, ts, _ = x_sample.shape
    past_len = page_table.shape[1] * cache_ckv.shape[2]
    heads = d // RW_N
    paged = (cache_ckv.reshape(cache_ckv.shape[1:]), jnp.swapaxes(cache_krope.reshape(cache_krope.shape[1:]), 1, 2),
             page_table)
    cp = _even_mixers(x_prompt, jnp.arange(tp), W, None, None)
    cp = _rwkv_operands(cp, p_prompt, W, jnp.zeros((bp, heads, RW_N, RW_N), F32), jnp.zeros((bp, d), F32))
    cs = _even_mixers(x_sample, past_len + jnp.arange(ts), W, state_hgrn[0], paged)
    y_p, wkv_p = _wkv_scan(*cp['scan_in'], cp['scan_s0'], value_major=cp['batch_lanes'])
    yp, ckv_p, kr_p, hg_p, wkv_p, sh_p = _group_outputs(cp, y_p, wkv_p, W)
    cs['ctx'] = _stacked_ctx_to_rows(_attn_sample(*cs['attn_args']), bs, ts)
    cs = _rwkv_operands(cs, p_sample, W, state_wkv[0], state_shift[0])
    y_s, wkv_s = _wkv_scan(*cs['scan_in'], cs['scan_s0'], value_major=cs['batch_lanes'])
    ys, ckv_s, kr_s, hg_s, wkv_s, sh_s = _group_outputs(cs, y_s, wkv_s, W)
    return (yp, ys, ckv_p, kr_p, ckv_s, kr_s, hg_p, hg_s, wkv_p, wkv_s, sh_p, sh_s)
```

```python
import functools

import jax
import jax.numpy as jnp
import numpy as np
from jax import lax
from jax.experimental import pallas as pl
from jax.experimental.pallas import tpu as pltpu

F32 = jnp.float32
BF16 = jnp.bfloat16

NORM_EPS = 1e-6
HG_HEADS = 4
HG_DK = 128
HG_DV = 128
HG_WIDTH = HG_HEADS * HG_DK
MLA_HEADS = 8
Q_LORA = 384
KV_LORA = 256
NOPE_DIM = 64
ROPE_DIM = 32
V_DIM = 64
MLA_WIDTH = MLA_HEADS * V_DIM
MLA_SCALE = (NOPE_DIM + ROPE_DIM) ** -0.5
ROPE_THETA = 10000.0
RW_N = 64
RW_EPS = 64e-5

LANES = 128
SUBLANES = 8
MXU_DIM = 256
VMEM_LIMIT_BYTES = 56 * 1024 * 1024

ROW_TILE = 256
HG_CHUNK = 64
HG_BLOCK = SUBLANES
HG_TIME_BLOCK = 1024
HG_SEQS_PER_STEP = 2
ATT_BLOCK = 256
PAGES_PER_STEP = 64
PAGES_PER_BLOCK = 8
PAGE_RING = 4
SCAN_TIME_BLOCK = 64
RETILE_TIME_BLOCKS = 2


def _cparams(sem):
    return pltpu.CompilerParams(dimension_semantics=sem, vmem_limit_bytes=VMEM_LIMIT_BYTES)


def _rms(x, g):
    return x * lax.rsqrt(jnp.mean(x * x, axis=-1, keepdims=True) + NORM_EPS) * g


def _sigmoid(x):
    return 1.0 / (1.0 + jnp.exp(-x))


def _silu(x):
    return x * _sigmoid(x)


def _dot(a, b):
    return jnp.dot(a, b, preferred_element_type=F32)


def _dot_nt(a, b):
    return lax.dot_general(a, b, (((1,), (1,)), ((), ())), preferred_element_type=F32)


def _head_sum(x, bd):
    hi = x.astype(BF16)
    lo = (x - hi.astype(F32)).astype(BF16)
    outs = []
    for c in range(x.shape[-1] // MXU_DIM):
        sl = slice(c * MXU_DIM, (c + 1) * MXU_DIM)
        outs.append(_dot(hi[:, sl], bd) + _dot(lo[:, sl], bd))
    return jnp.concatenate(outs, axis=-1)


def _full(shape):
    nd = len(shape)
    return pl.BlockSpec(shape, lambda *_: (0,) * nd)


def _even_pre_kernel(x_ref, cos_ref, sin_ref, g_ref, lbl_ref, w1_ref, w2_ref, qg_ref, kvg_ref, wuq_ref,
                     wuk_ref, zhg_ref, qlat_ref, qrope_ref, ckv_ref, kr_ref, ckvb_ref, krb_ref, mg_ref,
                     *, lb_rows):
    hn = _rms(x_ref[...], g_ref[...]).astype(BF16)
    z2 = _dot(hn, w2_ref[...])
    z1 = _dot(hn, w1_ref[...])
    lg = lbl_ref[...]
    e = jnp.exp(lg - jnp.max(lg, axis=0, keepdims=True))
    p = e / jnp.sum(e, axis=0, keepdims=True)
    lb = jnp.sum(p[:lb_rows], axis=0, keepdims=True)
    W = HG_WIDTH
    f = lb + (1.0 - lb) * _sigmoid(z1[:, W:2 * W])
    zhg_ref[:, 0:W] = _silu(z1[:, 0:W])
    zhg_ref[:, W:2 * W] = jnp.log(f)
    zhg_ref[:, 2 * W:3 * W] = z1[:, 2 * W:3 * W]
    zhg_ref[:, 3 * W:4 * W] = _silu(z1[:, 3 * W:4 * W])
    o_kv = Q_LORA
    o_mg = o_kv + KV_LORA
    o_kr = o_mg + MLA_WIDTH
    o_krs = o_kr + ROPE_DIM
    cqn = _rms(z2[:, 0:o_kv], qg_ref[...]).astype(BF16)
    qf = _dot(cqn, wuq_ref[...])
    cos = cos_ref[...]
    sin = sin_ref[...]
    n_nope = MLA_HEADS * NOPE_DIM
    n_rope = MLA_HEADS * ROPE_DIM
    qr = (qf[:, n_nope:n_nope + n_rope] * cos + qf[:, n_nope + n_rope:n_nope + 2 * n_rope] * sin) * MLA_SCALE
    for h in range(MLA_HEADS):
        qrope_ref[h] = qr[:, h * ROPE_DIM:(h + 1) * ROPE_DIM].astype(BF16)
        qn = qf[:, h * NOPE_DIM:(h + 1) * NOPE_DIM].astype(BF16)
        qlat_ref[h] = (_dot(qn, wuk_ref[h]) * MLA_SCALE).astype(BF16)
    ckv = _rms(z2[:, o_kv:o_mg], kvg_ref[...])
    ckv_ref[...] = ckv
    ckvb_ref[...] = ckv.astype(BF16)
    kr = z2[:, o_kr:o_krs] * cos[:, :ROPE_DIM] + z2[:, o_krs:o_krs + ROPE_DIM] * sin[:, :ROPE_DIM]
    kr_ref[...] = kr
    krb_ref[...] = kr.astype(BF16)
    mg_ref[...] = _silu(z2[:, o_mg:o_kr]).astype(BF16)


def _even_pre(x2d, cos_tab, sin_tab, g, lb_logits, w1, w2, qg, kvg, wuq, wuk, *, lb_rows):
    n, d = x2d.shape
    tm = min(ROW_TILE, n)
    n_tab = cos_tab.shape[0] // tm
    row = lambda i: (i, 0)
    tab = lambda i: (i % n_tab, 0)
    out_shape = (
        jax.ShapeDtypeStruct((n, 4 * HG_WIDTH), F32),
        jax.ShapeDtypeStruct((MLA_HEADS, n, KV_LORA), BF16),
        jax.ShapeDtypeStruct((MLA_HEADS, n, ROPE_DIM), BF16),
        jax.ShapeDtypeStruct((n, KV_LORA), F32),
        jax.ShapeDtypeStruct((n, ROPE_DIM), F32),
        jax.ShapeDtypeStruct((n, KV_LORA), BF16),
        jax.ShapeDtypeStruct((n, ROPE_DIM), BF16),
        jax.ShapeDtypeStruct((n, MLA_WIDTH), BF16),
    )
    out_specs = (
        pl.BlockSpec((tm, 4 * HG_WIDTH), row),
        pl.BlockSpec((MLA_HEADS, tm, KV_LORA), lambda i: (0, i, 0)),
        pl.BlockSpec((MLA_HEADS, tm, ROPE_DIM), lambda i: (0, i, 0)),
        pl.BlockSpec((tm, KV_LORA), row),
        pl.BlockSpec((tm, ROPE_DIM), row),
        pl.BlockSpec((tm, KV_LORA), row),
        pl.BlockSpec((tm, ROPE_DIM), row),
        pl.BlockSpec((tm, MLA_WIDTH), row),
    )
    in_specs = [
        pl.BlockSpec((tm, d), row),
        pl.BlockSpec((tm, cos_tab.shape[1]), tab),
        pl.BlockSpec((tm, sin_tab.shape[1]), tab),
        _full(g.shape), _full(lb_logits.shape), _full(w1.shape), _full(w2.shape), _full(qg.shape),
        _full(kvg.shape), _full(wuq.shape), _full(wuk.shape),
    ]
    return pl.pallas_call(
        functools.partial(_even_pre_kernel, lb_rows=lb_rows),
        grid=(n // tm,), in_specs=in_specs, out_specs=out_specs, out_shape=out_shape,
        compiler_params=_cparams(("parallel",)), name="even_pre",
    )(x2d, cos_tab, sin_tab, g, lb_logits, w1, w2, qg, kvg, wuq, wuk)


def _tril_ones(n, block):
    r = lax.broadcasted_iota(jnp.int32, (n, n), 0)
    c = lax.broadcasted_iota(jnp.int32, (n, n), 1)
    return ((r >= c) & ((r // block) == (c // block))).astype(F32)


def _hgrn_exact_blocks(q, k, v, b, block):
    rows = q.shape[0]
    rid = lax.broadcasted_iota(jnp.int32, (rows, 1), 0) % block
    o = jnp.sum(q * k, axis=-1, keepdims=True) * v
    for d in range(1, block):
        kd = pltpu.roll(k, d, 0)
        bd = pltpu.roll(b, d, 0)
        vd = pltpu.roll(v, d, 0)
        w = jnp.sum(q * kd * jnp.exp(jnp.minimum(b - bd, 0.0)), axis=-1, keepdims=True)
        o = o + jnp.where(rid >= d, w, 0.0) * vd
    return o


def _hgrn_finish(o, g, gate):
    return (o * lax.rsqrt(jnp.mean(o * o, axis=-1, keepdims=True) + NORM_EPS) * g * gate).astype(BF16)


def _hgrn_prompt_kernel(q_ref, lf_ref, v_ref, gt_ref, g_ref, o_ref, s_ref, st_ref):
    C = HG_CHUNK
    nseq, tb, _ = q_ref.shape
    tril = _tril_ones(C, C)
    rid = lax.broadcasted_iota(jnp.int32, (C, 1), 0)
    rr = lax.broadcasted_iota(jnp.int32, (C, C), 0)
    cc = lax.broadcasted_iota(jnp.int32, (C, C), 1)
    g = g_ref[...]

    @pl.when(pl.program_id(1) == 0)
    def _():
        st_ref[...] = jnp.zeros(st_ref.shape, F32)

    def decay(x):
        x['k'] = 1.0 - jnp.exp(x['lf'])
        x['b'] = jnp.dot(tril, x['lf'], precision=lax.Precision.HIGHEST, preferred_element_type=F32)

    def scores(x):
        q, k, b = x['q'], x['k'], x['b']
        x['o'] = _dot_nt((q * jnp.exp(b)).astype(BF16), x['st'].astype(BF16))
        x['att'] = []
        m = HG_BLOCK
        while 2 * m <= C:
            nb = C // (2 * m)
            b3 = b.reshape(nb, 2 * m, HG_DK)
            ref = jnp.broadcast_to(b3[:, m - 1:m, :], (nb, 2 * m, HG_DK)).reshape(C, HG_DK)
            upper = (rid % (2 * m)) >= m
            qt = q * jnp.where(upper, jnp.exp(jnp.minimum(b - ref, 0.0)), 0.0)
            kt = k * jnp.where(upper, 0.0, jnp.exp(jnp.minimum(ref - b, 0.0)))
            x['att'].append((m, _dot_nt(qt.astype(BF16), kt.astype(BF16))))
            m *= 2
        bend = b[C - 1:C, :]
        x['st'] = x['st'] * jnp.exp(bend) + _dot(x['v'].T.astype(BF16), (k * jnp.exp(bend - b)).astype(BF16))
        x['o'] = x['o'] + _hgrn_exact_blocks(q, k, x['v'], b, HG_BLOCK)

    def values(x):
        att = jnp.zeros((C, C), F32)
        for m, a in x['att']:
            if 2 * m < C:
                a = jnp.where((rr // (2 * m)) == (cc // (2 * m)), a, 0.0)
            att = att + a
        x['pv'] = _dot(att.astype(BF16), x['v'].astype(BF16))

    def chunk(c, carry):
        sl = pl.ds(pl.multiple_of(c * C, C), C)
        xs = [dict(s=s, h=h, hs=slice(h * HG_DK, (h + 1) * HG_DK)) for s in range(nseq) for h in range(HG_HEADS)]
        for x in xs:
            s, h, hs = x['s'], x['h'], x['hs']
            x.update(st=st_ref[s, h], q=q_ref[s, sl, hs], lf=lf_ref[s, sl, hs], v=v_ref[s, sl, hs],
                     gate=gt_ref[s, sl, hs])
        for stage in (decay, scores, values):
            for x in xs:
                stage(x)
        for x in xs:
            o_ref[x['s'], sl, x['hs']] = _hgrn_finish(x['o'] + x['pv'], g, x['gate'])
            st_ref[x['s'], x['h']] = x['st']
        return carry

    lax.fori_loop(0, tb // C, chunk, 0)

    @pl.when(pl.program_id(1) == pl.num_programs(1) - 1)
    def _():
        for s in range(nseq):
            for h in range(HG_HEADS):
                s_ref[s, h] = st_ref[s, h].T


def _hgrn_prompt(zhg, g, batch, seq):
    nseq = HG_SEQS_PER_STEP if batch % HG_SEQS_PER_STEP == 0 else 1
    tb = min(HG_TIME_BLOCK // nseq, seq)
    z3 = zhg.reshape(batch, seq, 4 * HG_WIDTH)
    blk = lambda off: pl.BlockSpec((nseq, tb, HG_WIDTH), lambda b, i: (b, i, off))
    ohg, s_hg = pl.pallas_call(
        _hgrn_prompt_kernel,
        grid=(batch // nseq, seq // tb),
        in_specs=[blk(0), blk(1), blk(2), blk(3), _full(g.shape)],
        out_specs=(pl.BlockSpec((nseq, tb, HG_HEADS * HG_DV), lambda b, i: (b, i, 0)),
                   pl.BlockSpec((nseq, HG_HEADS, HG_DK, HG_DV), lambda b, i: (b, 0, 0, 0))),
        out_shape=(jax.ShapeDtypeStruct((batch, seq, HG_HEADS * HG_DV), BF16),
                   jax.ShapeDtypeStruct((batch, HG_HEADS, HG_DK, HG_DV), F32)),
        scratch_shapes=[pltpu.VMEM((nseq, HG_HEADS, HG_DV, HG_DK), F32)],
        compiler_params=_cparams(("parallel", "arbitrary")), name="hgrn_prompt",
    )(z3, z3, z3, z3, g)
    return ohg.reshape(batch * seq, HG_HEADS * HG_DV), s_hg


def _hgrn_sample_kernel(z_ref, s0_ref, g_ref, o_ref, s_ref, *, seq):
    rows = z_ref.shape[0]
    nb = rows // seq
    tril = _tril_ones(rows, seq)
    rb = lax.broadcasted_iota(jnp.int32, (rows, 1), 0) // seq
    cb = lax.broadcasted_iota(jnp.int32, (1, rows), 1) // seq
    g = g_ref[...]
    W = HG_WIDTH
    for h in range(HG_HEADS):
        hs = slice(h * HG_DK, (h + 1) * HG_DK)
        q = z_ref[:, hs]
        lf = z_ref[:, W + h * HG_DK:W + (h + 1) * HG_DK]
        v = z_ref[:, 2 * W + h * HG_DV:2 * W + (h + 1) * HG_DV]
        gate = z_ref[:, 3 * W + h * HG_DV:3 * W + (h + 1) * HG_DV]
        k = 1.0 - jnp.exp(lf)
        b = jnp.dot(tril, lf, precision=lax.Precision.HIGHEST, preferred_element_type=F32)
        o = _hgrn_exact_blocks(q, k, v, b, seq)
        qe = (q * jnp.exp(b)).astype(BF16)
        bt = b.T
        vb = v.astype(BF16)
        for i in range(nb):
            s0 = s0_ref[i, h]
            o = o + jnp.where(rb == i, _dot(qe, s0.astype(BF16)), 0.0)
            last = i * seq + seq - 1
            bend_row = b[last:last + 1, :]
            kt = jnp.where(rb == i, k * jnp.exp(bend_row - b), 0.0)
            ktt = kt.T.astype(BF16)
            s_ref[i, h] = s0 * jnp.exp(bt[:, last:last + 1]) + _dot(ktt, vb)
        o_ref[:, hs] = _hgrn_finish(o, g, gate)


def _hgrn_sample(zhg, s0, g, seq):
    n = zhg.shape[0]
    nb = 8
    rows = nb * seq
    return pl.pallas_call(
        functools.partial(_hgrn_sample_kernel, seq=seq),
        grid=(n // rows,),
        in_specs=[pl.BlockSpec((rows, 4 * HG_WIDTH), lambda i: (i, 0)),
                  pl.BlockSpec((nb, HG_HEADS, HG_DK, HG_DV), lambda i: (i, 0, 0, 0)),
                  _full(g.shape)],
        out_specs=(pl.BlockSpec((rows, HG_HEADS * HG_DV), lambda i: (i, 0)),
                   pl.BlockSpec((nb, HG_HEADS, HG_DK, HG_DV), lambda i: (i, 0, 0, 0))),
        out_shape=(jax.ShapeDtypeStruct((n, HG_HEADS * HG_DV), BF16),
                   jax.ShapeDtypeStruct(s0.shape, F32)),
        compiler_params=_cparams(("parallel",)), name="hgrn_sample",
    )(zhg, s0, g)


def _lane_tile(x, width):
    if width <= LANES:
        return x[:, :width]
    return jnp.concatenate([x] * (width // LANES), axis=1)


def _softmax_init(m_ref, l_ref, acc_ref):
    m_ref[...] = jnp.full(m_ref.shape, -jnp.inf, F32)
    l_ref[...] = jnp.zeros(l_ref.shape, F32)
    acc_ref[...] = jnp.zeros(acc_ref.shape, F32)


def _online_step(s, kv, m_old, l_old, acc_old):
    m_new = jnp.maximum(m_old, jnp.max(s, axis=-1, keepdims=True))
    alpha = jnp.exp(m_old - m_new)
    p = jnp.exp(s - _lane_tile(m_new, s.shape[1]))
    l_new = alpha * l_old + jnp.sum(p, axis=-1, keepdims=True)
    acc_new = _lane_tile(alpha, acc_old.shape[1]) * acc_old + _dot(p.astype(BF16), kv)
    return m_new, l_new, acc_new


def _online_update(s, kv, rows, m_ref, l_ref, acc_ref):
    m_ref[rows, :], l_ref[rows, :], acc_ref[rows, :] = _online_step(
        s, kv, m_ref[rows, :], l_ref[rows, :], acc_ref[rows, :])


def _attn_prompt_kernel(ql_ref, qr_ref, kv_ref, kr_ref, o_ref, m_ref, l_ref, acc_ref):
    tq = ql_ref.shape[1]
    i = pl.program_id(1)
    _softmax_init(m_ref, l_ref, acc_ref)
    n_split = 2
    hs = MLA_HEADS // n_split
    mh = hs * tq
    qpos = lax.broadcasted_iota(jnp.int32, (mh, tq), 0) % tq
    kpos = lax.broadcasted_iota(jnp.int32, (mh, tq), 1)

    def block(j, masked):
        sl = pl.ds(pl.multiple_of(j * tq, tq), tq)
        kv = kv_ref[sl, :]
        kr = kr_ref[sl, :]
        scores = []
        for r in range(n_split):
            ql = ql_ref[r * hs:(r + 1) * hs].reshape(mh, KV_LORA)
            qr = qr_ref[r * hs:(r + 1) * hs].reshape(mh, ROPE_DIM)
            s = _dot_nt(ql, kv) + _dot_nt(qr, kr)
            scores.append(jnp.where(kpos <= qpos, s, -jnp.inf) if masked else s)
        for r, s in enumerate(scores):
            _online_update(s, kv, slice(r * mh, (r + 1) * mh), m_ref, l_ref, acc_ref)

    def body(j, carry):
        block(j, False)
        return carry

    lax.fori_loop(0, i, body, 0)
    block(i, True)
    for h in range(MLA_HEADS):
        rows = slice(h * tq, (h + 1) * tq)
        out = acc_ref[rows, :] / _lane_tile(l_ref[rows, :], KV_LORA)
        o_ref[:, h * KV_LORA:(h + 1) * KV_LORA] = out.astype(BF16)


def _attn_prompt(qlat, qrope, ckvb, krb, batch, seq):
    tq = ATT_BLOCK
    nq = seq // tq
    rows = MLA_HEADS * tq
    return pl.pallas_call(
        _attn_prompt_kernel,
        grid=(batch, nq),
        in_specs=[pl.BlockSpec((MLA_HEADS, tq, KV_LORA), lambda b, i: (0, b * nq + i, 0)),
                  pl.BlockSpec((MLA_HEADS, tq, ROPE_DIM), lambda b, i: (0, b * nq + i, 0)),
                  pl.BlockSpec((seq, KV_LORA), lambda b, i: (b, 0)),
                  pl.BlockSpec((seq, ROPE_DIM), lambda b, i: (b, 0))],
        out_specs=pl.BlockSpec((tq, MLA_HEADS * KV_LORA), lambda b, i: (b * nq + i, 0)),
        out_shape=jax.ShapeDtypeStruct((batch * seq, MLA_HEADS * KV_LORA), BF16),
        scratch_shapes=[pltpu.VMEM((rows, LANES), F32), pltpu.VMEM((rows, LANES), F32),
                        pltpu.VMEM((rows, KV_LORA), F32)],
        compiler_params=_cparams(("parallel", "arbitrary")), name="attn_prompt",
    )(qlat, qrope, ckvb, krb)


class _PageRing:
    def __init__(self, pt_ref, ckv_hbm, kr_hbm, kvbuf, krbuf, sems, units_per_seq):
        self.pt_ref, self.ckv_hbm, self.kr_hbm = pt_ref, ckv_hbm, kr_hbm
        self.kvbuf, self.krbuf, self.sems = kvbuf, krbuf, sems
        self.units_per_seq = units_per_seq
        self.pages = krbuf.shape[1]
        self.page = kvbuf.shape[1] // self.pages

    def _copies(self, u):
        slot = u % PAGE_RING
        seq_id = u // self.units_per_seq
        first = (u % self.units_per_seq) * self.pages
        out = []
        for j in range(self.pages):
            pg = self.pt_ref[seq_id, first + j]
            out.append(pltpu.make_async_copy(
                self.ckv_hbm.at[pg], self.kvbuf.at[slot, pl.ds(j * self.page, self.page), :], self.sems.at[0, slot]))
            out.append(pltpu.make_async_copy(self.kr_hbm.at[pg], self.krbuf.at[slot, j], self.sems.at[1, slot]))
        return out

    def prime(self, total):
        for u in range(PAGE_RING - 1):
            @pl.when(u < total)
            def _():
                for c in self._copies(u):
                    c.start()

    def wait(self, u):
        for c in self._copies(u):
            c.wait()

    def refill(self, u, total):
        @pl.when(u + PAGE_RING - 1 < total)
        def _():
            for c in self._copies(u + PAGE_RING - 1):
                c.start()

    def attend(self, u, ql, qr, state):
        slot = u % PAGE_RING
        sub = min(PAGES_PER_BLOCK, self.pages)
        blocks = []
        for c in range(0, self.pages, sub):
            kv = self.kvbuf[slot, c * self.page:(c + sub) * self.page, :].astype(BF16)
            kr = jnp.concatenate([self.krbuf[slot, c + j].astype(BF16) for j in range(sub)], axis=1)
            blocks.append((_dot_nt(ql, kv) + _dot(qr, kr), kv))
        for s, kv in blocks:
            state = _online_step(s, kv, *state)
        return state


def _attend_new_rows(ql, qr, kvn, krn, seq, state):
    sn = _dot_nt(ql, kvn) + _dot(qr, krn)
    qpos = lax.broadcasted_iota(jnp.int32, sn.shape, 0) % seq
    kpos = lax.broadcasted_iota(jnp.int32, sn.shape, 1)
    _, l, acc = _online_step(jnp.where(kpos <= qpos, sn, -jnp.inf), kvn, *state)
    return (acc / _lane_tile(l, KV_LORA)).astype(BF16)


def _attn_sample_kernel(pt_ref, ql_ref, qr_ref, kvn_ref, krn_ref, ckv_hbm, kr_hbm, o_ref,
                        kvbuf, krbuf, sems, m_ref, l_ref, acc_ref, *, seq):
    n_groups = pl.num_programs(1)
    g = pl.program_id(1)
    unit = pl.program_id(0) * n_groups + g
    total = pl.num_programs(0) * n_groups
    ring = _PageRing(pt_ref, ckv_hbm, kr_hbm, kvbuf, krbuf, sems, n_groups)

    @pl.when(unit == 0)
    def _():
        ring.prime(total)

    @pl.when(g == 0)
    def _():
        _softmax_init(m_ref, l_ref, acc_ref)

    ring.wait(unit)
    ql = ql_ref[...]
    qr = qr_ref[...]
    state = ring.attend(unit, ql, qr, (m_ref[...], l_ref[...], acc_ref[...]))
    m_ref[...], l_ref[...], acc_ref[...] = state
    ring.refill(unit, total)

    @pl.when(g == n_groups - 1)
    def _():
        o_ref[...] = _attend_new_rows(ql, qr, kvn_ref[...], krn_ref[...], seq, state)


def _page_ring_scratch(pages, page, rows):
    return [pltpu.VMEM((PAGE_RING, pages * page, KV_LORA), F32),
            pltpu.VMEM((PAGE_RING, pages, ROPE_DIM, page), F32),
            pltpu.SemaphoreType.DMA((2, PAGE_RING)),
            pltpu.VMEM((rows, LANES), F32), pltpu.VMEM((rows, LANES), F32), pltpu.VMEM((rows, KV_LORA), F32)]


def _attn_sample(page_table, qlat, qrope, kv_new, kr_new_t, cache_ckv, cache_krope_t, seq):
    batch, n_pages = page_table.shape
    G = min(PAGES_PER_STEP, n_pages)
    page = cache_ckv.shape[1]
    rows = qlat.shape[1]
    npad = kv_new.shape[1]

    per_b = lambda r, width: pl.BlockSpec((None, r, width), lambda b, g, pt: (b, 0, 0))
    in_hbm = pl.BlockSpec(memory_space=pl.ANY)
    grid_spec = pltpu.PrefetchScalarGridSpec(
        num_scalar_prefetch=1,
        grid=(batch, n_pages // G),
        in_specs=[per_b(rows, KV_LORA), per_b(rows, ROPE_DIM), per_b(npad, KV_LORA), per_b(ROPE_DIM, npad),
                  in_hbm, in_hbm],
        out_specs=per_b(rows, KV_LORA),
        scratch_shapes=_page_ring_scratch(G, page, rows),
    )
    return pl.pallas_call(
        functools.partial(_attn_sample_kernel, seq=seq),
        grid_spec=grid_spec,
        out_shape=jax.ShapeDtypeStruct((batch, rows, KV_LORA), BF16),
        compiler_params=_cparams(("arbitrary", "arbitrary")), name="attn_sample",
    )(page_table, qlat, qrope, kv_new, kr_new_t, cache_ckv, cache_krope_t)


def _ple(h, emb, png, pgw):
    gate = _sigmoid(_dot(_rms(h, png).astype(BF16), pgw))
    return h + gate * emb


def _even_post_kernel(x_ref, ohg_ref, ctx_ref, mg_ref, p_ref, wuv_ref, wout_ref, png_ref, pgw_ref, ppw_ref,
                      ng_ref, h_ref, hn_ref, last_ref):
    n_hg = HG_HEADS * HG_DV
    ctx_up = _dot(ctx_ref[...], wuv_ref[...])
    hg_out = _dot(ohg_ref[...], wout_ref[0:n_hg, :])
    emb = _dot(p_ref[...].astype(BF16), ppw_ref[...])
    o_mla = (ctx_up * mg_ref[...]).astype(BF16)
    h = x_ref[...] + hg_out + _dot(o_mla, wout_ref[n_hg:, :])
    h = _ple(h, emb, png_ref[...], pgw_ref[...])
    h_ref[...] = h
    hn = _rms(h, ng_ref[...])
    hn_ref[...] = hn
    last_ref[...] = hn[-1:, :]


def _layer_rows(p3d, layer, tm):
    return pl.BlockSpec((None, tm, p3d.shape[2]), lambda i: (layer, i, 0))


def _even_post(x2d, ohg, ctx, mg, p3d, layer, wuv, wout, png, pgw, ppw, ng):
    n, d = x2d.shape
    tm = min(ROW_TILE, n)
    row = lambda a: pl.BlockSpec((tm, a.shape[1]), lambda i: (i, 0))
    return pl.pallas_call(
        _even_post_kernel,
        grid=(n // tm,),
        in_specs=[row(x2d), row(ohg), row(ctx), row(mg), _layer_rows(p3d, layer, tm), _full(wuv.shape),
                  _full(wout.shape), _full(png.shape), _full(pgw.shape), _full(ppw.shape), _full(ng.shape)],
        out_specs=(row(x2d), row(x2d), pl.BlockSpec((None, 1, d), lambda i: (i, 0, 0))),
        out_shape=(jax.ShapeDtypeStruct((n, d), F32), jax.ShapeDtypeStruct((n, d), F32),
                   jax.ShapeDtypeStruct((n // tm, 1, d), F32)),
        compiler_params=_cparams(("parallel",)), name="even_post",
    )(x2d, ohg, ctx, mg, p3d, wuv, wout, png, pgw, ppw, ng)


def _rwkv_pre_kernel(hn_ref, pv_ref, mu_ref, wr_ref, wk_ref, wv_ref, wg_ref, w0_ref, w1_ref, w2_ref, a0_ref,
                     a1_ref, a2_ref, kk_ref, ka_ref, rk_ref, bd_ref,
                     r_ref, w_ref, k_ref, v_ref, na_ref, b_ref, bonus_ref, g_ref, *, time_minor):
    def put(ref, x):
        if time_minor:
            xt = x.T
            for j in range(ref.shape[0]):
                ref[j] = xt[:, j * LANES:(j + 1) * LANES]
        else:
            ref[...] = x

    hn = hn_ref[...]
    if time_minor:
        first = lax.broadcasted_iota(jnp.int32, (hn.shape[0], 1), 0) == 0
        prev = jnp.where(first, pv_ref[...], pltpu.roll(hn, 1, 0))
    else:
        prev = pv_ref[...]
    dlt = prev - hn
    mix = lambda j: (hn + dlt * mu_ref[j:j + 1, :]).astype(BF16)
    w_mid = _dot(mix(4), w1_ref[...])
    a_mid = _dot(mix(5), a1_ref[...])
    r = _dot(mix(0), wr_ref[...])
    k = _dot(mix(1), wk_ref[...])
    v = _dot(mix(2), wv_ref[...])
    g = _dot(mix(3), wg_ref[...])
    wl = w0_ref[...] + _dot(jnp.tanh(w_mid).astype(BF16), w2_ref[...])
    w_log = -(jnp.maximum(-wl, 0.0) + jnp.log(1.0 + jnp.exp(-jnp.abs(wl)))) - 0.5
    a = _sigmoid(a0_ref[...] + _dot(a_mid.astype(BF16), a2_ref[...]))
    kk = k * kk_ref[...]
    kk = kk / jnp.maximum(jnp.sqrt(_head_sum(kk * kk, bd_ref[...])), 1e-12)
    k_mod = k * (1.0 + (a - 1.0) * ka_ref[...])
    put(r_ref, r)
    put(w_ref, jnp.exp(-jnp.exp(w_log)))
    put(k_ref, k_mod)
    put(v_ref, v)
    put(na_ref, -kk)
    put(b_ref, kk * a)
    bonus_ref[...] = (_head_sum(r * k_mod * rk_ref[...], bd_ref[...]) * v).astype(BF16)
    g_ref[...] = _silu(g).astype(BF16)


def _rwkv_pre(hn, prev, mu, wr, wk, wv, wg, w0, w1, w2, a0, a1, a2, kk, ka, rk, bd, *, batch, time_minor):
    n, d = hn.shape
    tm = min(ROW_TILE, n)
    seq = n // batch
    nt = seq // tm if time_minor else 1
    row = pl.BlockSpec((tm, d), lambda i: (i, 0))
    ws = [mu, wr, wk, wv, wg, w0, w1, w2, a0, a1, a2, kk, ka, rk, bd]
    if time_minor:
        scan_spec = pl.BlockSpec((tm // LANES, None, d, LANES), lambda i: (i % nt, i // nt, 0, 0))
        scan_shape = jax.ShapeDtypeStruct((seq // LANES, batch, d, LANES), F32)
        prev_spec = pl.BlockSpec((None, 1, d), lambda i: (i, 0, 0))
    else:
        scan_spec, scan_shape, prev_spec = row, jax.ShapeDtypeStruct((n, d), F32), row
    return pl.pallas_call(
        functools.partial(_rwkv_pre_kernel, time_minor=time_minor),
        grid=(n // tm,),
        in_specs=[row, prev_spec] + [_full(w.shape) for w in ws],
        out_specs=(scan_spec,) * 6 + (row, row),
        out_shape=(scan_shape,) * 6 + (jax.ShapeDtypeStruct((n, d), BF16),) * 2,
        compiler_params=_cparams(("parallel",)), name="rwkv_pre",
    )(hn, prev, *ws)


def _retile_kernel(*refs):
    n = len(refs) // 2
    for src, dst in zip(refs[:n], refs[n:]):
        kb = src.shape[-2]
        rows = int(np.prod(src.shape)) // LANES
        src2 = src.reshape(rows, LANES)
        dst2 = dst.reshape(rows, LANES)
        for first in range(0, rows, LANES * kb):
            for j in range(kb):
                sel = pl.ds(first + j, LANES, stride=kb)
                dst2[sel, :] = src2[sel, :].T


def _retile(arrays, to_scan, batch=None):
    kb = SUBLANES
    n = len(arrays)
    if to_scan:
        tb, nb, nh, k, _ = arrays[0].shape
        out_shape = jax.ShapeDtypeStruct((tb * LANES, k, LANES), F32)
    else:
        t, k, _ = arrays[0].shape
        nb, tb, nh = batch, t // LANES, LANES // batch
        out_shape = jax.ShapeDtypeStruct((tb, nb, nh, k, LANES), F32)
    u = RETILE_TIME_BLOCKS if tb % RETILE_TIME_BLOCKS == 0 else 1
    tiled = pl.BlockSpec((u, nb, nh, kb, LANES), lambda j, i: (i, 0, 0, j, 0))
    scan = pl.BlockSpec((u * LANES, kb, LANES), lambda j, i: (i, j, 0))
    return pl.pallas_call(
        _retile_kernel,
        grid=(k // kb, tb // u),
        in_specs=[tiled if to_scan else scan] * n,
        out_specs=(scan if to_scan else tiled,) * n,
        out_shape=(out_shape,) * n,
        compiler_params=_cparams(("parallel", "parallel")), name="retile",
    )(*arrays)


def _wkv_first_sa(s_ref, a_ref):
    sa = s_ref[0] * a_ref[0, 0:1, :]
    for k in range(1, s_ref.shape[0]):
        sa = sa + s_ref[k] * a_ref[0, k:k + 1, :]
    return sa


def _wkv_step(t, sa, r_ref, w_ref, k_ref, v_ref, a_ref, b_ref, y_ref, s_ref):
    tt = r_ref.shape[0]
    tn = jnp.minimum(t + 1, tt - 1)
    row = lambda ref, k: ref[t, k:k + 1, :]
    v = v_ref[t]
    y = jnp.zeros_like(v)
    sa_next = jnp.zeros_like(v)
    for k in range(s_ref.shape[0]):
        s = s_ref[k] * row(w_ref, k) + sa * row(b_ref, k) + v * row(k_ref, k)
        s_ref[k] = s
        y = y + s * row(r_ref, k)
        sa_next = sa_next + s * a_ref[tn, k:k + 1, :]
    y_ref[t] = y
    return sa_next


def _wkv_scan_kernel(r_ref, w_ref, k_ref, v_ref, a_ref, b_ref, s0_ref, y_ref, so_ref, s_ref, *, value_major):
    @pl.when(pl.program_id(1) == 0)
    def _():
        s_ref[...] = jnp.swapaxes(s0_ref[...], 0, 1) if value_major else s0_ref[...]

    step = functools.partial(_wkv_step, r_ref=r_ref, w_ref=w_ref, k_ref=k_ref, v_ref=v_ref, a_ref=a_ref,
                             b_ref=b_ref, y_ref=y_ref, s_ref=s_ref)
    lax.fori_loop(0, r_ref.shape[0], step, _wkv_first_sa(s_ref, a_ref))

    @pl.when(pl.program_id(1) == pl.num_programs(1) - 1)
    def _():
        so_ref[...] = jnp.swapaxes(s_ref[...], 0, 1) if value_major else s_ref[...]


def _wkv_scan(r, w, k, v, a, b, s0, *, value_major):
    t, n, lanes = r.shape
    tt = min(SCAN_TIME_BLOCK, t)
    seq = pl.BlockSpec((tt, n, LANES), lambda g, i: (i, 0, g))
    if value_major:
        nb = s0.shape[3] // LANES
        st = pl.BlockSpec((None, n, n, LANES), lambda g, i: (g // nb, 0, 0, g % nb))
    else:
        st = pl.BlockSpec((n, n, LANES), lambda g, i: (0, 0, g))
    return pl.pallas_call(
        functools.partial(_wkv_scan_kernel, value_major=value_major),
        grid=(lanes // LANES, t // tt),
        in_specs=[seq] * 6 + [st],
        out_specs=(seq, st),
        out_shape=(jax.ShapeDtypeStruct((t, n, lanes), F32), jax.ShapeDtypeStruct(s0.shape, F32)),
        scratch_shapes=[pltpu.VMEM((n, n, LANES), F32)],
        compiler_params=_cparams(("parallel", "arbitrary")), name="wkv_scan",
    )(r, w, k, v, a, b, s0)


def _rwkv_post_kernel(y_ref, bonus_ref, g_ref, h_ref, p_ref, lnw_ref, lnb_ref, bd_ref, wo_ref, png_ref,
                      pgw_ref, ppw_ref, fg_ref, o_ref, *, time_minor):
    bd = bd_ref[...]
    emb = _dot(p_ref[...].astype(BF16), ppw_ref[...])
    if time_minor:
        y = jnp.concatenate([y_ref[j] for j in range(y_ref.shape[0])], axis=1).T
    else:
        y = y_ref[...]
    inv_n = 1.0 / RW_N
    yc = y - _head_sum(y, bd) * inv_n
    var = _head_sum(yc * yc, bd) * inv_n
    yn = yc * lax.rsqrt(var + RW_EPS) * lnw_ref[...] + lnb_ref[...]
    mix = ((yn + bonus_ref[...]) * g_ref[...]).astype(BF16)
    h = h_ref[...] + _dot(mix, wo_ref[...])
    h = _ple(h, emb, png_ref[...], pgw_ref[...])
    o_ref[...] = _rms(h, fg_ref[...])


def _rwkv_post(y, bonus, g, h, p3d, layer, lnw, lnb, bd, wo, png, pgw, ppw, fg, *, time_minor):
    n, d = h.shape
    tm = min(ROW_TILE, n)
    row = lambda a: pl.BlockSpec((tm, a.shape[1]), lambda i: (i, 0))
    if time_minor:
        nt = y.shape[0] * LANES // tm
        y_spec = pl.BlockSpec((tm // LANES, None, d, LANES), lambda i: (i % nt, i // nt, 0, 0))
    else:
        y_spec = row(y)
    ws = [lnw, lnb, bd, wo, png, pgw, ppw, fg]
    return pl.pallas_call(
        functools.partial(_rwkv_post_kernel, time_minor=time_minor),
        grid=(n // tm,),
        in_specs=[y_spec, row(bonus), row(g), row(h), _layer_rows(p3d, layer, tm)] + [_full(w.shape) for w in ws],
        out_specs=row(h),
        out_shape=jax.ShapeDtypeStruct((n, d), F32),
        compiler_params=_cparams(("parallel",)), name="rwkv_post",
    )(y, bonus, g, h, p3d, *ws)


def _rope_tables(pos, rows):
    half = ROPE_DIM // 2
    inv = ROPE_THETA ** (-jnp.arange(half, dtype=F32) / half)
    ang = pos.astype(F32)[:, None] * inv[None, :]
    cos = jnp.cos(ang)
    sin = jnp.sin(ang)
    cos = jnp.tile(jnp.concatenate([cos, cos], axis=-1), (rows // pos.shape[0], MLA_HEADS))
    sin = jnp.tile(jnp.concatenate([-sin, sin], axis=-1), (rows // pos.shape[0], MLA_HEADS))
    return cos, sin


def _swap_halves(w):
    half = w.shape[-1] // 2
    return jnp.concatenate([w[..., half:], w[..., :half]], axis=-1)


def _to_lanes(x, batch, seq, batch_minor):
    heads = x.shape[1] // RW_N
    order = (1, 3, 2, 0) if batch_minor else (1, 3, 0, 2)
    return x.reshape(batch, seq, heads, RW_N).transpose(order).reshape(seq, RW_N, batch * heads)


def _from_lanes(y, batch, seq, batch_minor):
    heads = y.shape[2] // batch
    if batch_minor:
        y = y.reshape(seq, RW_N, heads, batch).transpose(3, 0, 2, 1)
    else:
        y = y.reshape(seq, RW_N, batch, heads).transpose(2, 0, 3, 1)
    return y.reshape(batch * seq, heads * RW_N)


def _row(v):
    return v.reshape(1, -1)


def _even_mixers(x, pos, W, hg_s0, paged):
    batch, seq, d = x.shape
    n = batch * seq
    x2d = x.reshape(n, d)
    row = _row
    cos_tab, sin_tab = _rope_tables(pos, max(seq, min(ROW_TILE, n)))
    zhg, qlat, qrope, ckv, kr, ckvb, krb, mg = _even_pre(
        x2d, cos_tab, sin_tab, row(W['mix_norm'][0]), W['hg_lb_logits'], W['w_in_hg'], W['w_in_mla'],
        row(W['mla_q_norm'][0]), row(W['mla_kv_norm'][0]), W['w_uq'], W['w_uk'], lb_rows=1)
    g_hg = row(W['hg_norm'][0])
    c = dict(batch=batch, seq=seq, x2d=x2d, mg=mg, ckv=ckv, kr=kr)
    if paged is None:
        c['ohg'], c['s_hg'] = _hgrn_prompt(zhg, g_hg, batch, seq)
        c['ctx'] = _attn_prompt(qlat, qrope, ckvb, krb, batch, seq)
    else:
        c['ohg'], c['s_hg'] = _hgrn_sample(zhg, hg_s0, g_hg, seq)
        cache_ckv, cache_krope_t, page_table = paged
        stack = lambda q: q.reshape(MLA_HEADS, batch, seq, q.shape[-1]).transpose(1, 0, 2, 3).reshape(
            batch, MLA_HEADS * seq, q.shape[-1])
        pad = lambda a: jnp.pad(a.reshape(batch, seq, a.shape[-1]), ((0, 0), (0, 2 * SUBLANES - seq), (0, 0)))
        c['attn_args'] = (page_table, stack(qlat), stack(qrope), pad(ckvb), pad(krb).transpose(0, 2, 1),
                          cache_ckv, cache_krope_t, seq)
    return c


def _stacked_ctx_to_rows(ctx, batch, seq):
    return ctx.reshape(batch, MLA_HEADS, seq, KV_LORA).transpose(0, 2, 1, 3).reshape(batch * seq, MLA_HEADS * KV_LORA)


def _rwkv_operands(c, p, W, wkv_s0, shift_s0):
    batch, seq, x2d = c['batch'], c['seq'], c['x2d']
    n, d = x2d.shape
    row = _row
    p3d = p.reshape(p.shape[0], n, p.shape[-1])
    h1, hn1, tile_last = _even_post(
        x2d, c['ohg'], c['ctx'], c['mg'], p3d, 0, W['w_uv_bd'], W['w_out'], row(W['ple_norm'][0]),
        W['ple_gate_b'][0], W['ple_proj_b'][0], row(W['mix_norm'][1]))
    hn3 = hn1.reshape(batch, seq, d)
    heads = d // RW_N
    batch_lanes = batch % LANES == 0
    time_minor = seq % LANES == 0 and batch * heads == LANES
    if time_minor:
        tile_last = tile_last.reshape(batch, -1, d)
        prev = jnp.concatenate([shift_s0[:, None, :], tile_last[:, :-1]], axis=1).reshape(-1, 1, d)
    else:
        prev = jnp.concatenate([shift_s0[:, None, :], hn3[:, :-1]], axis=1).reshape(n, d)
    *scan_in, bonus, g = _rwkv_pre(
        hn1, prev, W['rw_mu'][0], W['w_r'], W['w_k'], W['w_v'], W['w_g'], row(W['rw_w0'][0]), W['w_w1'],
        W['w_w2'], row(W['rw_a0'][0]), W['w_a1'], W['w_a2'], row(W['rw_k_k'][0]), row(W['rw_k_a'][0]),
        row(W['rw_r_k'][0]), W['head_bd'], batch=batch, time_minor=time_minor)
    if time_minor:
        scan_in = _retile([a.reshape(seq // LANES, batch, heads, RW_N, LANES) for a in scan_in], True)
    else:
        scan_in = [_to_lanes(a, batch, seq, batch_lanes) for a in scan_in]
    if batch_lanes:
        scan_s0 = wkv_s0.transpose(1, 2, 3, 0)
    else:
        scan_s0 = wkv_s0.transpose(3, 2, 0, 1).reshape(RW_N, RW_N, -1)
    c.update(scan_in=scan_in, scan_s0=scan_s0, batch_lanes=batch_lanes, time_minor=time_minor, bonus=bonus, g=g,
             h1=h1, hn3=hn3, p3d=p3d)
    return c


def _group_outputs(c, y, s_wkv, W):
    batch, seq, h1 = c['batch'], c['seq'], c['h1']
    d = h1.shape[1]
    heads = d // RW_N
    row = _row
    time_minor, batch_lanes = c['time_minor'], c['batch_lanes']
    if batch_lanes:
        s_wkv = s_wkv.transpose(3, 0, 1, 2)
    else:
        s_wkv = s_wkv.reshape(RW_N, RW_N, batch, heads).transpose(2, 3, 1, 0)
    if time_minor:
        y = _retile([y], False, batch)[0].reshape(seq // LANES, batch, d, LANES)
    else:
        y = _from_lanes(y, batch, seq, batch_lanes)
    out = _rwkv_post(
        y, c['bonus'], c['g'], h1, c['p3d'], 1, row(W['rw_ln_w'][0]), row(W['rw_ln_b'][0]), W['head_bd'],
        W['w_o'], row(W['ple_norm'][1]), W['ple_gate_b'][1], W['ple_proj_b'][1], row(W['final_norm']),
        time_minor=time_minor)
    return (out.reshape(batch, seq, d), c['ckv'].reshape(1, batch, seq, KV_LORA),
            c['kr'].reshape(1, batch, seq, ROPE_DIM), c['s_hg'][None], s_wkv[None], c['hn3'][:, -1][None])


def kernel(x_prompt, x_sample, cache_ckv, cache_krope, state_hgrn, state_wkv, state_shift, page_table, p_prompt, p_sample, mix_norm, ev_w_in, hg_lb_logits, hg_norm, mla_q_norm, mla_w_uq, mla_kv_norm, mla_w_uk, mla_w_uv, ev_w_out, rw_mu, rw_w_rkvg, rw_w0, rw_w1, rw_w2, rw_a0, rw_a1, rw_a2, rw_k_k, rw_k_a, rw_r_k, rw_ln_w, rw_ln_b, rw_w_o, ple_norm, ple_gate, ple_proj, final_norm):
    bf = lambda a: a.astype(BF16)
    d = x_prompt.shape[-1]
    w_in = ev_w_in[0]
    o = np.cumsum([0, HG_WIDTH, HG_WIDTH, HG_HEADS * HG_DV, HG_HEADS * HG_DV, Q_LORA, KV_LORA, ROPE_DIM, MLA_WIDTH])
    cq, ckv_w, kr_w, mg_w = (w_in[:, o[4]:o[5]], w_in[:, o[5]:o[6]], w_in[:, o[6]:o[7]], w_in[:, o[7]:o[8]])
    uq = mla_w_uq[0].reshape(Q_LORA, MLA_HEADS, NOPE_DIM + ROPE_DIM)
    uq_rope = uq[:, :, NOPE_DIM:]
    uv_bd = jnp.zeros((MLA_HEADS, KV_LORA, MLA_HEADS, V_DIM), F32)
    uv_bd = uv_bd.at[jnp.arange(MLA_HEADS), :, jnp.arange(MLA_HEADS), :].set(mla_w_uv[0].transpose(0, 2, 1))
    hid = np.arange(MXU_DIM) // RW_N
    W = dict(
        mix_norm=mix_norm, hg_lb_logits=hg_lb_logits, hg_norm=hg_norm, mla_q_norm=mla_q_norm,
        mla_kv_norm=mla_kv_norm, ple_norm=ple_norm, final_norm=final_norm, rw_mu=rw_mu, rw_w0=rw_w0, rw_a0=rw_a0,
        rw_k_k=rw_k_k, rw_k_a=rw_k_a, rw_ln_w=rw_ln_w, rw_ln_b=rw_ln_b, rw_r_k=rw_r_k.reshape(rw_r_k.shape[0], -1),
        w_in_hg=bf(w_in[:, :o[4]]),
        w_in_mla=bf(jnp.concatenate([cq, ckv_w, mg_w, kr_w, _swap_halves(kr_w)], axis=-1)),
        w_uq=bf(jnp.concatenate([uq[:, :, :NOPE_DIM].reshape(Q_LORA, -1), uq_rope.reshape(Q_LORA, -1),
                                 _swap_halves(uq_rope).reshape(Q_LORA, -1)], axis=-1)),
        w_uk=bf(mla_w_uk[0]),
        w_uv_bd=bf(uv_bd.reshape(MLA_HEADS * KV_LORA, MLA_WIDTH)),
        w_out=bf(ev_w_out[0]),
        ple_gate_b=bf(ple_gate), ple_proj_b=bf(ple_proj),
        w_r=bf(rw_w_rkvg[0, 0]), w_k=bf(rw_w_rkvg[0, 1]), w_v=bf(rw_w_rkvg[0, 2]), w_g=bf(rw_w_rkvg[0, 3]),
        w_w1=bf(rw_w1[0]), w_w2=bf(rw_w2[0]), w_a1=bf(rw_a1[0]), w_a2=bf(rw_a2[0]), w_o=bf(rw_w_o[0]),
        head_bd=jnp.asarray(hid[:, None] == hid[None, :], BF16),
    )
    bp, tp, _ = x_prompt.shape
    bs, ts, _ = x_sample.shape
    past_len = page_table.shape[1] * cache_ckv.shape[2]
    heads = d // RW_N
    paged = (cache_ckv.reshape(cache_ckv.shape[1:]), jnp.swapaxes(cache_krope.reshape(cache_krope.shape[1:]), 1, 2),
             page_table)
    cp = _even_mixers(x_prompt, jnp.arange(tp), W, None, None)
    cp = _rwkv_operands(cp, p_prompt, W, jnp.zeros((bp, heads, RW_N, RW_N), F32), jnp.zeros((bp, d), F32))
    cs = _even_mixers(x_sample, past_len + jnp.arange(ts), W, state_hgrn[0], paged)
    y_p, wkv_p = _wkv_scan(*cp['scan_in'], cp['scan_s0'], value_major=cp['batch_lanes'])
    yp, ckv_p, kr_p, hg_p, wkv_p, sh_p = _group_outputs(cp, y_p, wkv_p, W)
    cs['ctx'] = _stacked_ctx_to_rows(_attn_sample(*cs['attn_args']), bs, ts)
    cs = _rwkv_operands(cs, p_sample, W, state_wkv[0], state_shift[0])
    y_s, wkv_s = _wkv_scan(*cs['scan_in'], cs['scan_s0'], value_major=cs['batch_lanes'])
    ys, ckv_s, kr_s, hg_s, wkv_s, sh_s = _group_outputs(cs, y_s, wkv_s, W)
    return (yp, ys, ckv_p, kr_p, ckv_s, kr_s, hg_p, hg_s, wkv_p, wkv_s, sh_p, sh_s)
```

```python
import functools

import jax
import jax.numpy as jnp
import numpy as np
from jax import lax
from jax.experimental import pallas as pl
from jax.experimental.pallas import tpu as pltpu

F32 = jnp.float32
BF16 = jnp.bfloat16

NORM_EPS = 1e-6
HG_HEADS = 4
HG_DK = 128
HG_DV = 128
HG_WIDTH = HG_HEADS * HG_DK
MLA_HEADS = 8
Q_LORA = 384
KV_LORA = 256
NOPE_DIM = 64
ROPE_DIM = 32
V_DIM = 64
MLA_WIDTH = MLA_HEADS * V_DIM
MLA_SCALE = (NOPE_DIM + ROPE_DIM) ** -0.5
ROPE_THETA = 10000.0
RW_N = 64
RW_EPS = 64e-5

LANES = 128
SUBLANES = 8
MXU_DIM = 256
VMEM_LIMIT_BYTES = 56 * 1024 * 1024

ROW_TILE = 256
HG_CHUNK = 64
HG_BLOCK = SUBLANES
HG_TIME_BLOCK = 1024
HG_SEQS_PER_STEP = 2
ATT_BLOCK = 256
PAGES_PER_STEP = 64
PAGES_PER_BLOCK = 8
PAGE_RING = 5
SCAN_TIME_BLOCK = 64
RETILE_TIME_BLOCKS = 2


def _cparams(sem):
    return pltpu.CompilerParams(dimension_semantics=sem, vmem_limit_bytes=VMEM_LIMIT_BYTES)


def _rms(x, g):
    return x * lax.rsqrt(jnp.mean(x * x, axis=-1, keepdims=True) + NORM_EPS) * g


def _sigmoid(x):
    return 1.0 / (1.0 + jnp.exp(-x))


def _silu(x):
    return x * _sigmoid(x)


def _dot(a, b):
    return jnp.dot(a, b, preferred_element_type=F32)


def _dot_nt(a, b):
    return lax.dot_general(a, b, (((1,), (1,)), ((), ())), preferred_element_type=F32)


def _head_sum(x, bd):
    hi = x.astype(BF16)
    lo = (x - hi.astype(F32)).astype(BF16)
    outs = []
    for c in range(x.shape[-1] // MXU_DIM):
        sl = slice(c * MXU_DIM, (c + 1) * MXU_DIM)
        outs.append(_dot(hi[:, sl], bd) + _dot(lo[:, sl], bd))
    return jnp.concatenate(outs, axis=-1)


def _full(shape):
    nd = len(shape)
    return pl.BlockSpec(shape, lambda *_: (0,) * nd)


def _even_pre_kernel(x_ref, cos_ref, sin_ref, g_ref, lbl_ref, w1_ref, w2_ref, qg_ref, kvg_ref, wuq_ref,
                     wuk_ref, zhg_ref, qlat_ref, qrope_ref, ckv_ref, kr_ref, ckvb_ref, krb_ref, mg_ref,
                     *, lb_rows):
    hn = _rms(x_ref[...], g_ref[...]).astype(BF16)
    z2 = _dot(hn, w2_ref[...])
    z1 = _dot(hn, w1_ref[...])
    lg = lbl_ref[...]
    e = jnp.exp(lg - jnp.max(lg, axis=0, keepdims=True))
    p = e / jnp.sum(e, axis=0, keepdims=True)
    lb = jnp.sum(p[:lb_rows], axis=0, keepdims=True)
    W = HG_WIDTH
    f = lb + (1.0 - lb) * _sigmoid(z1[:, W:2 * W])
    zhg_ref[:, 0:W] = _silu(z1[:, 0:W])
    zhg_ref[:, W:2 * W] = jnp.log(f)
    zhg_ref[:, 2 * W:3 * W] = z1[:, 2 * W:3 * W]
    zhg_ref[:, 3 * W:4 * W] = _silu(z1[:, 3 * W:4 * W])
    o_kv = Q_LORA
    o_mg = o_kv + KV_LORA
    o_kr = o_mg + MLA_WIDTH
    o_krs = o_kr + ROPE_DIM
    cqn = _rms(z2[:, 0:o_kv], qg_ref[...]).astype(BF16)
    qf = _dot(cqn, wuq_ref[...])
    cos = cos_ref[...]
    sin = sin_ref[...]
    n_nope = MLA_HEADS * NOPE_DIM
    n_rope = MLA_HEADS * ROPE_DIM
    qr = (qf[:, n_nope:n_nope + n_rope] * cos + qf[:, n_nope + n_rope:n_nope + 2 * n_rope] * sin) * MLA_SCALE
    for h in range(MLA_HEADS):
        qrope_ref[h] = qr[:, h * ROPE_DIM:(h + 1) * ROPE_DIM].astype(BF16)
        qn = qf[:, h * NOPE_DIM:(h + 1) * NOPE_DIM].astype(BF16)
        qlat_ref[h] = (_dot(qn, wuk_ref[h]) * MLA_SCALE).astype(BF16)
    ckv = _rms(z2[:, o_kv:o_mg], kvg_ref[...])
    ckv_ref[...] = ckv
    ckvb_ref[...] = ckv.astype(BF16)
    kr = z2[:, o_kr:o_krs] * cos[:, :ROPE_DIM] + z2[:, o_krs:o_krs + ROPE_DIM] * sin[:, :ROPE_DIM]
    kr_ref[...] = kr
    krb_ref[...] = kr.astype(BF16)
    mg_ref[...] = _silu(z2[:, o_mg:o_kr]).astype(BF16)


def _even_pre(x2d, cos_tab, sin_tab, g, lb_logits, w1, w2, qg, kvg, wuq, wuk, *, lb_rows):
    n, d = x2d.shape
    tm = min(ROW_TILE, n)
    n_tab = cos_tab.shape[0] // tm
    row = lambda i: (i, 0)
    tab = lambda i: (i % n_tab, 0)
    out_shape = (
        jax.ShapeDtypeStruct((n, 4 * HG_WIDTH), F32),
        jax.ShapeDtypeStruct((MLA_HEADS, n, KV_LORA), BF16),
        jax.ShapeDtypeStruct((MLA_HEADS, n, ROPE_DIM), BF16),
        jax.ShapeDtypeStruct((n, KV_LORA), F32),
        jax.ShapeDtypeStruct((n, ROPE_DIM), F32),
        jax.ShapeDtypeStruct((n, KV_LORA), BF16),
        jax.ShapeDtypeStruct((n, ROPE_DIM), BF16),
        jax.ShapeDtypeStruct((n, MLA_WIDTH), BF16),
    )
    out_specs = (
        pl.BlockSpec((tm, 4 * HG_WIDTH), row),
        pl.BlockSpec((MLA_HEADS, tm, KV_LORA), lambda i: (0, i, 0)),
        pl.BlockSpec((MLA_HEADS, tm, ROPE_DIM), lambda i: (0, i, 0)),
        pl.BlockSpec((tm, KV_LORA), row),
        pl.BlockSpec((tm, ROPE_DIM), row),
        pl.BlockSpec((tm, KV_LORA), row),
        pl.BlockSpec((tm, ROPE_DIM), row),
        pl.BlockSpec((tm, MLA_WIDTH), row),
    )
    in_specs = [
        pl.BlockSpec((tm, d), row),
        pl.BlockSpec((tm, cos_tab.shape[1]), tab),
        pl.BlockSpec((tm, sin_tab.shape[1]), tab),
        _full(g.shape), _full(lb_logits.shape), _full(w1.shape), _full(w2.shape), _full(qg.shape),
        _full(kvg.shape), _full(wuq.shape), _full(wuk.shape),
    ]
    return pl.pallas_call(
        functools.partial(_even_pre_kernel, lb_rows=lb_rows),
        grid=(n // tm,), in_specs=in_specs, out_specs=out_specs, out_shape=out_shape,
        compiler_params=_cparams(("parallel",)), name="even_pre",
    )(x2d, cos_tab, sin_tab, g, lb_logits, w1, w2, qg, kvg, wuq, wuk)


def _tril_ones(n, block):
    r = lax.broadcasted_iota(jnp.int32, (n, n), 0)
    c = lax.broadcasted_iota(jnp.int32, (n, n), 1)
    return ((r >= c) & ((r // block) == (c // block))).astype(F32)


def _hgrn_exact_blocks(q, k, v, b, block):
    rows = q.shape[0]
    rid = lax.broadcasted_iota(jnp.int32, (rows, 1), 0) % block
    o = jnp.sum(q * k, axis=-1, keepdims=True) * v
    for d in range(1, block):
        kd = pltpu.roll(k, d, 0)
        bd = pltpu.roll(b, d, 0)
        vd = pltpu.roll(v, d, 0)
        w = jnp.sum(q * kd * jnp.exp(jnp.minimum(b - bd, 0.0)), axis=-1, keepdims=True)
        o = o + jnp.where(rid >= d, w, 0.0) * vd
    return o


def _hgrn_finish(o, g, gate):
    return (o * lax.rsqrt(jnp.mean(o * o, axis=-1, keepdims=True) + NORM_EPS) * g * gate).astype(BF16)


def _hgrn_prompt_kernel(q_ref, lf_ref, v_ref, gt_ref, g_ref, o_ref, s_ref, st_ref):
    C = HG_CHUNK
    nseq, tb, _ = q_ref.shape
    tril = _tril_ones(C, C)
    rid = lax.broadcasted_iota(jnp.int32, (C, 1), 0)
    rr = lax.broadcasted_iota(jnp.int32, (C, C), 0)
    cc = lax.broadcasted_iota(jnp.int32, (C, C), 1)
    g = g_ref[...]

    @pl.when(pl.program_id(1) == 0)
    def _():
        st_ref[...] = jnp.zeros(st_ref.shape, F32)

    def decay(x):
        x['k'] = 1.0 - jnp.exp(x['lf'])
        x['b'] = jnp.dot(tril, x['lf'], precision=lax.Precision.HIGHEST, preferred_element_type=F32)

    def scores(x):
        q, k, b = x['q'], x['k'], x['b']
        x['o'] = _dot_nt((q * jnp.exp(b)).astype(BF16), x['st'].astype(BF16))
        x['att'] = []
        m = HG_BLOCK
        while 2 * m <= C:
            nb = C // (2 * m)
            b3 = b.reshape(nb, 2 * m, HG_DK)
            ref = jnp.broadcast_to(b3[:, m - 1:m, :], (nb, 2 * m, HG_DK)).reshape(C, HG_DK)
            upper = (rid % (2 * m)) >= m
            qt = q * jnp.where(upper, jnp.exp(jnp.minimum(b - ref, 0.0)), 0.0)
            kt = k * jnp.where(upper, 0.0, jnp.exp(jnp.minimum(ref - b, 0.0)))
            x['att'].append((m, _dot_nt(qt.astype(BF16), kt.astype(BF16))))
            m *= 2
        bend = b[C - 1:C, :]
        x['st'] = x['st'] * jnp.exp(bend) + _dot(x['v'].T.astype(BF16), (k * jnp.exp(bend - b)).astype(BF16))
        x['o'] = x['o'] + _hgrn_exact_blocks(q, k, x['v'], b, HG_BLOCK)

    def values(x):
        att = jnp.zeros((C, C), F32)
        for m, a in x['att']:
            if 2 * m < C:
                a = jnp.where((rr // (2 * m)) == (cc // (2 * m)), a, 0.0)
            att = att + a
        x['pv'] = _dot(att.astype(BF16), x['v'].astype(BF16))

    def chunk(c, carry):
        sl = pl.ds(pl.multiple_of(c * C, C), C)
        xs = [dict(s=s, h=h, hs=slice(h * HG_DK, (h + 1) * HG_DK)) for s in range(nseq) for h in range(HG_HEADS)]
        for x in xs:
            s, h, hs = x['s'], x['h'], x['hs']
            x.update(st=st_ref[s, h], q=q_ref[s, sl, hs], lf=lf_ref[s, sl, hs], v=v_ref[s, sl, hs],
                     gate=gt_ref[s, sl, hs])
        for stage in (decay, scores, values):
            for x in xs:
                stage(x)
        for x in xs:
            o_ref[x['s'], sl, x['hs']] = _hgrn_finish(x['o'] + x['pv'], g, x['gate'])
            st_ref[x['s'], x['h']] = x['st']
        return carry

    lax.fori_loop(0, tb // C, chunk, 0)

    @pl.when(pl.program_id(1) == pl.num_programs(1) - 1)
    def _():
        for s in range(nseq):
            for h in range(HG_HEADS):
                s_ref[s, h] = st_ref[s, h].T


def _hgrn_prompt(zhg, g, batch, seq):
    nseq = HG_SEQS_PER_STEP if batch % HG_SEQS_PER_STEP == 0 else 1
    tb = min(HG_TIME_BLOCK // nseq, seq)
    z3 = zhg.reshape(batch, seq, 4 * HG_WIDTH)
    blk = lambda off: pl.BlockSpec((nseq, tb, HG_WIDTH), lambda b, i: (b, i, off))
    ohg, s_hg = pl.pallas_call(
        _hgrn_prompt_kernel,
        grid=(batch // nseq, seq // tb),
        in_specs=[blk(0), blk(1), blk(2), blk(3), _full(g.shape)],
        out_specs=(pl.BlockSpec((nseq, tb, HG_HEADS * HG_DV), lambda b, i: (b, i, 0)),
                   pl.BlockSpec((nseq, HG_HEADS, HG_DK, HG_DV), lambda b, i: (b, 0, 0, 0))),
        out_shape=(jax.ShapeDtypeStruct((batch, seq, HG_HEADS * HG_DV), BF16),
                   jax.ShapeDtypeStruct((batch, HG_HEADS, HG_DK, HG_DV), F32)),
        scratch_shapes=[pltpu.VMEM((nseq, HG_HEADS, HG_DV, HG_DK), F32)],
        compiler_params=_cparams(("parallel", "arbitrary")), name="hgrn_prompt",
    )(z3, z3, z3, z3, g)
    return ohg.reshape(batch * seq, HG_HEADS * HG_DV), s_hg


def _hgrn_sample_kernel(z_ref, s0_ref, g_ref, o_ref, s_ref, *, seq):
    rows = z_ref.shape[0]
    nb = rows // seq
    tril = _tril_ones(rows, seq)
    rb = lax.broadcasted_iota(jnp.int32, (rows, 1), 0) // seq
    cb = lax.broadcasted_iota(jnp.int32, (1, rows), 1) // seq
    g = g_ref[...]
    W = HG_WIDTH
    for h in range(HG_HEADS):
        hs = slice(h * HG_DK, (h + 1) * HG_DK)
        q = z_ref[:, hs]
        lf = z_ref[:, W + h * HG_DK:W + (h + 1) * HG_DK]
        v = z_ref[:, 2 * W + h * HG_DV:2 * W + (h + 1) * HG_DV]
        gate = z_ref[:, 3 * W + h * HG_DV:3 * W + (h + 1) * HG_DV]
        k = 1.0 - jnp.exp(lf)
        b = jnp.dot(tril, lf, precision=lax.Precision.HIGHEST, preferred_element_type=F32)
        o = _hgrn_exact_blocks(q, k, v, b, seq)
        qe = (q * jnp.exp(b)).astype(BF16)
        bt = b.T
        vb = v.astype(BF16)
        for i in range(nb):
            s0 = s0_ref[i, h]
            o = o + jnp.where(rb == i, _dot(qe, s0.astype(BF16)), 0.0)
            last = i * seq + seq - 1
            bend_row = b[last:last + 1, :]
            kt = jnp.where(rb == i, k * jnp.exp(bend_row - b), 0.0)
            ktt = kt.T.astype(BF16)
            s_ref[i, h] = s0 * jnp.exp(bt[:, last:last + 1]) + _dot(ktt, vb)
        o_ref[:, hs] = _hgrn_finish(o, g, gate)


def _hgrn_sample(zhg, s0, g, seq):
    n = zhg.shape[0]
    nb = 8
    rows = nb * seq
    return pl.pallas_call(
        functools.partial(_hgrn_sample_kernel, seq=seq),
        grid=(n // rows,),
        in_specs=[pl.BlockSpec((rows, 4 * HG_WIDTH), lambda i: (i, 0)),
                  pl.BlockSpec((nb, HG_HEADS, HG_DK, HG_DV), lambda i: (i, 0, 0, 0)),
                  _full(g.shape)],
        out_specs=(pl.BlockSpec((rows, HG_HEADS * HG_DV), lambda i: (i, 0)),
                   pl.BlockSpec((nb, HG_HEADS, HG_DK, HG_DV), lambda i: (i, 0, 0, 0))),
        out_shape=(jax.ShapeDtypeStruct((n, HG_HEADS * HG_DV), BF16),
                   jax.ShapeDtypeStruct(s0.shape, F32)),
        compiler_params=_cparams(("parallel",)), name="hgrn_sample",
    )(zhg, s0, g)


def _lane_tile(x, width):
    if width <= LANES:
        return x[:, :width]
    return jnp.concatenate([x] * (width // LANES), axis=1)


def _softmax_init(m_ref, l_ref, acc_ref):
    m_ref[...] = jnp.full(m_ref.shape, -jnp.inf, F32)
    l_ref[...] = jnp.zeros(l_ref.shape, F32)
    acc_ref[...] = jnp.zeros(acc_ref.shape, F32)


def _online_step(s, kv, m_old, l_old, acc_old):
    m_new = jnp.maximum(m_old, jnp.max(s, axis=-1, keepdims=True))
    alpha = jnp.exp(m_old - m_new)
    p = jnp.exp(s - _lane_tile(m_new, s.shape[1]))
    l_new = alpha * l_old + jnp.sum(p, axis=-1, keepdims=True)
    acc_new = _lane_tile(alpha, acc_old.shape[1]) * acc_old + _dot(p.astype(BF16), kv)
    return m_new, l_new, acc_new


def _online_update(s, kv, rows, m_ref, l_ref, acc_ref):
    m_ref[rows, :], l_ref[rows, :], acc_ref[rows, :] = _online_step(
        s, kv, m_ref[rows, :], l_ref[rows, :], acc_ref[rows, :])


def _attn_prompt_kernel(ql_ref, qr_ref, kv_ref, kr_ref, o_ref, m_ref, l_ref, acc_ref):
    tq = ql_ref.shape[1]
    i = pl.program_id(1)
    _softmax_init(m_ref, l_ref, acc_ref)
    n_split = 2
    hs = MLA_HEADS // n_split
    mh = hs * tq
    qpos = lax.broadcasted_iota(jnp.int32, (mh, tq), 0) % tq
    kpos = lax.broadcasted_iota(jnp.int32, (mh, tq), 1)

    def block(j, masked):
        sl = pl.ds(pl.multiple_of(j * tq, tq), tq)
        kv = kv_ref[sl, :]
        kr = kr_ref[sl, :]
        scores = []
        for r in range(n_split):
            ql = ql_ref[r * hs:(r + 1) * hs].reshape(mh, KV_LORA)
            qr = qr_ref[r * hs:(r + 1) * hs].reshape(mh, ROPE_DIM)
            s = _dot_nt(ql, kv) + _dot_nt(qr, kr)
            scores.append(jnp.where(kpos <= qpos, s, -jnp.inf) if masked else s)
        for r, s in enumerate(scores):
            _online_update(s, kv, slice(r * mh, (r + 1) * mh), m_ref, l_ref, acc_ref)

    def body(j, carry):
        block(j, False)
        return carry

    lax.fori_loop(0, i, body, 0)
    block(i, True)
    for h in range(MLA_HEADS):
        rows = slice(h * tq, (h + 1) * tq)
        out = acc_ref[rows, :] / _lane_tile(l_ref[rows, :], KV_LORA)
        o_ref[:, h * KV_LORA:(h + 1) * KV_LORA] = out.astype(BF16)


def _attn_prompt(qlat, qrope, ckvb, krb, batch, seq):
    tq = ATT_BLOCK
    nq = seq // tq
    rows = MLA_HEADS * tq
    return pl.pallas_call(
        _attn_prompt_kernel,
        grid=(batch, nq),
        in_specs=[pl.BlockSpec((MLA_HEADS, tq, KV_LORA), lambda b, i: (0, b * nq + i, 0)),
                  pl.BlockSpec((MLA_HEADS, tq, ROPE_DIM), lambda b, i: (0, b * nq + i, 0)),
                  pl.BlockSpec((seq, KV_LORA), lambda b, i: (b, 0)),
                  pl.BlockSpec((seq, ROPE_DIM), lambda b, i: (b, 0))],
        out_specs=pl.BlockSpec((tq, MLA_HEADS * KV_LORA), lambda b, i: (b * nq + i, 0)),
        out_shape=jax.ShapeDtypeStruct((batch * seq, MLA_HEADS * KV_LORA), BF16),
        scratch_shapes=[pltpu.VMEM((rows, LANES), F32), pltpu.VMEM((rows, LANES), F32),
                        pltpu.VMEM((rows, KV_LORA), F32)],
        compiler_params=_cparams(("parallel", "arbitrary")), name="attn_prompt",
    )(qlat, qrope, ckvb, krb)


class _PageRing:
    def __init__(self, pt_ref, ckv_hbm, kr_hbm, kvbuf, krbuf, sems, units_per_seq):
        self.pt_ref, self.ckv_hbm, self.kr_hbm = pt_ref, ckv_hbm, kr_hbm
        self.kvbuf, self.krbuf, self.sems = kvbuf, krbuf, sems
        self.units_per_seq = units_per_seq
        self.pages = krbuf.shape[1]
        self.page = kvbuf.shape[1] // self.pages

    def _copies(self, u):
        slot = u % PAGE_RING
        seq_id = u // self.units_per_seq
        first = (u % self.units_per_seq) * self.pages
        out = []
        for j in range(self.pages):
            pg = self.pt_ref[seq_id, first + j]
            out.append(pltpu.make_async_copy(
                self.ckv_hbm.at[pg], self.kvbuf.at[slot, pl.ds(j * self.page, self.page), :], self.sems.at[0, slot]))
            out.append(pltpu.make_async_copy(self.kr_hbm.at[pg], self.krbuf.at[slot, j], self.sems.at[1, slot]))
        return out

    def prime(self, total):
        for u in range(PAGE_RING - 1):
            @pl.when(u < total)
            def _():
                for c in self._copies(u):
                    c.start()

    def wait(self, u):
        for c in self._copies(u):
            c.wait()

    def refill(self, u, total):
        @pl.when(u + PAGE_RING - 1 < total)
        def _():
            for c in self._copies(u + PAGE_RING - 1):
                c.start()

    def attend(self, u, ql, qr, state):
        slot = u % PAGE_RING
        sub = min(PAGES_PER_BLOCK, self.pages)
        blocks = []
        for c in range(0, self.pages, sub):
            kv = self.kvbuf[slot, c * self.page:(c + sub) * self.page, :].astype(BF16)
            kr = jnp.concatenate([self.krbuf[slot, c + j].astype(BF16) for j in range(sub)], axis=1)
            blocks.append((_dot_nt(ql, kv) + _dot(qr, kr), kv))
        for s, kv in blocks:
            state = _online_step(s, kv, *state)
        return state


def _attend_new_rows(ql, qr, kvn, krn, seq, state):
    sn = _dot_nt(ql, kvn) + _dot(qr, krn)
    qpos = lax.broadcasted_iota(jnp.int32, sn.shape, 0) % seq
    kpos = lax.broadcasted_iota(jnp.int32, sn.shape, 1)
    _, l, acc = _online_step(jnp.where(kpos <= qpos, sn, -jnp.inf), kvn, *state)
    return (acc / _lane_tile(l, KV_LORA)).astype(BF16)


def _attn_sample_kernel(pt_ref, ql_ref, qr_ref, kvn_ref, krn_ref, ckv_hbm, kr_hbm, o_ref,
                        kvbuf, krbuf, sems, m_ref, l_ref, acc_ref, *, seq):
    n_groups = pl.num_programs(1)
    g = pl.program_id(1)
    unit = pl.program_id(0) * n_groups + g
    total = pl.num_programs(0) * n_groups
    ring = _PageRing(pt_ref, ckv_hbm, kr_hbm, kvbuf, krbuf, sems, n_groups)

    @pl.when(unit == 0)
    def _():
        ring.prime(total)

    @pl.when(g == 0)
    def _():
        _softmax_init(m_ref, l_ref, acc_ref)

    ring.wait(unit)
    ql = ql_ref[...]
    qr = qr_ref[...]
    state = ring.attend(unit, ql, qr, (m_ref[...], l_ref[...], acc_ref[...]))
    m_ref[...], l_ref[...], acc_ref[...] = state
    ring.refill(unit, total)

    @pl.when(g == n_groups - 1)
    def _():
        o_ref[...] = _attend_new_rows(ql, qr, kvn_ref[...], krn_ref[...], seq, state)


def _page_ring_scratch(pages, page, rows):
    return [pltpu.VMEM((PAGE_RING, pages * page, KV_LORA), F32),
            pltpu.VMEM((PAGE_RING, pages, ROPE_DIM, page), F32),
            pltpu.SemaphoreType.DMA((2, PAGE_RING)),
            pltpu.VMEM((rows, LANES), F32), pltpu.VMEM((rows, LANES), F32), pltpu.VMEM((rows, KV_LORA), F32)]


def _attn_sample(page_table, qlat, qrope, kv_new, kr_new_t, cache_ckv, cache_krope_t, seq):
    batch, n_pages = page_table.shape
    G = min(PAGES_PER_STEP, n_pages)
    page = cache_ckv.shape[1]
    rows = qlat.shape[1]
    npad = kv_new.shape[1]

    per_b = lambda r, width: pl.BlockSpec((None, r, width), lambda b, g, pt: (b, 0, 0))
    in_hbm = pl.BlockSpec(memory_space=pl.ANY)
    grid_spec = pltpu.PrefetchScalarGridSpec(
        num_scalar_prefetch=1,
        grid=(batch, n_pages // G),
        in_specs=[per_b(rows, KV_LORA), per_b(rows, ROPE_DIM), per_b(npad, KV_LORA), per_b(ROPE_DIM, npad),
                  in_hbm, in_hbm],
        out_specs=per_b(rows, KV_LORA),
        scratch_shapes=_page_ring_scratch(G, page, rows),
    )
    return pl.pallas_call(
        functools.partial(_attn_sample_kernel, seq=seq),
        grid_spec=grid_spec,
        out_shape=jax.ShapeDtypeStruct((batch, rows, KV_LORA), BF16),
        compiler_params=_cparams(("arbitrary", "arbitrary")), name="attn_sample",
    )(page_table, qlat, qrope, kv_new, kr_new_t, cache_ckv, cache_krope_t)


def _ple(h, emb, png, pgw):
    gate = _sigmoid(_dot(_rms(h, png).astype(BF16), pgw))
    return h + gate * emb


def _even_post_kernel(x_ref, ohg_ref, ctx_ref, mg_ref, p_ref, wuv_ref, wout_ref, png_ref, pgw_ref, ppw_ref,
                      ng_ref, h_ref, hn_ref, last_ref):
    n_hg = HG_HEADS * HG_DV
    ctx_up = _dot(ctx_ref[...], wuv_ref[...])
    hg_out = _dot(ohg_ref[...], wout_ref[0:n_hg, :])
    emb = _dot(p_ref[...].astype(BF16), ppw_ref[...])
    o_mla = (ctx_up * mg_ref[...]).astype(BF16)
    h = x_ref[...] + hg_out + _dot(o_mla, wout_ref[n_hg:, :])
    h = _ple(h, emb, png_ref[...], pgw_ref[...])
    h_ref[...] = h
    hn = _rms(h, ng_ref[...])
    hn_ref[...] = hn
    last_ref[...] = hn[-1:, :]


def _layer_rows(p3d, layer, tm):
    return pl.BlockSpec((None, tm, p3d.shape[2]), lambda i: (layer, i, 0))


def _even_post(x2d, ohg, ctx, mg, p3d, layer, wuv, wout, png, pgw, ppw, ng):
    n, d = x2d.shape
    tm = min(ROW_TILE, n)
    row = lambda a: pl.BlockSpec((tm, a.shape[1]), lambda i: (i, 0))
    return pl.pallas_call(
        _even_post_kernel,
        grid=(n // tm,),
        in_specs=[row(x2d), row(ohg), row(ctx), row(mg), _layer_rows(p3d, layer, tm), _full(wuv.shape),
                  _full(wout.shape), _full(png.shape), _full(pgw.shape), _full(ppw.shape), _full(ng.shape)],
        out_specs=(row(x2d), row(x2d), pl.BlockSpec((None, 1, d), lambda i: (i, 0, 0))),
        out_shape=(jax.ShapeDtypeStruct((n, d), F32), jax.ShapeDtypeStruct((n, d), F32),
                   jax.ShapeDtypeStruct((n // tm, 1, d), F32)),
        compiler_params=_cparams(("parallel",)), name="even_post",
    )(x2d, ohg, ctx, mg, p3d, wuv, wout, png, pgw, ppw, ng)


def _rwkv_pre_kernel(hn_ref, pv_ref, mu_ref, wr_ref, wk_ref, wv_ref, wg_ref, w0_ref, w1_ref, w2_ref, a0_ref,
                     a1_ref, a2_ref, kk_ref, ka_ref, rk_ref, bd_ref,
                     r_ref, w_ref, k_ref, v_ref, na_ref, b_ref, bonus_ref, g_ref, *, time_minor):
    def put(ref, x):
        if time_minor:
            xt = x.T
            for j in range(ref.shape[0]):
                ref[j] = xt[:, j * LANES:(j + 1) * LANES]
        else:
            ref[...] = x

    hn = hn_ref[...]
    if time_minor:
        first = lax.broadcasted_iota(jnp.int32, (hn.shape[0], 1), 0) == 0
        prev = jnp.where(first, pv_ref[...], pltpu.roll(hn, 1, 0))
    else:
        prev = pv_ref[...]
    dlt = prev - hn
    mix = lambda j: (hn + dlt * mu_ref[j:j + 1, :]).astype(BF16)
    w_mid = _dot(mix(4), w1_ref[...])
    a_mid = _dot(mix(5), a1_ref[...])
    r = _dot(mix(0), wr_ref[...])
    k = _dot(mix(1), wk_ref[...])
    v = _dot(mix(2), wv_ref[...])
    g = _dot(mix(3), wg_ref[...])
    wl = w0_ref[...] + _dot(jnp.tanh(w_mid).astype(BF16), w2_ref[...])
    w_log = -(jnp.maximum(-wl, 0.0) + jnp.log(1.0 + jnp.exp(-jnp.abs(wl)))) - 0.5
    a = _sigmoid(a0_ref[...] + _dot(a_mid.astype(BF16), a2_ref[...]))
    kk = k * kk_ref[...]
    kk = kk / jnp.maximum(jnp.sqrt(_head_sum(kk * kk, bd_ref[...])), 1e-12)
    k_mod = k * (1.0 + (a - 1.0) * ka_ref[...])
    put(r_ref, r)
    put(w_ref, jnp.exp(-jnp.exp(w_log)))
    put(k_ref, k_mod)
    put(v_ref, v)
    put(na_ref, -kk)
    put(b_ref, kk * a)
    bonus_ref[...] = (_head_sum(r * k_mod * rk_ref[...], bd_ref[...]) * v).astype(BF16)
    g_ref[...] = _silu(g).astype(BF16)


def _rwkv_pre(hn, prev, mu, wr, wk, wv, wg, w0, w1, w2, a0, a1, a2, kk, ka, rk, bd, *, batch, time_minor):
    n, d = hn.shape
    tm = min(ROW_TILE, n)
    seq = n // batch
    nt = seq // tm if time_minor else 1
    row = pl.BlockSpec((tm, d), lambda i: (i, 0))
    ws = [mu, wr, wk, wv, wg, w0, w1, w2, a0, a1, a2, kk, ka, rk, bd]
    if time_minor:
        scan_spec = pl.BlockSpec((tm // LANES, None, d, LANES), lambda i: (i % nt, i // nt, 0, 0))
        scan_shape = jax.ShapeDtypeStruct((seq // LANES, batch, d, LANES), F32)
        prev_spec = pl.BlockSpec((None, 1, d), lambda i: (i, 0, 0))
    else:
        scan_spec, scan_shape, prev_spec = row, jax.ShapeDtypeStruct((n, d), F32), row
    return pl.pallas_call(
        functools.partial(_rwkv_pre_kernel, time_minor=time_minor),
        grid=(n // tm,),
        in_specs=[row, prev_spec] + [_full(w.shape) for w in ws],
        out_specs=(scan_spec,) * 6 + (row, row),
        out_shape=(scan_shape,) * 6 + (jax.ShapeDtypeStruct((n, d), BF16),) * 2,
        compiler_params=_cparams(("parallel",)), name="rwkv_pre",
    )(hn, prev, *ws)


def _retile_kernel(*refs):
    n = len(refs) // 2
    for src, dst in zip(refs[:n], refs[n:]):
        kb = src.shape[-2]
        rows = int(np.prod(src.shape)) // LANES
        src2 = src.reshape(rows, LANES)
        dst2 = dst.reshape(rows, LANES)
        for first in range(0, rows, LANES * kb):
            for j in range(kb):
                sel = pl.ds(first + j, LANES, stride=kb)
                dst2[sel, :] = src2[sel, :].T


def _retile(arrays, to_scan, batch=None):
    kb = SUBLANES
    n = len(arrays)
    if to_scan:
        tb, nb, nh, k, _ = arrays[0].shape
        out_shape = jax.ShapeDtypeStruct((tb * LANES, k, LANES), F32)
    else:
        t, k, _ = arrays[0].shape
        nb, tb, nh = batch, t // LANES, LANES // batch
        out_shape = jax.ShapeDtypeStruct((tb, nb, nh, k, LANES), F32)
    u = RETILE_TIME_BLOCKS if tb % RETILE_TIME_BLOCKS == 0 else 1
    tiled = pl.BlockSpec((u, nb, nh, kb, LANES), lambda j, i: (i, 0, 0, j, 0))
    scan = pl.BlockSpec((u * LANES, kb, LANES), lambda j, i: (i, j, 0))
    return pl.pallas_call(
        _retile_kernel,
        grid=(k // kb, tb // u),
        in_specs=[tiled if to_scan else scan] * n,
        out_specs=(scan if to_scan else tiled,) * n,
        out_shape=(out_shape,) * n,
        compiler_params=_cparams(("parallel", "parallel")), name="retile",
    )(*arrays)


def _wkv_first_sa(s_ref, a_ref):
    sa = s_ref[0] * a_ref[0, 0:1, :]
    for k in range(1, s_ref.shape[0]):
        sa = sa + s_ref[k] * a_ref[0, k:k + 1, :]
    return sa


def _wkv_step(t, sa, r_ref, w_ref, k_ref, v_ref, a_ref, b_ref, y_ref, s_ref):
    tt = r_ref.shape[0]
    tn = jnp.minimum(t + 1, tt - 1)
    row = lambda ref, k: ref[t, k:k + 1, :]
    v = v_ref[t]
    y = jnp.zeros_like(v)
    sa_next = jnp.zeros_like(v)
    for k in range(s_ref.shape[0]):
        s = s_ref[k] * row(w_ref, k) + sa * row(b_ref, k) + v * row(k_ref, k)
        s_ref[k] = s
        y = y + s * row(r_ref, k)
        sa_next = sa_next + s * a_ref[tn, k:k + 1, :]
    y_ref[t] = y
    return sa_next


def _wkv_scan_kernel(r_ref, w_ref, k_ref, v_ref, a_ref, b_ref, s0_ref, y_ref, so_ref, s_ref, *, value_major):
    @pl.when(pl.program_id(1) == 0)
    def _():
        s_ref[...] = jnp.swapaxes(s0_ref[...], 0, 1) if value_major else s0_ref[...]

    step = functools.partial(_wkv_step, r_ref=r_ref, w_ref=w_ref, k_ref=k_ref, v_ref=v_ref, a_ref=a_ref,
                             b_ref=b_ref, y_ref=y_ref, s_ref=s_ref)
    lax.fori_loop(0, r_ref.shape[0], step, _wkv_first_sa(s_ref, a_ref))

    @pl.when(pl.program_id(1) == pl.num_programs(1) - 1)
    def _():
        so_ref[...] = jnp.swapaxes(s_ref[...], 0, 1) if value_major else s_ref[...]


def _wkv_scan(r, w, k, v, a, b, s0, *, value_major):
    t, n, lanes = r.shape
    tt = min(SCAN_TIME_BLOCK, t)
    seq = pl.BlockSpec((tt, n, LANES), lambda g, i: (i, 0, g))
    if value_major:
        nb = s0.shape[3] // LANES
        st = pl.BlockSpec((None, n, n, LANES), lambda g, i: (g // nb, 0, 0, g % nb))
    else:
        st = pl.BlockSpec((n, n, LANES), lambda g, i: (0, 0, g))
    return pl.pallas_call(
        functools.partial(_wkv_scan_kernel, value_major=value_major),
        grid=(lanes // LANES, t // tt),
        in_specs=[seq] * 6 + [st],
        out_specs=(seq, st),
        out_shape=(jax.ShapeDtypeStruct((t, n, lanes), F32), jax.ShapeDtypeStruct(s0.shape, F32)),
        scratch_shapes=[pltpu.VMEM((n, n, LANES), F32)],
        compiler_params=_cparams(("parallel", "arbitrary")), name="wkv_scan",
    )(r, w, k, v, a, b, s0)


def _rwkv_post_kernel(y_ref, bonus_ref, g_ref, h_ref, p_ref, lnw_ref, lnb_ref, bd_ref, wo_ref, png_ref,
                      pgw_ref, ppw_ref, fg_ref, o_ref, *, time_minor):
    bd = bd_ref[...]
    emb = _dot(p_ref[...].astype(BF16), ppw_ref[...])
    if time_minor:
        y = jnp.concatenate([y_ref[j] for j in range(y_ref.shape[0])], axis=1).T
    else:
        y = y_ref[...]
    inv_n = 1.0 / RW_N
    yc = y - _head_sum(y, bd) * inv_n
    var = _head_sum(yc * yc, bd) * inv_n
    yn = yc * lax.rsqrt(var + RW_EPS) * lnw_ref[...] + lnb_ref[...]
    mix = ((yn + bonus_ref[...]) * g_ref[...]).astype(BF16)
    h = h_ref[...] + _dot(mix, wo_ref[...])
    h = _ple(h, emb, png_ref[...], pgw_ref[...])
    o_ref[...] = _rms(h, fg_ref[...])


def _rwkv_post(y, bonus, g, h, p3d, layer, lnw, lnb, bd, wo, png, pgw, ppw, fg, *, time_minor):
    n, d = h.shape
    tm = min(ROW_TILE, n)
    row = lambda a: pl.BlockSpec((tm, a.shape[1]), lambda i: (i, 0))
    if time_minor:
        nt = y.shape[0] * LANES // tm
        y_spec = pl.BlockSpec((tm // LANES, None, d, LANES), lambda i: (i % nt, i // nt, 0, 0))
    else:
        y_spec = row(y)
    ws = [lnw, lnb, bd, wo, png, pgw, ppw, fg]
    return pl.pallas_call(
        functools.partial(_rwkv_post_kernel, time_minor=time_minor),
        grid=(n // tm,),
        in_specs=[y_spec, row(bonus), row(g), row(h), _layer_rows(p3d, layer, tm)] + [_full(w.shape) for w in ws],
        out_specs=row(h),
        out_shape=jax.ShapeDtypeStruct((n, d), F32),
        compiler_params=_cparams(("parallel",)), name="rwkv_post",
    )(y, bonus, g, h, p3d, *ws)


def _rope_tables(pos, rows):
    half = ROPE_DIM // 2
    inv = ROPE_THETA ** (-jnp.arange(half, dtype=F32) / half)
    ang = pos.astype(F32)[:, None] * inv[None, :]
    cos = jnp.cos(ang)
    sin = jnp.sin(ang)
    cos = jnp.tile(jnp.concatenate([cos, cos], axis=-1), (rows // pos.shape[0], MLA_HEADS))
    sin = jnp.tile(jnp.concatenate([-sin, sin], axis=-1), (rows // pos.shape[0], MLA_HEADS))
    return cos, sin


def _swap_halves(w):
    half = w.shape[-1] // 2
    return jnp.concatenate([w[..., half:], w[..., :half]], axis=-1)


def _to_lanes(x, batch, seq, batch_minor):
    heads = x.shape[1] // RW_N
    order = (1, 3, 2, 0) if batch_minor else (1, 3, 0, 2)
    return x.reshape(batch, seq, heads, RW_N).transpose(order).reshape(seq, RW_N, batch * heads)


def _from_lanes(y, batch, seq, batch_minor):
    heads = y.shape[2] // batch
    if batch_minor:
        y = y.reshape(seq, RW_N, heads, batch).transpose(3, 0, 2, 1)
    else:
        y = y.reshape(seq, RW_N, batch, heads).transpose(2, 0, 3, 1)
    return y.reshape(batch * seq, heads * RW_N)


def _row(v):
    return v.reshape(1, -1)


def _even_mixers(x, pos, W, hg_s0, paged):
    batch, seq, d = x.shape
    n = batch * seq
    x2d = x.reshape(n, d)
    row = _row
    cos_tab, sin_tab = _rope_tables(pos, max(seq, min(ROW_TILE, n)))
    zhg, qlat, qrope, ckv, kr, ckvb, krb, mg = _even_pre(
        x2d, cos_tab, sin_tab, row(W['mix_norm'][0]), W['hg_lb_logits'], W['w_in_hg'], W['w_in_mla'],
        row(W['mla_q_norm'][0]), row(W['mla_kv_norm'][0]), W['w_uq'], W['w_uk'], lb_rows=1)
    g_hg = row(W['hg_norm'][0])
    c = dict(batch=batch, seq=seq, x2d=x2d, mg=mg, ckv=ckv, kr=kr)
    if paged is None:
        c['ohg'], c['s_hg'] = _hgrn_prompt(zhg, g_hg, batch, seq)
        c['ctx'] = _attn_prompt(qlat, qrope, ckvb, krb, batch, seq)
    else:
        c['ohg'], c['s_hg'] = _hgrn_sample(zhg, hg_s0, g_hg, seq)
        cache_ckv, cache_krope_t, page_table = paged
        stack = lambda q: q.reshape(MLA_HEADS, batch, seq, q.shape[-1]).transpose(1, 0, 2, 3).reshape(
            batch, MLA_HEADS * seq, q.shape[-1])
        pad = lambda a: jnp.pad(a.reshape(batch, seq, a.shape[-1]), ((0, 0), (0, 2 * SUBLANES - seq), (0, 0)))
        c['attn_args'] = (page_table, stack(qlat), stack(qrope), pad(ckvb), pad(krb).transpose(0, 2, 1),
                          cache_ckv, cache_krope_t, seq)
    return c


def _stacked_ctx_to_rows(ctx, batch, seq):
    return ctx.reshape(batch, MLA_HEADS, seq, KV_LORA).transpose(0, 2, 1, 3).reshape(batch * seq, MLA_HEADS * KV_LORA)


def _rwkv_operands(c, p, W, wkv_s0, shift_s0):
    batch, seq, x2d = c['batch'], c['seq'], c['x2d']
    n, d = x2d.shape
    row = _row
    p3d = p.reshape(p.shape[0], n, p.shape[-1])
    h1, hn1, tile_last = _even_post(
        x2d, c['ohg'], c['ctx'], c['mg'], p3d, 0, W['w_uv_bd'], W['w_out'], row(W['ple_norm'][0]),
        W['ple_gate_b'][0], W['ple_proj_b'][0], row(W['mix_norm'][1]))
    hn3 = hn1.reshape(batch, seq, d)
    heads = d // RW_N
    batch_lanes = batch % LANES == 0
    time_minor = seq % LANES == 0 and batch * heads == LANES
    if time_minor:
        tile_last = tile_last.reshape(batch, -1, d)
        prev = jnp.concatenate([shift_s0[:, None, :], tile_last[:, :-1]], axis=1).reshape(-1, 1, d)
    else:
        prev = jnp.concatenate([shift_s0[:, None, :], hn3[:, :-1]], axis=1).reshape(n, d)
    *scan_in, bonus, g = _rwkv_pre(
        hn1, prev, W['rw_mu'][0], W['w_r'], W['w_k'], W['w_v'], W['w_g'], row(W['rw_w0'][0]), W['w_w1'],
        W['w_w2'], row(W['rw_a0'][0]), W['w_a1'], W['w_a2'], row(W['rw_k_k'][0]), row(W['rw_k_a'][0]),
        row(W['rw_r_k'][0]), W['head_bd'], batch=batch, time_minor=time_minor)
    if time_minor:
        scan_in = _retile([a.reshape(seq // LANES, batch, heads, RW_N, LANES) for a in scan_in], True)
    else:
        scan_in = [_to_lanes(a, batch, seq, batch_lanes) for a in scan_in]
    if batch_lanes:
        scan_s0 = wkv_s0.transpose(1, 2, 3, 0)
    else:
        scan_s0 = wkv_s0.transpose(3, 2, 0, 1).reshape(RW_N, RW_N, -1)
    c.update(scan_in=scan_in, scan_s0=scan_s0, batch_lanes=batch_lanes, time_minor=time_minor, bonus=bonus, g=g,
             h1=h1, hn3=hn3, p3d=p3d)
    return c


def _group_outputs(c, y, s_wkv, W):
    batch, seq, h1 = c['batch'], c['seq'], c['h1']
    d = h1.shape[1]
    heads = d // RW_N
    row = _row
    time_minor, batch_lanes = c['time_minor'], c['batch_lanes']
    if batch_lanes:
        s_wkv = s_wkv.transpose(3, 0, 1, 2)
    else:
        s_wkv = s_wkv.reshape(RW_N, RW_N, batch, heads).transpose(2, 3, 1, 0)
    if time_minor:
        y = _retile([y], False, batch)[0].reshape(seq // LANES, batch, d, LANES)
    else:
        y = _from_lanes(y, batch, seq, batch_lanes)
    out = _rwkv_post(
        y, c['bonus'], c['g'], h1, c['p3d'], 1, row(W['rw_ln_w'][0]), row(W['rw_ln_b'][0]), W['head_bd'],
        W['w_o'], row(W['ple_norm'][1]), W['ple_gate_b'][1], W['ple_proj_b'][1], row(W['final_norm']),
        time_minor=time_minor)
    return (out.reshape(batch, seq, d), c['ckv'].reshape(1, batch, seq, KV_LORA),
            c['kr'].reshape(1, batch, seq, ROPE_DIM), c['s_hg'][None], s_wkv[None], c['hn3'][:, -1][None])


def kernel(x_prompt, x_sample, cache_ckv, cache_krope, state_hgrn, state_wkv, state_shift, page_table, p_prompt, p_sample, mix_norm, ev_w_in, hg_lb_logits, hg_norm, mla_q_norm, mla_w_uq, mla_kv_norm, mla_w_uk, mla_w_uv, ev_w_out, rw_mu, rw_w_rkvg, rw_w0, rw_w1, rw_w2, rw_a0, rw_a1, rw_a2, rw_k_k, rw_k_a, rw_r_k, rw_ln_w, rw_ln_b, rw_w_o, ple_norm, ple_gate, ple_proj, final_norm):
    bf = lambda a: a.astype(BF16)
    d = x_prompt.shape[-1]
    w_in = ev_w_in[0]
    o = np.cumsum([0, HG_WIDTH, HG_WIDTH, HG_HEADS * HG_DV, HG_HEADS * HG_DV, Q_LORA, KV_LORA, ROPE_DIM, MLA_WIDTH])
    cq, ckv_w, kr_w, mg_w = (w_in[:, o[4]:o[5]], w_in[:, o[5]:o[6]], w_in[:, o[6]:o[7]], w_in[:, o[7]:o[8]])
    uq = mla_w_uq[0].reshape(Q_LORA, MLA_HEADS, NOPE_DIM + ROPE_DIM)
    uq_rope = uq[:, :, NOPE_DIM:]
    uv_bd = jnp.zeros((MLA_HEADS, KV_LORA, MLA_HEADS, V_DIM), F32)
    uv_bd = uv_bd.at[jnp.arange(MLA_HEADS), :, jnp.arange(MLA_HEADS), :].set(mla_w_uv[0].transpose(0, 2, 1))
    hid = np.arange(MXU_DIM) // RW_N
    W = dict(
        mix_norm=mix_norm, hg_lb_logits=hg_lb_logits, hg_norm=hg_norm, mla_q_norm=mla_q_norm,
        mla_kv_norm=mla_kv_norm, ple_norm=ple_norm, final_norm=final_norm, rw_mu=rw_mu, rw_w0=rw_w0, rw_a0=rw_a0,
        rw_k_k=rw_k_k, rw_k_a=rw_k_a, rw_ln_w=rw_ln_w, rw_ln_b=rw_ln_b, rw_r_k=rw_r_k.reshape(rw_r_k.shape[0], -1),
        w_in_hg=bf(w_in[:, :o[4]]),
        w_in_mla=bf(jnp.concatenate([cq, ckv_w, mg_w, kr_w, _swap_halves(kr_w)], axis=-1)),
        w_uq=bf(jnp.concatenate([uq[:, :, :NOPE_DIM].reshape(Q_LORA, -1), uq_rope.reshape(Q_LORA, -1),
                                 _swap_halves(uq_rope).reshape(Q_LORA, -1)], axis=-1)),
        w_uk=bf(mla_w_uk[0]),
        w_uv_bd=bf(uv_bd.reshape(MLA_HEADS * KV_LORA, MLA_WIDTH)),
        w_out=bf(ev_w_out[0]),
        ple_gate_b=bf(ple_gate), ple_proj_b=bf(ple_proj),
        w_r=bf(rw_w_rkvg[0, 0]), w_k=bf(rw_w_rkvg[0, 1]), w_v=bf(rw_w_rkvg[0, 2]), w_g=bf(rw_w_rkvg[0, 3]),
        w_w1=bf(rw_w1[0]), w_w2=bf(rw_w2[0]), w_a1=bf(rw_a1[0]), w_a2=bf(rw_a2[0]), w_o=bf(rw_w_o[0]),
        head_bd=jnp.asarray(hid[:, None] == hid[None, :], BF16),
    )
    bp, tp, _ = x_prompt.shape
    bs, ts, _ = x_sample.shape
    past_len = page_table.shape[1] * cache_ckv.shape[2]
    heads = d // RW_N
    paged = (cache_ckv.reshape(cache_ckv.shape[1:]), jnp.swapaxes(cache_krope.reshape(cache_krope.shape[1:]), 1, 2),
             page_table)
    cp = _even_mixers(x_prompt, jnp.arange(tp), W, None, None)
    cp = _rwkv_operands(cp, p_prompt, W, jnp.zeros((bp, heads, RW_N, RW_N), F32), jnp.zeros((bp, d), F32))
    cs = _even_mixers(x_sample, past_len + jnp.arange(ts), W, state_hgrn[0], paged)
    y_p, wkv_p = _wkv_scan(*cp['scan_in'], cp['scan_s0'], value_major=cp['batch_lanes'])
    yp, ckv_p, kr_p, hg_p, wkv_p, sh_p = _group_outputs(cp, y_p, wkv_p, W)
    cs['ctx'] = _stacked_ctx_to_rows(_attn_sample(*cs['attn_args']), bs, ts)
    cs = _rwkv_operands(cs, p_sample, W, state_wkv[0], state_shift[0])
    y_s, wkv_s = _wkv_scan(*cs['scan_in'], cs['scan_s0'], value_major=cs['batch_lanes'])
    ys, ckv_s, kr_s, hg_s, wkv_s, sh_s = _group_outputs(cs, y_s, wkv_s, W)
    return (yp, ys, ckv_p, kr_p, ckv_s, kr_s, hg_p, hg_s, wkv_p, wkv_s, sh_p, sh_s)
```
